```python
import jax
import jax.numpy as jnp
from jax import lax
import numpy as np

D_MODEL = 1024
BATCH = 4
SEQ = 8192
DEPTH = 2

GRID_W = 64
CTX_LEN = 256
N_BRANCH = 4
BRANCH_W = 512
MLA_HEADS = 8
MLA_NOPE = 64
MLA_ROPE = 32
MLA_V = 64
MLA_QK = MLA_NOPE + MLA_ROPE
Q_LORA = 256
KV_LORA = 128
Q_BLOCK = 128
ROPE_BASE = 10000.0
NAT_HEADS = 8
NAT_DH = 64
NAT_KH = 8
NAT_KW = 16
HG_HEADS = 4
HG_DK = 128
HG_DV = 128
GLA_HEADS = 4
GLA_DK = 64
GLA_DV = 128
GLA_LOWRANK = 16
GLA_NORMALIZER = 16.0
CHUNK = 64
EPS = 1e-6

W_MLA = Q_LORA + KV_LORA + MLA_ROPE
W_NAT = 3 * NAT_HEADS * NAT_DH
W_HG = 3 * HG_HEADS * HG_DK + HG_HEADS * HG_DV
W_GLA = 2 * GLA_HEADS * GLA_DK + GLA_HEADS * GLA_DV + 2 * GLA_LOWRANK
SEGMENTS = (W_MLA, W_NAT, W_HG, W_GLA, BRANCH_W, BRANCH_W, BRANCH_W, BRANCH_W)
IN_W = W_MLA + W_NAT + W_HG + W_GLA + N_BRANCH * BRANCH_W

kernel_name = 'hybrid_diffusion_mla_nat_hgrn2_gla'


def rms_norm(x, g):
    xf = x.astype(jnp.float32)
    y = xf * lax.rsqrt(jnp.mean(xf * xf, axis=-1, keepdims=True) + EPS)
    return (y * g.astype(jnp.float32)).astype(x.dtype)


def split_cols(t, widths):
    out, start = [], 0
    for w in widths:
        out.append(t[..., start:start + w])
        start += w
    return out


def axial_rope(n_tok, dtype):
    quarter = MLA_ROPE // 4
    inv_freq = ROPE_BASE ** (-jnp.arange(quarter, dtype=jnp.float32) / quarter)
    t = jnp.arange(n_tok, dtype=jnp.int32)
    row = (t // GRID_W).astype(jnp.float32)
    col = (t % GRID_W).astype(jnp.float32)
    ang = jnp.concatenate([row[:, None] * inv_freq, col[:, None] * inv_freq], axis=-1)
    return jnp.cos(ang).astype(dtype), jnp.sin(ang).astype(dtype)


def apply_rope(x, cos, sin):
    x1, x2 = jnp.split(x, 2, axis=-1)
    cs, sn = cos[None, :, None, :], sin[None, :, None, :]
    return jnp.concatenate([x1 * cs - x2 * sn, x1 * sn + x2 * cs], axis=-1)


def softmax_attend(q, k, v, scale):
    s = jnp.einsum('bqhd,bkhd->bhqk', q, k).astype(jnp.float32) * scale
    p = jax.nn.softmax(s, axis=-1).astype(v.dtype)
    return jnp.einsum('bhqk,bkhd->bqhd', p, v)


def joint_attend(q_a, k_a, v_a, q_b, k_b, v_b, scale):
    s_a = jnp.einsum('bqhd,bkhd->bhqk', q_a, k_a).astype(jnp.float32)
    s_b = jnp.einsum('bqhd,bkhd->bhqk', q_b, k_b).astype(jnp.float32)
    p = jax.nn.softmax(jnp.concatenate([s_a, s_b], axis=-1) * scale, axis=-1).astype(v_a.dtype)
    n_a = k_a.shape[1]
    return (jnp.einsum('bhqk,bkhd->bqhd', p[..., :n_a], v_a)
            + jnp.einsum('bhqk,bkhd->bqhd', p[..., n_a:], v_b))


def blocked_joint_attend(q_a, q_b, k_a, v_a, k_b, v_b, scale):
    B, N, H, _ = q_a.shape

    def blocks(t):
        return t.reshape(B, N // Q_BLOCK, Q_BLOCK, H, t.shape[-1]).transpose(1, 0, 2, 3, 4)

    o = lax.map(lambda qs: joint_attend(qs[0], k_a, v_a, qs[1], k_b, v_b, scale), (blocks(q_a), blocks(q_b)))
    return o.transpose(1, 0, 2, 3, 4).reshape(B, N, H, v_a.shape[-1])


def mla_mixer(pl, pc, w_uq, w_ukv, g_cq, g_ckv, g_q, g_k, need_ctx):
    B, n = pl.shape[:2]

    def queries(c_q):
        q = (rms_norm(c_q, g_cq) @ w_uq).reshape(c_q.shape[0], c_q.shape[1], MLA_HEADS, MLA_QK)
        return rms_norm(q, g_q)

    def keys_values(c_kv, k_r):
        b, m = c_kv.shape[:2]
        kv = (rms_norm(c_kv, g_ckv) @ w_ukv).reshape(b, m, MLA_HEADS, MLA_NOPE + MLA_V)
        k_rope = jnp.broadcast_to(k_r[:, :, None, :], (b, m, MLA_HEADS, MLA_ROPE))
        k = jnp.concatenate([kv[..., :MLA_NOPE], k_rope], axis=-1)
        return rms_norm(k, g_k), kv[..., MLA_NOPE:]

    cq_l, ckv_l, kr_l = split_cols(pl, (Q_LORA, KV_LORA, MLA_ROPE))
    cq_c, ckv_c, kr_c = split_cols(pc, (Q_LORA, KV_LORA, MLA_ROPE))
    cos, sin = axial_rope(n, pl.dtype)

    def rotate(t):
        return jnp.concatenate([t[..., :MLA_NOPE], apply_rope(t[..., MLA_NOPE:], cos, sin)], axis=-1)

    q_plain = queries(cq_l)
    k_l, v_l = keys_values(ckv_l, kr_l)
    k_c, v_c = keys_values(ckv_c, kr_c)
    scale = MLA_QK ** -0.5
    o_l = blocked_joint_attend(rotate(q_plain), q_plain, rotate(k_l), v_l, k_c, v_c, scale)
    o_c = softmax_attend(queries(cq_c), k_c, v_c, scale).reshape(B, -1, BRANCH_W) if need_ctx else None
    return o_l.reshape(B, n, BRANCH_W), o_c


def nat_mixer(pl, pc, rpb, g_q, g_k, need_ctx):
    B, n = pl.shape[:2]
    rows = n // GRID_W
    kh, kw = min(NAT_KH, rows), min(NAT_KW, GRID_W)

    def heads(t, g=None):
        t = t.reshape(t.shape[0], t.shape[1], NAT_HEADS, NAT_DH)
        return t if g is None else rms_norm(t, g)

    q_l, k_l, v_l = split_cols(pl, (BRANCH_W,) * 3)
    q_c, k_c, v_c = split_cols(pc, (BRANCH_W,) * 3)
    k_c, v_c = heads(k_c, g_k), heads(v_c)
    grid = (B, rows, GRID_W, NAT_HEADS, NAT_DH)
    kg = heads(k_l, g_k).reshape(grid)
    vg = heads(v_l).reshape(grid)
    qg = heads(q_l, g_q).reshape(grid).transpose(1, 0, 2, 3, 4)
    scale = NAT_DH ** -0.5
    col = jnp.arange(GRID_W)
    c0 = jnp.clip(col - kw // 2, 0, GRID_W - kw)
    col_ok = (col[None, :] >= c0[:, None]) & (col[None, :] < c0[:, None] + kw)
    dc = jnp.clip(col[None, :] - col[:, None], -(NAT_KW - 1), NAT_KW - 1) + NAT_KW - 1

    def row_block(args):
        r, q_r = args
        r0 = jnp.clip(r - kh // 2, 0, rows - kh)
        k_r = lax.dynamic_slice_in_dim(kg, r0, kh, axis=1)
        v_r = lax.dynamic_slice_in_dim(vg, r0, kh, axis=1)
        s_win = jnp.einsum('bqhd,brkhd->bhqrk', q_r, k_r).astype(jnp.float32) * scale
        dr = r0 + jnp.arange(kh) - r + NAT_KH - 1
        bias = rpb[:, dr][:, :, dc].transpose(0, 2, 1, 3)
        s_win = jnp.where(col_ok[None, None, :, None, :], s_win + bias[None].astype(jnp.float32), -jnp.inf)
        s_ctx = jnp.einsum('bqhd,bkhd->bhqk', q_r, k_c).astype(jnp.float32) * scale
        s = jnp.concatenate([s_win.reshape(B, NAT_HEADS, GRID_W, kh * GRID_W), s_ctx], axis=-1)
        p = jax.nn.softmax(s, axis=-1).astype(v_r.dtype)
        p_win = p[..., :kh * GRID_W].reshape(B, NAT_HEADS, GRID_W, kh, GRID_W)
        return (jnp.einsum('bhqrk,brkhd->bqhd', p_win, v_r)
                + jnp.einsum('bhqk,bkhd->bqhd', p[..., kh * GRID_W:], v_c))

    og = lax.map(row_block, (jnp.arange(rows), qg))
    o_l = og.transpose(1, 0, 2, 3, 4).reshape(B, n, BRANCH_W)
    o_c = softmax_attend(heads(q_c, g_q), k_c, v_c, scale).reshape(B, -1, BRANCH_W) if need_ctx else None
    return o_l, o_c


def chunk_scan(q, k, v, log_a, s0):
    B, n, H, _ = k.shape
    dv = v.shape[-1]
    nc = n // CHUNK

    def chunks(t):
        return t.astype(jnp.float32).reshape(B, nc, CHUNK, H, t.shape[-1]).transpose(1, 0, 3, 2, 4)

    lower = jnp.tril(jnp.ones((CHUNK, CHUNK), dtype=bool))[:, :, None]

    def step(S, xs):
        kc, vc, ac = xs[:3]
        b = jnp.cumsum(ac, axis=2)
        b_end = b[:, :, -1:, :]
        S_new = (jnp.exp(b_end[:, :, 0, :])[..., None] * S
                 + jnp.einsum('bhjd,bhjv->bhdv', kc * jnp.exp(b_end - b), vc))
        if len(xs) == 3:
            return S_new, None
        qc = xs[3]
        rel = jnp.exp(jnp.where(lower, b[:, :, :, None, :] - b[:, :, None, :, :], -jnp.inf))
        scores = jnp.einsum('bhid,bhjd,bhijd->bhij', qc, kc, rel)
        o = jnp.einsum('bhij,bhjv->bhiv', scores, vc) + jnp.einsum('bhid,bhdv->bhiv', qc * jnp.exp(b), S)
        return S_new, o

    xs = (chunks(k), chunks(v), chunks(log_a))
    if q is None:
        S, _ = lax.scan(step, s0, xs)
        return None, S
    S, o = lax.scan(step, s0, xs + (chunks(q),))
    return o.transpose(1, 0, 3, 2, 4).reshape(B, n, H, dv).astype(v.dtype), S


def _dirn(t, d):
    if t is None or d == 0:
        return t
    return jnp.flip(t, axis=1)


def bidir_scan(lat, ctx, need_ctx):
    q_l, v_l, k_l, a_l = lat
    q_c, v_c, k_c, a_c = ctx
    B, _, H, dk = k_l[0].shape
    s0 = jnp.zeros((B, H, dk, v_l.shape[-1]), jnp.float32)
    o_lat, o_ctx = None, None
    for d in range(2):
        oc, s_ctx = chunk_scan(_dirn(q_c, d), _dirn(k_c[d], d), _dirn(v_c, d), _dirn(a_c[d], d), s0)
        ol, _ = chunk_scan(_dirn(q_l, d), _dirn(k_l[d], d), _dirn(v_l, d), _dirn(a_l[d], d), s_ctx)
        ol = _dirn(ol, d)
        o_lat = ol if o_lat is None else o_lat + ol
        if need_ctx:
            oc = _dirn(oc, d)
            o_ctx = oc if o_ctx is None else o_ctx + oc
    return o_lat, o_ctx


def hgrn2_mixer(pl, pc, lb, g_o, need_ctx):
    def feats(p, with_q):
        b, m = p.shape[:2]
        shp = (b, m, HG_HEADS, HG_DK)
        q, f_f, f_b, i = split_cols(p, (HG_HEADS * HG_DK,) * 3 + (HG_HEADS * HG_DV,))
        ks, las = [], []
        for d, f in enumerate((f_f, f_b)):
            f = f.astype(jnp.float32)
            lbd = lb[d]
            log_f = jnp.logaddexp(jnp.log(lbd), jnp.log1p(-lbd) + jax.nn.log_sigmoid(f))
            ks.append(((1.0 - lbd) * jax.nn.sigmoid(-f)).reshape(shp))
            las.append(log_f.reshape(shp))
        qq = (jax.nn.silu(q) * HG_DK ** -0.5).reshape(shp) if with_q else None
        return qq, i.reshape(b, m, HG_HEADS, HG_DV), ks, las

    o_l, o_c = bidir_scan(feats(pl, True), feats(pc, need_ctx), need_ctx)
    B, n = pl.shape[:2]
    o_l = rms_norm(o_l, g_o).reshape(B, n, BRANCH_W)
    if need_ctx:
        o_c = rms_norm(o_c, g_o).reshape(B, -1, BRANCH_W)
    return o_l, o_c


def gla_mixer(pl, pc, w2, b2, g_o, need_ctx):
    def feats(p, with_q):
        b, m = p.shape[:2]
        shp = (b, m, GLA_HEADS, GLA_DK)
        q, k, v, r_f, r_b = split_cols(p, (GLA_HEADS * GLA_DK,) * 2 + (GLA_HEADS * GLA_DV,) + (GLA_LOWRANK,) * 2)
        las = [(jax.nn.log_sigmoid((r @ w2[d] + b2[d]).astype(jnp.float32)) / GLA_NORMALIZER).reshape(shp)
               for d, r in enumerate((r_f, r_b))]
        k = k.reshape(shp)
        qq = (q * GLA_DK ** -0.5).reshape(shp) if with_q else None
        return qq, v.reshape(b, m, GLA_HEADS, GLA_DV), (k, k), las

    o_l, o_c = bidir_scan(feats(pl, True), feats(pc, need_ctx), need_ctx)
    B, n = pl.shape[:2]
    o_l = rms_norm(o_l, g_o).reshape(B, n, BRANCH_W)
    if need_ctx:
        o_c = rms_norm(o_c, g_o).reshape(B, -1, BRANCH_W)
    return o_l, o_c


def branch_merge(h, ys, zs, w_br, w_merge, b_merge, w_out):
    acc = None
    for br in range(N_BRANCH):
        gate = jax.nn.sigmoid(h @ w_merge[br] + b_merge[br])
        part = gate * ((ys[br] * jax.nn.silu(zs[br])) @ w_br[br])
        acc = part if acc is None else acc + part
    return acc @ w_out


def setup_inputs(seed: int = 0) -> dict:
    key = jax.random.key(seed)
    ks = iter(jax.random.split(key, 32))

    def nrm(shape, std):
        return jax.random.normal(next(ks), shape, jnp.float32) * std

    def gain(shape):
        return 1.0 + nrm(shape, 0.02)

    L, D = DEPTH, D_MODEL
    return {
        'x': nrm((BATCH, SEQ, D), 1.0),
        'c': nrm((BATCH, D), 1.0),
        'ctx': nrm((BATCH, CTX_LEN, D), 1.0),
        'c_ctx': nrm((D,), 1.0),
        'ada_w': nrm((L, D, 3 * D), 0.5 * D ** -0.5),
        'ada_b': nrm((L, 3 * D), 0.02),
        'norm_g': gain((L, D)),
        'w_in': nrm((L, D, IN_W), D ** -0.5),
        'mla_w_uq': nrm((L, Q_LORA, MLA_HEADS * MLA_QK), Q_LORA ** -0.5),
        'mla_w_ukv': nrm((L, KV_LORA, MLA_HEADS * (MLA_NOPE + MLA_V)), KV_LORA ** -0.5),
        'mla_g_cq': gain((L, Q_LORA)),
        'mla_g_ckv': gain((L, KV_LORA)),
        'mla_g_q': gain((L, MLA_QK)),
        'mla_g_k': gain((L, MLA_QK)),
        'nat_rpb': nrm((L, NAT_HEADS, 2 * NAT_KH - 1, 2 * NAT_KW - 1), 0.2),
        'nat_g_q': gain((L, NAT_DH)),
        'nat_g_k': gain((L, NAT_DH)),
        'hg_lb_logits': nrm((L, 2, HG_HEADS * HG_DK), 1.0),
        'hg_g_o': gain((L, HG_DV)),
        'gla_w2': nrm((L, 2, GLA_LOWRANK, GLA_HEADS * GLA_DK), GLA_LOWRANK ** -0.5),
        'gla_b2': nrm((L, 2, GLA_HEADS * GLA_DK), 0.1),
        'gla_g_o': gain((L, GLA_DV)),
        'w_br': nrm((L, N_BRANCH, BRANCH_W, D), BRANCH_W ** -0.5),
        'w_merge': nrm((L, N_BRANCH, D, D), D ** -0.5),
        'b_merge': nrm((L, N_BRANCH, D), 0.02),
        'w_out': nrm((L, D, D), D ** -0.5),
    }


def reference(x, c, ctx, c_ctx, ada_w, ada_b, norm_g, w_in, mla_w_uq, mla_w_ukv, mla_g_cq, mla_g_ckv, mla_g_q,
              mla_g_k, nat_rpb, nat_g_q, nat_g_k, hg_lb_logits, hg_g_o, gla_w2, gla_b2, gla_g_o, w_br, w_merge,
              b_merge, w_out):
    sm = jax.nn.softmax(hg_lb_logits.astype(jnp.float32), axis=0)
    lower_bounds = jnp.maximum(jnp.cumsum(sm, axis=0) - sm[0], 0.0)
    cx = ctx
    for l in range(DEPTH):
        need_ctx = l < DEPTH - 1
        sh, sc, gt = jnp.split(jax.nn.silu(c) @ ada_w[l] + ada_b[l], 3, axis=-1)
        sh_c, sc_c, gt_c = jnp.split(jax.nn.silu(c_ctx) @ ada_w[l] + ada_b[l], 3, axis=-1)
        h = rms_norm(x, norm_g[l]) * (1.0 + sc[:, None]) + sh[:, None]
        hc = rms_norm(cx, norm_g[l]) * (1.0 + sc_c) + sh_c
        seg_l = split_cols(h @ w_in[l], SEGMENTS)
        seg_c = split_cols(hc @ w_in[l], SEGMENTS)
        out_a = mla_mixer(seg_l[0], seg_c[0], mla_w_uq[l], mla_w_ukv[l], mla_g_cq[l], mla_g_ckv[l],
                          mla_g_q[l], mla_g_k[l], need_ctx)
        out_b = nat_mixer(seg_l[1], seg_c[1], nat_rpb[l], nat_g_q[l], nat_g_k[l], need_ctx)
        out_c = hgrn2_mixer(seg_l[2], seg_c[2], lower_bounds[l], hg_g_o[l], need_ctx)
        out_d = gla_mixer(seg_l[3], seg_c[3], gla_w2[l], gla_b2[l], gla_g_o[l], need_ctx)
        outs = (out_a, out_b, out_c, out_d)
        x = x + gt[:, None] * branch_merge(h, [o[0] for o in outs], seg_l[4:], w_br[l], w_merge[l],
                                           b_merge[l], w_out[l])
        if need_ctx:
            cx = cx + gt_c * branch_merge(hc, [o[1] for o in outs], seg_c[4:], w_br[l], w_merge[l],
                                          b_merge[l], w_out[l])
    return x
```

```python
import functools

import jax
import jax.numpy as jnp
import numpy as np
from jax import lax
from jax.experimental import pallas as pl
from jax.experimental.pallas import tpu as pltpu

F32 = jnp.float32
BF16 = jnp.bfloat16

GRID_W = 64
N_BRANCH = 4
BRANCH_W = 512
MLA_HEADS = 8
MLA_NOPE = 64
MLA_ROPE = 32
MLA_V = 64
MLA_QK = MLA_NOPE + MLA_ROPE
Q_LORA = 256
KV_LORA = 128
ROPE_BASE = 10000.0
NAT_HEADS = 8
NAT_DH = 64
NAT_KH = 8
NAT_KW = 16
HG_HEADS = 4
HG_DK = 128
GLA_HEADS = 4
GLA_DK = 64
GLA_LOWRANK = 16
GLA_NORMALIZER = 16.0
EPS = 1e-6

W_MLA = Q_LORA + KV_LORA + MLA_ROPE
W_NAT = 3 * BRANCH_W
W_HG = 4 * BRANCH_W
W_GLA = 2 * GLA_HEADS * GLA_DK + BRANCH_W + 2 * GLA_LOWRANK

LANES = 128
HEAD_PAD = 128
TOK_BLK = 256

COL_HG = 0
COL_Z = COL_HG + W_HG
COL_NAT = COL_Z + N_BRANCH * BRANCH_W
COL_GLA = COL_NAT + W_NAT
COL_MLA = COL_GLA + 1024
COL_GLR = COL_MLA + 512
IN_W_PAD = COL_GLR + LANES
MM_COL_BLK = IN_W_PAD // 3
MM_ROW_BLK = 512

SUB = 16
CHUNK = 64
EXP_CAP = 80.0
NEG_BIG = -1e30


def _cparams(sem, vmem_mb=48):
    return pltpu.CompilerParams(dimension_semantics=sem, vmem_limit_bytes=vmem_mb * 1024 * 1024)


def _dot(a, b):
    return jnp.dot(a, b, preferred_element_type=F32)


def _dot_nt(a, b):
    return lax.dot_general(a, b, (((1,), (1,)), ((), ())), preferred_element_type=F32)


def _ada_kernel(c_ref, w_ref, b_ref, o_ref):
    c = c_ref[...]
    a = (c * jax.nn.sigmoid(c)).astype(BF16)
    o_ref[...] = _dot(a, w_ref[...]) + b_ref[...]


def _ada_vectors(cc, ada_w, ada_b):
    L, D, D3 = ada_w.shape
    tn = 1024
    return pl.pallas_call(
        _ada_kernel,
        out_shape=jax.ShapeDtypeStruct((L, cc.shape[0], D3), F32),
        grid=(L, D3 // tn),
        in_specs=[
            pl.BlockSpec((cc.shape[0], D), lambda l, j: (0, 0)),
            pl.BlockSpec((None, D, tn), lambda l, j: (l, 0, j)),
            pl.BlockSpec((None, 1, tn), lambda l, j: (l, 0, j)),
        ],
        out_specs=pl.BlockSpec((None, cc.shape[0], tn), lambda l, j: (l, 0, j)),
        compiler_params=_cparams(("parallel", "parallel")),
        name="ada_vectors",
    )(cc, ada_w, ada_b.reshape(L, 1, D3))


def _normmod_kernel(x_ref, g_ref, m_ref, h_ref):
    x = x_ref[...]
    y = x * lax.rsqrt(jnp.mean(x * x, axis=-1, keepdims=True) + EPS) * g_ref[...]
    h_ref[...] = (y * (1.0 + m_ref[1:2, :]) + m_ref[0:1, :]).astype(h_ref.dtype)


def _norm_modulate(xs, g, modsel, n_lat_blk):
    B, Nt, D = xs.shape
    return pl.pallas_call(
        _normmod_kernel,
        out_shape=jax.ShapeDtypeStruct((B, Nt, D), BF16),
        grid=(B, Nt // TOK_BLK),
        in_specs=[
            pl.BlockSpec((None, TOK_BLK, D), lambda b, i: (b, i, 0)),
            pl.BlockSpec((1, D), lambda b, i: (0, 0)),
            pl.BlockSpec((None, None, 3, D), lambda b, i: (b, jnp.where(i >= n_lat_blk, 1, 0), 0, 0)),
        ],
        out_specs=pl.BlockSpec((None, TOK_BLK, D), lambda b, i: (b, i, 0)),
        compiler_params=_cparams(("parallel", "parallel")),
        name="norm_modulate",
    )(xs, g.reshape(1, D), modsel)


def _mm_kernel(a_ref, w_ref, o_ref):
    o_ref[...] = _dot(a_ref[...], w_ref[...]).astype(o_ref.dtype)


def _in_projection(h2, w, out_dtype):
    M, K = h2.shape
    N = w.shape[1]
    return pl.pallas_call(
        _mm_kernel,
        out_shape=jax.ShapeDtypeStruct((M, N), out_dtype),
        grid=(N // MM_COL_BLK, M // MM_ROW_BLK),
        in_specs=[
            pl.BlockSpec((MM_ROW_BLK, K), lambda j, i: (i, 0)),
            pl.BlockSpec((K, MM_COL_BLK), lambda j, i: (0, j)),
        ],
        out_specs=pl.BlockSpec((MM_ROW_BLK, MM_COL_BLK), lambda j, i: (i, j)),
        compiler_params=_cparams(("parallel", "parallel")),
        name="in_projection",
    )(h2, w)


def _pack_w_in(w_in_l):
    o_nat = W_MLA
    o_hg = o_nat + W_NAT
    o_gla = o_hg + W_HG
    o_z = o_gla + W_GLA
    D = w_in_l.shape[0]
    zeros = lambda n: jnp.zeros((D, n), w_in_l.dtype)
    cols = [
        w_in_l[:, o_hg:o_hg + W_HG],
        w_in_l[:, o_z:o_z + N_BRANCH * BRANCH_W],
        w_in_l[:, o_nat:o_nat + W_NAT],
        w_in_l[:, o_gla:o_gla + 1024],
        w_in_l[:, :W_MLA], zeros(512 - W_MLA),
        w_in_l[:, o_gla + 1024:o_gla + W_GLA], zeros(LANES - 2 * GLA_LOWRANK),
    ]
    return jnp.concatenate(cols, axis=1).astype(BF16)


def _mla_prep_kernel(p_ref, wq_ref, wk_ref, wv_ref, gcq_ref, gckv_ref, gq_ref, gk_ref, vone_ref,
                     c_ref, s1_ref, s2_ref, qp_ref, qr_ref, k_ref, v_ref):
    p = p_ref[...].astype(F32)
    cq = p[:, :Q_LORA]
    r = lax.rsqrt(jnp.mean(cq * cq, axis=-1, keepdims=True) + EPS)
    cqn = (cq * r * gcq_ref[...]).astype(BF16)
    qraw = _dot(cqn, wq_ref[...])
    kv = p[:, Q_LORA:]
    ckv = kv[:, :KV_LORA]
    r2 = lax.rsqrt(jnp.mean(ckv * ckv, axis=-1, keepdims=True) + EPS)
    lane = lax.broadcasted_iota(jnp.int32, kv.shape, 1)
    lhs = (kv * jnp.where(lane < KV_LORA, r2, 1.0) * gckv_ref[...]).astype(BF16)
    kraw = _dot(lhs, wk_ref[...])
    vraw = _dot(lhs, wv_ref[...]) + vone_ref[...]
    cs = c_ref[...]
    s1 = s1_ref[...]
    s2 = s2_ref[...]

    def rope(t):
        return t * cs + pltpu.roll(t, HEAD_PAD - MLA_ROPE // 2, 1) * s1 + pltpu.roll(t, MLA_ROPE // 2, 1) * s2

    inv = 1.0 / MLA_QK
    for hd in range(MLA_HEADS):
        sl = slice(HEAD_PAD * hd, HEAD_PAD * (hd + 1))
        qh = qraw[:, sl]
        qn = qh * lax.rsqrt(jnp.sum(qh * qh, axis=-1, keepdims=True) * inv + EPS) * gq_ref[...]
        qp_ref[hd] = qn.astype(qp_ref.dtype)
        qr_ref[hd] = rope(qn).astype(qr_ref.dtype)
        kh = kraw[:, sl]
        kn = kh * lax.rsqrt(jnp.sum(kh * kh, axis=-1, keepdims=True) * inv + EPS) * gk_ref[...]
        k_ref[hd] = rope(kn).astype(k_ref.dtype)
        v_ref[hd] = vraw[:, sl].astype(v_ref.dtype)


def _mla_weights(w_uq, w_ukv, g_cq, g_ckv, g_q, g_k):
    H = MLA_HEADS
    wq = jnp.pad(w_uq.reshape(Q_LORA, H, MLA_QK), ((0, 0), (0, 0), (0, HEAD_PAD - MLA_QK)))
    wq = wq.reshape(Q_LORA, H * HEAD_PAD).astype(BF16)
    wkv = w_ukv.reshape(KV_LORA, H, MLA_NOPE + MLA_V)
    wk = jnp.pad(wkv[:, :, :MLA_NOPE], ((0, 0), (0, 0), (0, HEAD_PAD - MLA_NOPE)))
    place = jnp.zeros((MLA_ROPE, H, HEAD_PAD), F32)
    place = place.at[jnp.arange(MLA_ROPE), :, MLA_NOPE + jnp.arange(MLA_ROPE)].set(1.0)
    wk = jnp.concatenate([wk, place, jnp.zeros((256 - KV_LORA - MLA_ROPE, H, HEAD_PAD), F32)], axis=0)
    wk = wk.reshape(256, H * HEAD_PAD).astype(BF16)
    wv = jnp.pad(wkv[:, :, MLA_NOPE:], ((0, 256 - KV_LORA), (0, 0), (0, HEAD_PAD - MLA_V)))
    wv = wv.reshape(256, H * HEAD_PAD).astype(BF16)
    vone = jnp.zeros((H, HEAD_PAD), F32).at[:, MLA_V].set(1.0).reshape(1, H * HEAD_PAD)
    gckv = jnp.concatenate([g_ckv, jnp.ones((256 - KV_LORA,), F32)]).reshape(1, 256)
    gq = jnp.pad(g_q * (MLA_QK ** -0.5), (0, HEAD_PAD - MLA_QK)).reshape(1, HEAD_PAD)
    gk = jnp.pad(g_k, (0, HEAD_PAD - MLA_QK)).reshape(1, HEAD_PAD)
    return wq, wk, wv, g_cq.reshape(1, Q_LORA), gckv, gq, gk, vone


def _rope_tables(n_lat, n_ctx):
    quarter = MLA_ROPE // 4
    inv_freq = ROPE_BASE ** (-jnp.arange(quarter, dtype=F32) / quarter)
    t = jnp.arange(n_lat, dtype=jnp.int32)
    row = (t // GRID_W).astype(F32)
    col = (t % GRID_W).astype(F32)
    ang = jnp.concatenate([row[:, None] * inv_freq, col[:, None] * inv_freq], axis=-1)
    cos, sin = jnp.cos(ang), jnp.sin(ang)
    half = MLA_ROPE // 2
    ones = jnp.ones((n_lat, MLA_NOPE), F32)
    tail = jnp.ones((n_lat, HEAD_PAD - MLA_QK), F32)
    zl = jnp.zeros((n_lat, MLA_NOPE), F32)
    zh = jnp.zeros((n_lat, half), F32)
    zt = jnp.zeros((n_lat, HEAD_PAD - MLA_QK), F32)
    c = jnp.concatenate([ones, cos, cos, tail], axis=1)
    s1 = jnp.concatenate([zl, -sin, zh, zt], axis=1)
    s2 = jnp.concatenate([zl, zh, sin, zt], axis=1)
    c = jnp.concatenate([c, jnp.ones((n_ctx, HEAD_PAD), F32)], axis=0)
    s1 = jnp.concatenate([s1, jnp.zeros((n_ctx, HEAD_PAD), F32)], axis=0)
    s2 = jnp.concatenate([s2, jnp.zeros((n_ctx, HEAD_PAD), F32)], axis=0)
    return c, s1, s2


def _mla_prep(p3, mw, tabs):
    B, Nt, _ = p3.shape
    H = MLA_HEADS
    wq, wk, wv, gcq, gckv, gq, gk, vone = mw
    const = lambda shape: pl.BlockSpec(shape, lambda b, i: (0,) * len(shape))
    tab = pl.BlockSpec((TOK_BLK, HEAD_PAD), lambda b, i: (i, 0))
    out = pl.BlockSpec((None, H, TOK_BLK, HEAD_PAD), lambda b, i: (b, 0, i, 0))
    shp = jax.ShapeDtypeStruct((B, H, Nt, HEAD_PAD), BF16)
    return pl.pallas_call(
        _mla_prep_kernel,
        out_shape=(shp, shp, shp, shp),
        grid=(B, Nt // TOK_BLK),
        in_specs=[
            pl.BlockSpec((None, TOK_BLK, 512), lambda b, i: (b, i, COL_MLA // 512)),
            const(wq.shape), const(wk.shape), const(wv.shape), const(gcq.shape), const(gckv.shape),
            const(gq.shape), const(gk.shape), const(vone.shape), tab, tab, tab,
        ],
        out_specs=(out, out, out, out),
        compiler_params=_cparams(("parallel", "parallel")),
        name="mla_prep",
    )(p3, wq, wk, wv, gcq, gckv, gq, gk, vone, *tabs)


MLA_TQ = 512
MLA_TK = 512


def _softmax_step(q, kc, vc, m, acc):
    s = _dot_nt(q, kc)
    m_new = jnp.maximum(m, jnp.max(s, axis=-1, keepdims=True))
    p = jnp.exp(s - m_new)
    acc = jnp.exp(m - m_new) * acc + _dot(p.astype(BF16), vc)
    return m_new, acc


def _mla_attn_kernel(qr_ref, qp_ref, k_ref, v_ref, o_ref, *, n_lat):
    tq = qr_ref.shape[1]
    outs = []
    for hh in range(2):
        qr = qr_ref[hh]

        def body(c, carry, hh=hh, qr=qr):
            off = pl.multiple_of(c * MLA_TK, MLA_TK)
            return _softmax_step(qr, k_ref[hh, pl.ds(off, MLA_TK), :], v_ref[hh, pl.ds(off, MLA_TK), :], *carry)

        m0 = jnp.full((tq, 1), NEG_BIG, F32)
        a0 = jnp.zeros((tq, HEAD_PAD), F32)
        m, acc = lax.fori_loop(0, n_lat // MLA_TK, body, (m0, a0))
        m, acc = _softmax_step(qp_ref[hh], k_ref[hh, n_lat:, :], v_ref[hh, n_lat:, :], m, acc)
        outs.append(acc[:, :MLA_V] / acc[:, MLA_V:MLA_V + 1])
    o_ref[...] = jnp.concatenate(outs, axis=-1).astype(o_ref.dtype)


def _mla_ctx_kernel(qp_ref, k_ref, v_ref, o_ref):
    outs = []
    for hh in range(2):
        s = _dot_nt(qp_ref[hh], k_ref[hh])
        p = jnp.exp(s - jnp.max(s, axis=-1, keepdims=True))
        acc = _dot(p.astype(BF16), v_ref[hh])
        outs.append(acc[:, :MLA_V] / acc[:, MLA_V:MLA_V + 1])
    o_ref[...] = jnp.concatenate(outs, axis=-1).astype(o_ref.dtype)


def _mla_attention(qp, qr, k, v, n_lat, need_ctx):
    B, H, Nt, _ = k.shape
    n_ctx = Nt - n_lat
    qspec = pl.BlockSpec((None, 2, MLA_TQ, HEAD_PAD), lambda b, g, i: (b, g, i, 0))
    kvspec = pl.BlockSpec((None, 2, Nt, HEAD_PAD), lambda b, g, i: (b, g, 0, 0))
    o_lat = pl.pallas_call(
        functools.partial(_mla_attn_kernel, n_lat=n_lat),
        out_shape=jax.ShapeDtypeStruct((B, n_lat, BRANCH_W), BF16),
        grid=(B, H // 2, n_lat // MLA_TQ),
        in_specs=[qspec, qspec, kvspec, kvspec],
        out_specs=pl.BlockSpec((None, MLA_TQ, 2 * MLA_V), lambda b, g, i: (b, i, g)),
        compiler_params=_cparams(("parallel", "parallel", "arbitrary")),
        name="mla_attention",
    )(qr, qp, k, v)
    if not need_ctx:
        return o_lat
    cblk = n_lat // n_ctx
    cspec = pl.BlockSpec((None, 2, n_ctx, HEAD_PAD), lambda b, g: (b, g, cblk, 0))
    o_ctx = pl.pallas_call(
        _mla_ctx_kernel,
        out_shape=jax.ShapeDtypeStruct((B, n_ctx, BRANCH_W), BF16),
        grid=(B, H // 2),
        in_specs=[cspec, cspec, cspec],
        out_specs=pl.BlockSpec((None, n_ctx, 2 * MLA_V), lambda b, g: (b, 0, g)),
        compiler_params=_cparams(("parallel", "parallel")),
        name="mla_attention_ctx",
    )(qp, k, v)
    return jnp.concatenate([o_lat, o_ctx], axis=1)


NAT_ROWS_BLK = 8
NAT_TQ = NAT_ROWS_BLK * GRID_W
NAT_WIN_ROWS = 16
NAT_WIN = NAT_WIN_ROWS * GRID_W
NAT_PREP_BLK = 1024


def _pair_norm(x, g, half0):
    x2 = x * x
    s0 = jnp.sum(jnp.where(half0, x2, 0.0), axis=-1, keepdims=True)
    s1 = jnp.sum(jnp.where(half0, 0.0, x2), axis=-1, keepdims=True)
    ms = jnp.where(half0, s0, s1) * (1.0 / NAT_DH)
    return x * lax.rsqrt(ms + EPS) * g


def _nat_kernel(q_ref, kl_ref, vl_ref, kc_ref, vc_ref, bias_ref, gq_ref, gk_ref, o_ref, kn_ref, vb_ref,
                *, n_lat, rows_total):
    rb = pl.program_id(2)
    half0 = lax.broadcasted_iota(jnp.int32, (1, LANES), 1) < NAT_DH

    @pl.when(rb == 0)
    def _():
        def body(c, carry):
            off = pl.multiple_of(c * NAT_PREP_BLK, NAT_PREP_BLK)
            kn_ref[pl.ds(off, NAT_PREP_BLK), :] = _pair_norm(
                kl_ref[pl.ds(off, NAT_PREP_BLK), :].astype(F32), gk_ref[...], half0).astype(BF16)
            vb_ref[pl.ds(off, NAT_PREP_BLK), :] = vl_ref[pl.ds(off, NAT_PREP_BLK), :].astype(BF16)
            return carry

        lax.fori_loop(0, n_lat // NAT_PREP_BLK, body, 0)
        kn_ref[n_lat:, :] = _pair_norm(kc_ref[...].astype(F32), gk_ref[...], half0).astype(BF16)
        vb_ref[n_lat:, :] = vc_ref[...].astype(BF16)

    q = _pair_norm(q_ref[...].astype(F32), gq_ref[...], half0)
    ws = jnp.clip(rb * NAT_ROWS_BLK - NAT_KH // 2, 0, rows_total - NAT_WIN_ROWS)
    off = pl.multiple_of(ws * GRID_W, GRID_W)
    kw = kn_ref[pl.ds(off, NAT_WIN), :]
    vw = vb_ref[pl.ds(off, NAT_WIN), :]
    kc = kn_ref[n_lat:, :]
    vc = vb_ref[n_lat:, :]
    outs = []
    for hh in range(2):
        sel = half0 if hh == 0 else jnp.logical_not(half0)
        qm = jnp.where(sel, q, 0.0).astype(BF16)
        sw = _dot_nt(qm, kw) + bias_ref[hh].astype(F32)
        sc = _dot_nt(qm, kc)
        m = jnp.maximum(jnp.max(sw, axis=-1, keepdims=True), jnp.max(sc, axis=-1, keepdims=True))
        pw = jnp.exp(sw - m)
        pc = jnp.exp(sc - m)
        l = jnp.sum(pw, axis=-1, keepdims=True) + jnp.sum(pc, axis=-1, keepdims=True)
        o = _dot(pw.astype(BF16), vw) + _dot(pc.astype(BF16), vc)
        outs.append(o / l)
    o_ref[...] = jnp.where(half0, outs[0], outs[1]).astype(o_ref.dtype)


def _nat_ctx_kernel(q_ref, kc_ref, vc_ref, gq_ref, gk_ref, o_ref):
    half0 = lax.broadcasted_iota(jnp.int32, (1, LANES), 1) < NAT_DH
    q = _pair_norm(q_ref[...].astype(F32), gq_ref[...], half0)
    kc = _pair_norm(kc_ref[...].astype(F32), gk_ref[...], half0).astype(BF16)
    vc = vc_ref[...].astype(BF16)
    outs = []
    for hh in range(2):
        sel = half0 if hh == 0 else jnp.logical_not(half0)
        s = _dot_nt(jnp.where(sel, q, 0.0).astype(BF16), kc)
        p = jnp.exp(s - jnp.max(s, axis=-1, keepdims=True))
        outs.append(_dot(p.astype(BF16), vc) / jnp.sum(p, axis=-1, keepdims=True))
    o_ref[...] = jnp.where(half0, outs[0], outs[1]).astype(o_ref.dtype)


def _nat_bias_tables(rpb, rows_total):
    rbs = np.array([0, min(NAT_ROWS_BLK, rows_total - NAT_ROWS_BLK), rows_total - NAT_ROWS_BLK])
    ws = np.clip(rbs - NAT_KH // 2, 0, rows_total - NAT_WIN_ROWS)
    qi = np.arange(NAT_TQ) // GRID_W
    qc = np.arange(NAT_TQ) % GRID_W
    kk = np.arange(NAT_WIN) // GRID_W
    kc = np.arange(NAT_WIN) % GRID_W
    qrow = rbs[:, None] + qi[None, :]
    r0 = np.clip(qrow - NAT_KH // 2, 0, rows_total - NAT_KH)
    krow = ws[:, None] + kk[None, :]
    row_ok = (krow[:, None, :] >= r0[:, :, None]) & (krow[:, None, :] < r0[:, :, None] + NAT_KH)
    dr = krow[:, None, :] - qrow[:, :, None] + NAT_KH - 1
    c0 = np.clip(qc - NAT_KW // 2, 0, GRID_W - NAT_KW)
    col_ok = (kc[None, :] >= c0[:, None]) & (kc[None, :] < c0[:, None] + NAT_KW)
    dc = np.clip(kc[None, :] - qc[:, None], -(NAT_KW - 1), NAT_KW - 1) + NAT_KW - 1
    ok = row_ok & col_ok[None]
    dr = np.clip(dr, 0, 2 * NAT_KH - 2)
    tab = rpb[:, dr, np.broadcast_to(dc[None], dr.shape)]
    tab = jnp.where(jnp.asarray(ok)[None], tab, NEG_BIG).astype(BF16)
    tab = tab.transpose(1, 0, 2, 3)
    return tab.reshape(3, NAT_HEADS // 2, 2, NAT_TQ, NAT_WIN)


def _nat_attention(p3, rpb, g_q, g_k, n_lat, need_ctx):
    B, Nt, _ = p3.shape
    n_ctx = Nt - n_lat
    rows_total = n_lat // GRID_W
    n_rb = rows_total // NAT_ROWS_BLK
    bias = _nat_bias_tables(rpb, rows_total)
    gq = jnp.tile(g_q * (NAT_DH ** -0.5), 2).reshape(1, LANES)
    gk = jnp.tile(g_k, 2).reshape(1, LANES)
    cq, ck, cv = ((COL_NAT + j * BRANCH_W) // LANES for j in range(3))
    cblk = n_lat // n_ctx

    def variant(rb):
        return jnp.where(rb == 0, 0, jnp.where(rb == n_rb - 1, 2, 1))

    vec = pl.BlockSpec((1, LANES), lambda b, g, rb: (0, 0))
    o_lat = pl.pallas_call(
        functools.partial(_nat_kernel, n_lat=n_lat, rows_total=rows_total),
        out_shape=jax.ShapeDtypeStruct((B, n_lat, BRANCH_W), BF16),
        grid=(B, NAT_HEADS // 2, n_rb),
        in_specs=[
            pl.BlockSpec((None, NAT_TQ, LANES), lambda b, g, rb: (b, rb, cq + g)),
            pl.BlockSpec((None, n_lat, LANES), lambda b, g, rb: (b, 0, ck + g)),
            pl.BlockSpec((None, n_lat, LANES), lambda b, g, rb: (b, 0, cv + g)),
            pl.BlockSpec((None, n_ctx, LANES), lambda b, g, rb: (b, cblk, ck + g)),
            pl.BlockSpec((None, n_ctx, LANES), lambda b, g, rb: (b, cblk, cv + g)),
            pl.BlockSpec((None, None, 2, NAT_TQ, NAT_WIN), lambda b, g, rb: (variant(rb), g, 0, 0, 0)),
            vec, vec,
        ],
        out_specs=pl.BlockSpec((None, NAT_TQ, LANES), lambda b, g, rb: (b, rb, g)),
        scratch_shapes=[pltpu.VMEM((Nt, LANES), BF16), pltpu.VMEM((Nt, LANES), BF16)],
        compiler_params=_cparams(("parallel", "parallel", "arbitrary")),
        name="nat_attention",
    )(p3, p3, p3, p3, p3, bias, gq, gk)
    if not need_ctx:
        return o_lat
    vec2 = pl.BlockSpec((1, LANES), lambda b, g: (0, 0))
    o_ctx = pl.pallas_call(
        _nat_ctx_kernel,
        out_shape=jax.ShapeDtypeStruct((B, n_ctx, BRANCH_W), BF16),
        grid=(B, NAT_HEADS // 2),
        in_specs=[
            pl.BlockSpec((None, n_ctx, LANES), lambda b, g: (b, cblk, cq + g)),
            pl.BlockSpec((None, n_ctx, LANES), lambda b, g: (b, cblk, ck + g)),
            pl.BlockSpec((None, n_ctx, LANES), lambda b, g: (b, cblk, cv + g)),
            vec2, vec2,
        ],
        out_specs=pl.BlockSpec((None, n_ctx, LANES), lambda b, g: (b, 0, g)),
        compiler_params=_cparams(("parallel", "parallel")),
        name="nat_attention_ctx",
    )(p3, p3, p3, gq, gk)
    return jnp.concatenate([o_lat, o_ctx], axis=1)


def _log_sigmoid(x):
    return jnp.minimum(x, 0.0) - jnp.log1p(jnp.exp(-jnp.abs(x)))


def _split3(a):
    hi = a.astype(BF16)
    r1 = a - hi.astype(F32)
    mid = r1.astype(BF16)
    lo = (r1 - mid.astype(F32)).astype(BF16)
    return hi, mid, lo


def _scan_chunk(qs, k, v, a, s_ref, rev, heads):
    C = qs.shape[0]
    row = lax.broadcasted_iota(jnp.int32, (C, C), 0)
    col = lax.broadcasted_iota(jnp.int32, (C, C), 1)
    causal = (row <= col) if rev else (row >= col)
    tri = jnp.where(causal, 1.0, 0.0).astype(BF16)
    hi, mid, lo = _split3(a)
    bc = _dot(tri, hi) + _dot(tri, mid) + _dot(tri, lo)
    bex = bc - a
    nsub = C // SUB
    refs = []
    for i in range(nsub):
        r = SUB * i + (SUB - 1 if rev else 0)
        refs.append(bex[r:r + 1, :])
    rfull = jnp.concatenate([jnp.broadcast_to(r, (SUB, r.shape[1])) for r in refs], axis=0)
    qt = qs * jnp.exp(bc - rfull)
    q_in = qs * jnp.exp(bc)
    e = 0 if rev else C - 1
    bend = bc[e:e + 1, :]
    k_out = k * jnp.exp(bend - bc)
    dec = jnp.exp(bend)
    kts = [k * jnp.exp(jnp.minimum(r - bc, EXP_CAP)) for r in refs]
    outs = []
    for hd, (ks, kmask, vs) in enumerate(heads):
        ksl = slice(ks, ks + LANES)
        vsl = slice(vs, vs + LANES)

        def msk(t, kmask=kmask):
            return t if kmask is None else jnp.where(kmask, t, 0.0)

        qt_h = msk(qt[:, ksl]).astype(BF16)
        blocks = [_dot_nt(qt_h[SUB * i:SUB * (i + 1), :], kts[i][:, ksl].astype(BF16)) for i in range(nsub)]
        sc = jnp.where(causal, jnp.concatenate(blocks, axis=0), 0.0).astype(BF16)
        vh = v[:, vsl].astype(BF16)
        st = s_ref[hd]
        o = _dot(sc, vh) + _dot_nt(msk(q_in[:, ksl]).astype(BF16), st.astype(BF16))
        s_ref[hd] = st * dec[:, ksl] + _dot(v[:, vsl].T.astype(BF16), msk(k_out[:, ksl]).astype(BF16))
        outs.append(o)
    return jnp.concatenate(outs, axis=-1)


def _run_block(qs, k, v, a, s_ref, o_ref, rev, heads):
    n_ch = qs.shape[0] // CHUNK
    order = range(n_ch - 1, -1, -1) if rev else range(n_ch)
    for c in order:
        rs = slice(CHUNK * c, CHUNK * (c + 1))
        o_ref[rs, :] = _scan_chunk(qs[rs], k[rs], v[rs], a[rs], s_ref, rev, heads).astype(o_ref.dtype)


def _hgrn2_kernel(q_ref, f_ref, v_ref, lb_ref, o_ref, s_ref, *, rev):
    @pl.when(pl.program_id(1) == 0)
    def _():
        s_ref[...] = jnp.zeros_like(s_ref)

    q = q_ref[...].astype(F32)
    f = f_ref[...].astype(F32)
    loglb = lb_ref[0:1, :]
    log1mlb = lb_ref[1:2, :]
    oneml = lb_ref[2:3, :]
    u = log1mlb + _log_sigmoid(f)
    a = jnp.maximum(u, loglb) + jnp.log1p(jnp.exp(-jnp.abs(u - loglb)))
    k = oneml * jax.nn.sigmoid(-f)
    qs = q * jax.nn.sigmoid(q) * (HG_DK ** -0.5)
    heads = [(LANES * h, None, LANES * h) for h in range(HG_HEADS)]
    _run_block(qs, k, v_ref[...].astype(F32), a, s_ref, o_ref, rev, heads)


def _gla_kernel(q_ref, k_ref, v_ref, r_ref, w2_ref, b2_ref, o_ref, s_ref, *, rev):
    @pl.when(pl.program_id(1) == 0)
    def _():
        s_ref[...] = jnp.zeros_like(s_ref)

    g = _dot(r_ref[...].astype(BF16), w2_ref[...]) + b2_ref[...]
    a = _log_sigmoid(g) * (1.0 / GLA_NORMALIZER)
    qs = q_ref[...].astype(F32) * (GLA_DK ** -0.5)
    half0 = lax.broadcasted_iota(jnp.int32, (1, LANES), 1) < GLA_DK
    heads = [(LANES * (h // 2), half0 if h % 2 == 0 else jnp.logical_not(half0), LANES * h)
             for h in range(GLA_HEADS)]
    _run_block(qs, k_ref[...].astype(F32), v_ref[...].astype(F32), a, s_ref, o_ref, rev, heads)


def _scan_order(rev, n_lat_blk):
    if rev:
        return lambda s: jnp.where(s == 0, n_lat_blk, n_lat_blk - s)
    return lambda s: jnp.where(s == 0, n_lat_blk, s - 1)


def _hgrn2_scan(p3, lbvec, rev, n_lat):
    B, Nt, _ = p3.shape
    nb = Nt // TOK_BLK
    blk = _scan_order(rev, n_lat // TOK_BLK)
    c0 = COL_HG // BRANCH_W
    cf = c0 + (2 if rev else 1)
    spec = lambda cidx: pl.BlockSpec((None, TOK_BLK, BRANCH_W), lambda b, s: (b, blk(s), cidx))
    return pl.pallas_call(
        functools.partial(_hgrn2_kernel, rev=rev),
        out_shape=jax.ShapeDtypeStruct((B, Nt, BRANCH_W), F32),
        grid=(B, nb),
        in_specs=[spec(c0), spec(cf), spec(c0 + 3), pl.BlockSpec((3, BRANCH_W), lambda b, s: (0, 0))],
        out_specs=pl.BlockSpec((None, TOK_BLK, BRANCH_W), lambda b, s: (b, blk(s), 0)),
        scratch_shapes=[pltpu.VMEM((HG_HEADS, LANES, LANES), F32)],
        compiler_params=_cparams(("parallel", "arbitrary")),
        name="hgrn2_scan_bwd" if rev else "hgrn2_scan_fwd",
    )(p3, p3, p3, lbvec)


def _gla_scan(p3, w2pad, b2, rev, n_lat):
    B, Nt, _ = p3.shape
    nb = Nt // TOK_BLK
    blk = _scan_order(rev, n_lat // TOK_BLK)
    wqk = GLA_HEADS * GLA_DK
    spec = lambda w, cidx: pl.BlockSpec((None, TOK_BLK, w), lambda b, s: (b, blk(s), cidx))
    return pl.pallas_call(
        functools.partial(_gla_kernel, rev=rev),
        out_shape=jax.ShapeDtypeStruct((B, Nt, BRANCH_W), F32),
        grid=(B, nb),
        in_specs=[
            spec(wqk, COL_GLA // wqk), spec(wqk, COL_GLA // wqk + 1), spec(BRANCH_W, (COL_GLA + 2 * wqk) // BRANCH_W),
            spec(LANES, COL_GLR // LANES),
            pl.BlockSpec((LANES, wqk), lambda b, s: (0, 0)),
            pl.BlockSpec((1, wqk), lambda b, s: (0, 0)),
        ],
        out_specs=pl.BlockSpec((None, TOK_BLK, BRANCH_W), lambda b, s: (b, blk(s), 0)),
        scratch_shapes=[pltpu.VMEM((GLA_HEADS, LANES, LANES), F32)],
        compiler_params=_cparams(("parallel", "arbitrary")),
        name="gla_scan_bwd" if rev else "gla_scan_fwd",
    )(p3, p3, p3, p3, w2pad, b2)


def _group_norm(o, g):
    parts = []
    for hd in range(o.shape[1] // LANES):
        oh = o[:, LANES * hd:LANES * (hd + 1)]
        parts.append(oh * lax.rsqrt(jnp.mean(oh * oh, axis=-1, keepdims=True) + EPS))
    return jnp.concatenate(parts, axis=-1) * g


def _merge_kernel(x_ref, h_ref, za_ref, zb_ref, zc_ref, zd_ref, ya_ref, yb_ref, hf_ref, hb_ref, gf_ref, gb_ref,
                  wbr_ref, wmg_ref, bmg_ref, wout_ref, ghg_ref, ggl_ref, m_ref, o_ref):
    h = h_ref[...]
    ys = (
        ya_ref[...].astype(F32),
        yb_ref[...].astype(F32),
        _group_norm(hf_ref[...].astype(F32) + hb_ref[...].astype(F32), ghg_ref[...]),
        _group_norm(gf_ref[...].astype(F32) + gb_ref[...].astype(F32), ggl_ref[...]),
    )
    zs = (za_ref, zb_ref, zc_ref, zd_ref)
    acc = None
    for br in range(N_BRANCH):
        gate = jax.nn.sigmoid(_dot(h, wmg_ref[br]) + bmg_ref[br])
        z = zs[br][...].astype(F32)
        yz = (ys[br] * (z * jax.nn.sigmoid(z))).astype(BF16)
        part = gate * _dot(yz, wbr_ref[br])
        acc = part if acc is None else acc + part
    out = _dot(acc.astype(BF16), wout_ref[...])
    o_ref[...] = x_ref[...] + m_ref[2:3, :] * out


def _merge(xs, h, p3, ys, w_br, w_merge, b_merge, w_out, g_hg, g_gla, modsel, n_lat_blk, n_blk):
    B, _, D = xs.shape
    tok = lambda w: pl.BlockSpec((None, TOK_BLK, w), lambda b, i: (b, i, 0))
    zspec = lambda br: pl.BlockSpec((None, TOK_BLK, BRANCH_W), lambda b, i: (b, i, COL_Z // BRANCH_W + br))
    once = pl.Buffered(1)
    const = lambda shape: pl.BlockSpec(shape, lambda b, i: (0,) * len(shape), pipeline_mode=once)
    ghg = jnp.tile(g_hg, HG_HEADS).reshape(1, BRANCH_W)
    ggl = jnp.tile(g_gla, GLA_HEADS).reshape(1, BRANCH_W)
    bm = b_merge.reshape(N_BRANCH, 1, D)
    return pl.pallas_call(
        _merge_kernel,
        out_shape=jax.ShapeDtypeStruct((B, n_blk * TOK_BLK, D), F32),
        grid=(B, n_blk),
        in_specs=[tok(D), tok(D)] + [zspec(br) for br in range(N_BRANCH)] + [tok(BRANCH_W)] * 6 + [
            const(w_br.shape), const(w_merge.shape), const(bm.shape), const(w_out.shape),
            const(ghg.shape), const(ggl.shape),
            pl.BlockSpec((None, None, 3, D), lambda b, i: (b, jnp.where(i >= n_lat_blk, 1, 0), 0, 0)),
        ],
        out_specs=tok(D),
        compiler_params=_cparams(("parallel", "parallel"), vmem_mb=56),
        name="branch_merge",
    )(xs, h, p3, p3, p3, p3, *ys, w_br, w_merge, bm, w_out, ghg, ggl, modsel)


def kernel(x, c, ctx, c_ctx, ada_w, ada_b, norm_g, w_in, mla_w_uq, mla_w_ukv, mla_g_cq, mla_g_ckv, mla_g_q,
           mla_g_k, nat_rpb, nat_g_q, nat_g_k, hg_lb_logits, hg_g_o, gla_w2, gla_b2, gla_g_o, w_br, w_merge,
           b_merge, w_out):
    B, n_lat, D = x.shape
    n_ctx = ctx.shape[1]
    depth = ada_w.shape[0]
    Nt = n_lat + n_ctx
    n_lat_blk = n_lat // TOK_BLK
    assert n_ctx == TOK_BLK and n_lat % NAT_PREP_BLK == 0 and (B * Nt) % MM_ROW_BLK == 0

    cc = jnp.zeros((8, D), F32).at[:B].set(c).at[B].set(c_ctx)
    ada = _ada_vectors(cc, ada_w.astype(BF16), ada_b).reshape(depth, 8, 3, D)
    sm = jax.nn.softmax(hg_lb_logits.astype(F32), axis=0)
    lower = jnp.maximum(jnp.cumsum(sm, axis=0) - sm[0], 0.0)
    tabs = _rope_tables(n_lat, n_ctx)

    xs = jnp.concatenate([x, ctx], axis=1)
    for l in range(depth):
        need_ctx = l < depth - 1
        modsel = jnp.stack([ada[l, :B], jnp.broadcast_to(ada[l, B], (B, 3, D))], axis=1)
        h = _norm_modulate(xs, norm_g[l], modsel, n_lat_blk)
        p3 = _in_projection(h.reshape(B * Nt, D), _pack_w_in(w_in[l]), F32).reshape(B, Nt, IN_W_PAD)

        mw = _mla_weights(mla_w_uq[l], mla_w_ukv[l], mla_g_cq[l], mla_g_ckv[l], mla_g_q[l], mla_g_k[l])
        qp, qr, km, vm = _mla_prep(p3, mw, tabs)
        y_a = _mla_attention(qp, qr, km, vm, n_lat, need_ctx)
        y_b = _nat_attention(p3, nat_rpb[l], nat_g_q[l], nat_g_k[l], n_lat, need_ctx)

        scans = []
        for d in range(2):
            lb = lower[l, d]
            lbvec = jnp.stack([jnp.log(lb), jnp.log1p(-lb), 1.0 - lb], axis=0)
            scans.append(_hgrn2_scan(p3, lbvec, d == 1, n_lat))
        for d in range(2):
            w2pad = jnp.zeros((LANES, GLA_HEADS * GLA_DK), F32)
            w2pad = w2pad.at[GLA_LOWRANK * d:GLA_LOWRANK * (d + 1)].set(gla_w2[l, d]).astype(BF16)
            scans.append(_gla_scan(p3, w2pad, gla_b2[l, d].reshape(1, -1), d == 1, n_lat))

        n_blk = Nt // TOK_BLK if need_ctx else n_lat_blk
        xs = _merge(xs, h, p3, (y_a, y_b, *scans), w_br[l].astype(BF16), w_merge[l].astype(BF16), b_merge[l],
                    w_out[l].astype(BF16), hg_g_o[l], gla_g_o[l], modsel, n_lat_blk, n_blk)
    return xs[:, :n_lat]
```

```python
import functools

import jax
import jax.numpy as jnp
import numpy as np
from jax import lax
from jax.experimental import pallas as pl
from jax.experimental.pallas import tpu as pltpu

F32 = jnp.float32
BF16 = jnp.bfloat16

GRID_W = 64
N_BRANCH = 4
BRANCH_W = 512
MLA_HEADS = 8
MLA_NOPE = 64
MLA_ROPE = 32
MLA_V = 64
MLA_QK = MLA_NOPE + MLA_ROPE
Q_LORA = 256
KV_LORA = 128
ROPE_BASE = 10000.0
NAT_HEADS = 8
NAT_DH = 64
NAT_KH = 8
NAT_KW = 16
HG_HEADS = 4
HG_DK = 128
GLA_HEADS = 4
GLA_DK = 64
GLA_LOWRANK = 16
GLA_NORMALIZER = 16.0
EPS = 1e-6

W_MLA = Q_LORA + KV_LORA + MLA_ROPE
W_NAT = 3 * BRANCH_W
W_HG = 4 * BRANCH_W
W_GLA = 2 * GLA_HEADS * GLA_DK + BRANCH_W + 2 * GLA_LOWRANK

LANES = 128
HEAD_PAD = 128
TOK_BLK = 256

COL_HG = 0
COL_Z = COL_HG + W_HG
COL_NAT = COL_Z + N_BRANCH * BRANCH_W
COL_GLA = COL_NAT + W_NAT
COL_MLA = COL_GLA + 1024
COL_GLR = COL_MLA + 512
IN_W_PAD = COL_GLR + LANES
MM_COL_BLK = IN_W_PAD // 3
MM_ROW_BLK = 512

SUB = 16
CHUNK = 64
EXP_CAP = 80.0
NEG_BIG = -1e30
LOG2E = 1.4426950408889634


def _cparams(sem, vmem_mb=48):
    return pltpu.CompilerParams(dimension_semantics=sem, vmem_limit_bytes=vmem_mb * 1024 * 1024)


def _dot(a, b):
    return jnp.dot(a, b, preferred_element_type=F32)


def _dot_nt(a, b):
    return lax.dot_general(a, b, (((1,), (1,)), ((), ())), preferred_element_type=F32)


def _ada_kernel(c_ref, w_ref, b_ref, o_ref):
    c = c_ref[...]
    a = (c * jax.nn.sigmoid(c)).astype(BF16)
    o_ref[...] = _dot(a, w_ref[...]) + b_ref[...]


def _ada_vectors(cc, ada_w, ada_b):
    L, D, D3 = ada_w.shape
    tn = 1024
    return pl.pallas_call(
        _ada_kernel,
        out_shape=jax.ShapeDtypeStruct((L, cc.shape[0], D3), F32),
        grid=(L, D3 // tn),
        in_specs=[
            pl.BlockSpec((cc.shape[0], D), lambda l, j: (0, 0)),
            pl.BlockSpec((None, D, tn), lambda l, j: (l, 0, j)),
            pl.BlockSpec((None, 1, tn), lambda l, j: (l, 0, j)),
        ],
        out_specs=pl.BlockSpec((None, cc.shape[0], tn), lambda l, j: (l, 0, j)),
        compiler_params=_cparams(("parallel", "parallel")),
        name="ada_vectors",
    )(cc, ada_w, ada_b.reshape(L, 1, D3))


def _normmod_kernel(x_ref, g_ref, m_ref, h_ref):
    x = x_ref[...]
    y = x * lax.rsqrt(jnp.mean(x * x, axis=-1, keepdims=True) + EPS) * g_ref[...]
    h_ref[...] = (y * (1.0 + m_ref[1:2, :]) + m_ref[0:1, :]).astype(h_ref.dtype)


def _norm_modulate(xs, g, modsel, n_lat_blk):
    B, Nt, D = xs.shape
    return pl.pallas_call(
        _normmod_kernel,
        out_shape=jax.ShapeDtypeStruct((B, Nt, D), BF16),
        grid=(B, Nt // TOK_BLK),
        in_specs=[
            pl.BlockSpec((None, TOK_BLK, D), lambda b, i: (b, i, 0)),
            pl.BlockSpec((1, D), lambda b, i: (0, 0)),
            pl.BlockSpec((None, None, 3, D), lambda b, i: (b, jnp.where(i >= n_lat_blk, 1, 0), 0, 0)),
        ],
        out_specs=pl.BlockSpec((None, TOK_BLK, D), lambda b, i: (b, i, 0)),
        compiler_params=_cparams(("parallel", "parallel")),
        name="norm_modulate",
    )(xs, g.reshape(1, D), modsel)


def _mm_kernel(a_ref, w_ref, o_ref):
    o_ref[...] = _dot(a_ref[...], w_ref[...]).astype(o_ref.dtype)


def _in_projection(h2, w, out_dtype):
    M, K = h2.shape
    N = w.shape[1]
    return pl.pallas_call(
        _mm_kernel,
        out_shape=jax.ShapeDtypeStruct((M, N), out_dtype),
        grid=(N // MM_COL_BLK, M // MM_ROW_BLK),
        in_specs=[
            pl.BlockSpec((MM_ROW_BLK, K), lambda j, i: (i, 0)),
            pl.BlockSpec((K, MM_COL_BLK), lambda j, i: (0, j)),
        ],
        out_specs=pl.BlockSpec((MM_ROW_BLK, MM_COL_BLK), lambda j, i: (i, j)),
        compiler_params=_cparams(("parallel", "parallel")),
        name="in_projection",
    )(h2, w)


def _pack_w_in(w_in_l):
    o_nat = W_MLA
    o_hg = o_nat + W_NAT
    o_gla = o_hg + W_HG
    o_z = o_gla + W_GLA
    D = w_in_l.shape[0]
    zeros = lambda n: jnp.zeros((D, n), w_in_l.dtype)
    cols = [
        w_in_l[:, o_hg:o_hg + W_HG],
        w_in_l[:, o_z:o_z + N_BRANCH * BRANCH_W],
        w_in_l[:, o_nat:o_nat + W_NAT],
        w_in_l[:, o_gla:o_gla + 1024],
        w_in_l[:, :W_MLA], zeros(512 - W_MLA),
        w_in_l[:, o_gla + 1024:o_gla + W_GLA], zeros(LANES - 2 * GLA_LOWRANK),
    ]
    return jnp.concatenate(cols, axis=1).astype(BF16)


def _mla_prep_kernel(p_ref, wq_ref, wk_ref, wv_ref, gcq_ref, gckv_ref, gq_ref, gk_ref, vone_ref,
                     c_ref, s1_ref, s2_ref, qp_ref, qr_ref, k_ref, v_ref):
    p = p_ref[...].astype(F32)
    cq = p[:, :Q_LORA]
    r = lax.rsqrt(jnp.mean(cq * cq, axis=-1, keepdims=True) + EPS)
    cqn = (cq * r * gcq_ref[...]).astype(BF16)
    qraw = _dot(cqn, wq_ref[...])
    kv = p[:, Q_LORA:]
    ckv = kv[:, :KV_LORA]
    r2 = lax.rsqrt(jnp.mean(ckv * ckv, axis=-1, keepdims=True) + EPS)
    lane = lax.broadcasted_iota(jnp.int32, kv.shape, 1)
    lhs = (kv * jnp.where(lane < KV_LORA, r2, 1.0) * gckv_ref[...]).astype(BF16)
    kraw = _dot(lhs, wk_ref[...])
    vraw = _dot(lhs, wv_ref[...]) + vone_ref[...]
    cs = c_ref[...]
    s1 = s1_ref[...]
    s2 = s2_ref[...]

    def rope(t):
        return t * cs + pltpu.roll(t, HEAD_PAD - MLA_ROPE // 2, 1) * s1 + pltpu.roll(t, MLA_ROPE // 2, 1) * s2

    inv = 1.0 / MLA_QK
    for hd in range(MLA_HEADS):
        sl = slice(HEAD_PAD * hd, HEAD_PAD * (hd + 1))
        qh = qraw[:, sl]
        qn = qh * lax.rsqrt(jnp.sum(qh * qh, axis=-1, keepdims=True) * inv + EPS) * gq_ref[...]
        qp_ref[hd] = qn.astype(qp_ref.dtype)
        qr_ref[hd] = rope(qn).astype(qr_ref.dtype)
        kh = kraw[:, sl]
        kn = kh * lax.rsqrt(jnp.sum(kh * kh, axis=-1, keepdims=True) * inv + EPS) * gk_ref[...]
        k_ref[hd] = rope(kn).astype(k_ref.dtype)
        v_ref[hd] = vraw[:, sl].astype(v_ref.dtype)


def _mla_weights(w_uq, w_ukv, g_cq, g_ckv, g_q, g_k):
    H = MLA_HEADS
    wq = jnp.pad(w_uq.reshape(Q_LORA, H, MLA_QK), ((0, 0), (0, 0), (0, HEAD_PAD - MLA_QK)))
    wq = wq.reshape(Q_LORA, H * HEAD_PAD).astype(BF16)
    wkv = w_ukv.reshape(KV_LORA, H, MLA_NOPE + MLA_V)
    wk = jnp.pad(wkv[:, :, :MLA_NOPE], ((0, 0), (0, 0), (0, HEAD_PAD - MLA_NOPE)))
    place = jnp.zeros((MLA_ROPE, H, HEAD_PAD), F32)
    place = place.at[jnp.arange(MLA_ROPE), :, MLA_NOPE + jnp.arange(MLA_ROPE)].set(1.0)
    wk = jnp.concatenate([wk, place, jnp.zeros((256 - KV_LORA - MLA_ROPE, H, HEAD_PAD), F32)], axis=0)
    wk = wk.reshape(256, H * HEAD_PAD).astype(BF16)
    wv = jnp.pad(wkv[:, :, MLA_NOPE:], ((0, 256 - KV_LORA), (0, 0), (0, HEAD_PAD - MLA_V)))
    wv = wv.reshape(256, H * HEAD_PAD).astype(BF16)
    vone = jnp.zeros((H, HEAD_PAD), F32).at[:, MLA_V].set(1.0).reshape(1, H * HEAD_PAD)
    gckv = jnp.concatenate([g_ckv, jnp.ones((256 - KV_LORA,), F32)]).reshape(1, 256)
    gq = jnp.pad(g_q * (MLA_QK ** -0.5 * LOG2E), (0, HEAD_PAD - MLA_QK)).reshape(1, HEAD_PAD)
    gk = jnp.pad(g_k, (0, HEAD_PAD - MLA_QK)).reshape(1, HEAD_PAD)
    return wq, wk, wv, g_cq.reshape(1, Q_LORA), gckv, gq, gk, vone


def _rope_tables(n_lat, n_ctx):
    quarter = MLA_ROPE // 4
    inv_freq = ROPE_BASE ** (-jnp.arange(quarter, dtype=F32) / quarter)
    t = jnp.arange(n_lat, dtype=jnp.int32)
    row = (t // GRID_W).astype(F32)
    col = (t % GRID_W).astype(F32)
    ang = jnp.concatenate([row[:, None] * inv_freq, col[:, None] * inv_freq], axis=-1)
    cos, sin = jnp.cos(ang), jnp.sin(ang)
    half = MLA_ROPE // 2
    ones = jnp.ones((n_lat, MLA_NOPE), F32)
    tail = jnp.ones((n_lat, HEAD_PAD - MLA_QK), F32)
    zl = jnp.zeros((n_lat, MLA_NOPE), F32)
    zh = jnp.zeros((n_lat, half), F32)
    zt = jnp.zeros((n_lat, HEAD_PAD - MLA_QK), F32)
    c = jnp.concatenate([ones, cos, cos, tail], axis=1)
    s1 = jnp.concatenate([zl, -sin, zh, zt], axis=1)
    s2 = jnp.concatenate([zl, zh, sin, zt], axis=1)
    c = jnp.concatenate([c, jnp.ones((n_ctx, HEAD_PAD), F32)], axis=0)
    s1 = jnp.concatenate([s1, jnp.zeros((n_ctx, HEAD_PAD), F32)], axis=0)
    s2 = jnp.concatenate([s2, jnp.zeros((n_ctx, HEAD_PAD), F32)], axis=0)
    return c, s1, s2


def _mla_prep(p3, mw, tabs):
    B, Nt, _ = p3.shape
    H = MLA_HEADS
    wq, wk, wv, gcq, gckv, gq, gk, vone = mw
    const = lambda shape: pl.BlockSpec(shape, lambda b, i: (0,) * len(shape))
    tab = pl.BlockSpec((TOK_BLK, HEAD_PAD), lambda b, i: (i, 0))
    out = pl.BlockSpec((None, H, TOK_BLK, HEAD_PAD), lambda b, i: (b, 0, i, 0))
    shp = jax.ShapeDtypeStruct((B, H, Nt, HEAD_PAD), BF16)
    return pl.pallas_call(
        _mla_prep_kernel,
        out_shape=(shp, shp, shp, shp),
        grid=(B, Nt // TOK_BLK),
        in_specs=[
            pl.BlockSpec((None, TOK_BLK, 512), lambda b, i: (b, i, COL_MLA // 512)),
            const(wq.shape), const(wk.shape), const(wv.shape), const(gcq.shape), const(gckv.shape),
            const(gq.shape), const(gk.shape), const(vone.shape), tab, tab, tab,
        ],
        out_specs=(out, out, out, out),
        compiler_params=_cparams(("parallel", "parallel")),
        name="mla_prep",
    )(p3, wq, wk, wv, gcq, gckv, gq, gk, vone, *tabs)


MLA_TQ = 1024
MLA_TK = 512


def _softmax_step(q, kc, vc, m, acc):
    s = _dot_nt(q, kc)
    m_new = jnp.maximum(m, jnp.max(s, axis=-1, keepdims=True))
    p = jnp.exp2(s - m_new)
    acc = jnp.exp2(m - m_new) * acc + _dot(p.astype(BF16), vc)
    return m_new, acc


def _mla_attn_kernel(qr_ref, qp_ref, k_ref, v_ref, o_ref, *, n_lat):
    tq = qr_ref.shape[1]

    def body(c, carry):
        off = pl.multiple_of(c * MLA_TK, MLA_TK)
        return tuple(
            _softmax_step(qr_ref[hh], k_ref[hh, pl.ds(off, MLA_TK), :], v_ref[hh, pl.ds(off, MLA_TK), :], *carry[hh])
            for hh in range(2))

    m0 = jnp.full((tq, 1), NEG_BIG, F32)
    a0 = jnp.zeros((tq, HEAD_PAD), F32)
    carry = lax.fori_loop(0, n_lat // MLA_TK, body, ((m0, a0), (m0, a0)), unroll=4)
    outs = []
    for hh in range(2):
        _, acc = _softmax_step(qp_ref[hh], k_ref[hh, n_lat:, :], v_ref[hh, n_lat:, :], *carry[hh])
        outs.append(acc[:, :MLA_V] / acc[:, MLA_V:MLA_V + 1])
    o_ref[...] = jnp.concatenate(outs, axis=-1).astype(o_ref.dtype)


def _mla_ctx_kernel(qp_ref, k_ref, v_ref, o_ref):
    outs = []
    for hh in range(2):
        s = _dot_nt(qp_ref[hh], k_ref[hh])
        p = jnp.exp2(s - jnp.max(s, axis=-1, keepdims=True))
        acc = _dot(p.astype(BF16), v_ref[hh])
        outs.append(acc[:, :MLA_V] / acc[:, MLA_V:MLA_V + 1])
    o_ref[...] = jnp.concatenate(outs, axis=-1).astype(o_ref.dtype)


def _mla_attention(qp, qr, k, v, n_lat, need_ctx):
    B, H, Nt, _ = k.shape
    n_ctx = Nt - n_lat
    qspec = pl.BlockSpec((None, 2, MLA_TQ, HEAD_PAD), lambda b, g, i: (b, g, i, 0))
    kvspec = pl.BlockSpec((None, 2, Nt, HEAD_PAD), lambda b, g, i: (b, g, 0, 0))
    o_lat = pl.pallas_call(
        functools.partial(_mla_attn_kernel, n_lat=n_lat),
        out_shape=jax.ShapeDtypeStruct((B, n_lat, BRANCH_W), BF16),
        grid=(B, H // 2, n_lat // MLA_TQ),
        in_specs=[qspec, qspec, kvspec, kvspec],
        out_specs=pl.BlockSpec((None, MLA_TQ, 2 * MLA_V), lambda b, g, i: (b, i, g)),
        compiler_params=_cparams(("parallel", "parallel", "arbitrary")),
        name="mla_attention",
    )(qr, qp, k, v)
    if not need_ctx:
        return o_lat
    cblk = n_lat // n_ctx
    cspec = pl.BlockSpec((None, 2, n_ctx, HEAD_PAD), lambda b, g: (b, g, cblk, 0))
    o_ctx = pl.pallas_call(
        _mla_ctx_kernel,
        out_shape=jax.ShapeDtypeStruct((B, n_ctx, BRANCH_W), BF16),
        grid=(B, H // 2),
        in_specs=[cspec, cspec, cspec],
        out_specs=pl.BlockSpec((None, n_ctx, 2 * MLA_V), lambda b, g: (b, 0, g)),
        compiler_params=_cparams(("parallel", "parallel")),
        name="mla_attention_ctx",
    )(qp, k, v)
    return jnp.concatenate([o_lat, o_ctx], axis=1)


NAT_ROWS_BLK = 8
NAT_TQ = NAT_ROWS_BLK * GRID_W
NAT_WIN_ROWS = 16
NAT_WIN = NAT_WIN_ROWS * GRID_W
NAT_PREP_BLK = 1024


def _pair_norm(x, g, half0):
    x2 = x * x
    s0 = jnp.sum(jnp.where(half0, x2, 0.0), axis=-1, keepdims=True)
    s1 = jnp.sum(jnp.where(half0, 0.0, x2), axis=-1, keepdims=True)
    ms = jnp.where(half0, s0, s1) * (1.0 / NAT_DH)
    return x * lax.rsqrt(ms + EPS) * g


def _nat_kernel(q_ref, kl_ref, vl_ref, kc_ref, vc_ref, bias_ref, gq_ref, gk_ref, o_ref, kn_ref, vb_ref,
                *, n_lat, rows_total):
    rb = pl.program_id(2)
    half0 = lax.broadcasted_iota(jnp.int32, (1, LANES), 1) < NAT_DH

    @pl.when(rb == 0)
    def _():
        def body(c, carry):
            off = pl.multiple_of(c * NAT_PREP_BLK, NAT_PREP_BLK)
            kn_ref[pl.ds(off, NAT_PREP_BLK), :] = _pair_norm(
                kl_ref[pl.ds(off, NAT_PREP_BLK), :].astype(F32), gk_ref[...], half0).astype(BF16)
            vb_ref[pl.ds(off, NAT_PREP_BLK), :] = vl_ref[pl.ds(off, NAT_PREP_BLK), :].astype(BF16)
            return carry

        lax.fori_loop(0, n_lat // NAT_PREP_BLK, body, 0)
        kn_ref[n_lat:, :] = _pair_norm(kc_ref[...].astype(F32), gk_ref[...], half0).astype(BF16)
        vb_ref[n_lat:, :] = vc_ref[...].astype(BF16)

    q = _pair_norm(q_ref[...].astype(F32), gq_ref[...], half0)
    ws = jnp.clip(rb * NAT_ROWS_BLK - NAT_KH // 2, 0, rows_total - NAT_WIN_ROWS)
    off = pl.multiple_of(ws * GRID_W, GRID_W)
    kw = kn_ref[pl.ds(off, NAT_WIN), :]
    vw = vb_ref[pl.ds(off, NAT_WIN), :]
    kc = kn_ref[n_lat:, :]
    vc = vb_ref[n_lat:, :]
    outs = []
    for hh in range(2):
        sel = half0 if hh == 0 else jnp.logical_not(half0)
        qm = jnp.where(sel, q, 0.0).astype(BF16)
        sw = _dot_nt(qm, kw) + bias_ref[hh].astype(F32)
        sc = _dot_nt(qm, kc)
        m = jnp.maximum(jnp.max(sw, axis=-1, keepdims=True), jnp.max(sc, axis=-1, keepdims=True))
        pw = jnp.exp2(sw - m)
        pc = jnp.exp2(sc - m)
        l = jnp.sum(pw, axis=-1, keepdims=True) + jnp.sum(pc, axis=-1, keepdims=True)
        o = _dot(pw.astype(BF16), vw) + _dot(pc.astype(BF16), vc)
        outs.append(o / l)
    o_ref[...] = jnp.where(half0, outs[0], outs[1]).astype(o_ref.dtype)


def _nat_ctx_kernel(q_ref, kc_ref, vc_ref, gq_ref, gk_ref, o_ref):
    half0 = lax.broadcasted_iota(jnp.int32, (1, LANES), 1) < NAT_DH
    q = _pair_norm(q_ref[...].astype(F32), gq_ref[...], half0)
    kc = _pair_norm(kc_ref[...].astype(F32), gk_ref[...], half0).astype(BF16)
    vc = vc_ref[...].astype(BF16)
    outs = []
    for hh in range(2):
        sel = half0 if hh == 0 else jnp.logical_not(half0)
        s = _dot_nt(jnp.where(sel, q, 0.0).astype(BF16), kc)
        p = jnp.exp2(s - jnp.max(s, axis=-1, keepdims=True))
        outs.append(_dot(p.astype(BF16), vc) / jnp.sum(p, axis=-1, keepdims=True))
    o_ref[...] = jnp.where(half0, outs[0], outs[1]).astype(o_ref.dtype)


def _nat_bias_tables(rpb, rows_total):
    rbs = np.array([0, min(NAT_ROWS_BLK, rows_total - NAT_ROWS_BLK), rows_total - NAT_ROWS_BLK])
    ws = np.clip(rbs - NAT_KH // 2, 0, rows_total - NAT_WIN_ROWS)
    qrow = rbs[:, None] + np.arange(NAT_ROWS_BLK)[None, :]
    r0 = np.clip(qrow - NAT_KH // 2, 0, rows_total - NAT_KH)
    krow = ws[:, None] + np.arange(NAT_WIN_ROWS)[None, :]
    row_ok = (krow[:, None, :] >= r0[:, :, None]) & (krow[:, None, :] < r0[:, :, None] + NAT_KH)
    dr = np.clip(krow[:, None, :] - qrow[:, :, None] + NAT_KH - 1, 0, 2 * NAT_KH - 2)
    col = np.arange(GRID_W)
    c0 = np.clip(col - NAT_KW // 2, 0, GRID_W - NAT_KW)
    col_ok = (col[None, :] >= c0[:, None]) & (col[None, :] < c0[:, None] + NAT_KW)
    dc = np.clip(col[None, :] - col[:, None], -(NAT_KW - 1), NAT_KW - 1) + NAT_KW - 1
    oh_r = np.eye(2 * NAT_KH - 1, dtype=np.float32)[dr]
    oh_c = np.eye(2 * NAT_KW - 1, dtype=np.float32)[dc]
    hi = lax.Precision.HIGHEST
    by_col = jnp.einsum('hrc,qkc->hrqk', rpb.astype(F32), oh_c, precision=hi)
    tab = jnp.einsum('vijr,hrqk->vhiqjk', oh_r, by_col, precision=hi)
    ok = row_ok[:, None, :, None, :, None] & col_ok[None, None, None, :, None, :]
    tab = jnp.where(jnp.asarray(ok), tab * LOG2E, NEG_BIG).astype(BF16)
    return tab.reshape(3, NAT_HEADS // 2, 2, NAT_TQ, NAT_WIN)


def _nat_attention(p3, rpb, g_q, g_k, n_lat, need_ctx):
    B, Nt, _ = p3.shape
    n_ctx = Nt - n_lat
    rows_total = n_lat // GRID_W
    n_rb = rows_total // NAT_ROWS_BLK
    bias = _nat_bias_tables(rpb, rows_total)
    gq = jnp.tile(g_q * (NAT_DH ** -0.5 * LOG2E), 2).reshape(1, LANES)
    gk = jnp.tile(g_k, 2).reshape(1, LANES)
    cq, ck, cv = ((COL_NAT + j * BRANCH_W) // LANES for j in range(3))
    cblk = n_lat // n_ctx

    def variant(rb):
        return jnp.where(rb == 0, 0, jnp.where(rb == n_rb - 1, 2, 1))

    vec = pl.BlockSpec((1, LANES), lambda b, g, rb: (0, 0))
    o_lat = pl.pallas_call(
        functools.partial(_nat_kernel, n_lat=n_lat, rows_total=rows_total),
        out_shape=jax.ShapeDtypeStruct((B, n_lat, BRANCH_W), BF16),
        grid=(B, NAT_HEADS // 2, n_rb),
        in_specs=[
            pl.BlockSpec((None, NAT_TQ, LANES), lambda b, g, rb: (b, rb, cq + g)),
            pl.BlockSpec((None, n_lat, LANES), lambda b, g, rb: (b, 0, ck + g)),
            pl.BlockSpec((None, n_lat, LANES), lambda b, g, rb: (b, 0, cv + g)),
            pl.BlockSpec((None, n_ctx, LANES), lambda b, g, rb: (b, cblk, ck + g)),
            pl.BlockSpec((None, n_ctx, LANES), lambda b, g, rb: (b, cblk, cv + g)),
            pl.BlockSpec((None, None, 2, NAT_TQ, NAT_WIN), lambda b, g, rb: (variant(rb), g, 0, 0, 0)),
            vec, vec,
        ],
        out_specs=pl.BlockSpec((None, NAT_TQ, LANES), lambda b, g, rb: (b, rb, g)),
        scratch_shapes=[pltpu.VMEM((Nt, LANES), BF16), pltpu.VMEM((Nt, LANES), BF16)],
        compiler_params=_cparams(("parallel", "parallel", "arbitrary")),
        name="nat_attention",
    )(p3, p3, p3, p3, p3, bias, gq, gk)
    if not need_ctx:
        return o_lat
    vec2 = pl.BlockSpec((1, LANES), lambda b, g: (0, 0))
    o_ctx = pl.pallas_call(
        _nat_ctx_kernel,
        out_shape=jax.ShapeDtypeStruct((B, n_ctx, BRANCH_W), BF16),
        grid=(B, NAT_HEADS // 2),
        in_specs=[
            pl.BlockSpec((None, n_ctx, LANES), lambda b, g: (b, cblk, cq + g)),
            pl.BlockSpec((None, n_ctx, LANES), lambda b, g: (b, cblk, ck + g)),
            pl.BlockSpec((None, n_ctx, LANES), lambda b, g: (b, cblk, cv + g)),
            vec2, vec2,
        ],
        out_specs=pl.BlockSpec((None, n_ctx, LANES), lambda b, g: (b, 0, g)),
        compiler_params=_cparams(("parallel", "parallel")),
        name="nat_attention_ctx",
    )(p3, p3, p3, gq, gk)
    return jnp.concatenate([o_lat, o_ctx], axis=1)


def _log_sigmoid(x):
    return jnp.minimum(x, 0.0) - jnp.log1p(jnp.exp(-jnp.abs(x)))


def _split3(a):
    hi = a.astype(BF16)
    r1 = a - hi.astype(F32)
    mid = r1.astype(BF16)
    lo = (r1 - mid.astype(F32)).astype(BF16)
    return hi, mid, lo


def _scan_chunk(qs, k, v, a, s_ref, rev, heads):
    C = qs.shape[0]
    row = lax.broadcasted_iota(jnp.int32, (C, C), 0)
    col = lax.broadcasted_iota(jnp.int32, (C, C), 1)
    causal = (row <= col) if rev else (row >= col)
    tri = jnp.where(causal, 1.0, 0.0).astype(BF16)
    hi, mid, lo = _split3(a)
    bc = _dot(tri, hi) + _dot(tri, mid) + _dot(tri, lo)
    bex = bc - a
    nsub = C // SUB
    refs = []
    for i in range(nsub):
        r = SUB * i + (SUB - 1 if rev else 0)
        refs.append(bex[r:r + 1, :])
    rfull = jnp.concatenate([jnp.broadcast_to(r, (SUB, r.shape[1])) for r in refs], axis=0)
    qt = qs * jnp.exp(bc - rfull)
    q_in = qs * jnp.exp(bc)
    e = 0 if rev else C - 1
    bend = bc[e:e + 1, :]
    k_out = k * jnp.exp(bend - bc)
    dec = jnp.exp(bend)
    kts = [k * jnp.exp(jnp.minimum(r - bc, EXP_CAP)) for r in refs]
    outs = []
    for hd, (ks, kmask, vs) in enumerate(heads):
        ksl = slice(ks, ks + LANES)
        vsl = slice(vs, vs + LANES)

        def msk(t, kmask=kmask):
            return t if kmask is None else jnp.where(kmask, t, 0.0)

        qt_h = msk(qt[:, ksl]).astype(BF16)
        blocks = [_dot_nt(qt_h[SUB * i:SUB * (i + 1), :], kts[i][:, ksl].astype(BF16)) for i in range(nsub)]
        sc = jnp.where(causal, jnp.concatenate(blocks, axis=0), 0.0).astype(BF16)
        vh = v[:, vsl].astype(BF16)
        st = s_ref[hd]
        o = _dot(sc, vh) + _dot_nt(msk(q_in[:, ksl]).astype(BF16), st.astype(BF16))
        s_ref[hd] = st * dec[:, ksl] + _dot(v[:, vsl].T.astype(BF16), msk(k_out[:, ksl]).astype(BF16))
        outs.append(o)
    return jnp.concatenate(outs, axis=-1)


def _run_block(qs, k, v, a, s_ref, o_ref, rev, heads):
    n_ch = qs.shape[0] // CHUNK
    order = range(n_ch - 1, -1, -1) if rev else range(n_ch)
    for c in order:
        rs = slice(CHUNK * c, CHUNK * (c + 1))
        o_ref[rs, :] = _scan_chunk(qs[rs], k[rs], v[rs], a[rs], s_ref, rev, heads).astype(o_ref.dtype)


def _hgrn2_kernel(q_ref, f_ref, v_ref, lb_ref, o_ref, s_ref, *, rev):
    @pl.when(pl.program_id(1) == 0)
    def _():
        s_ref[...] = jnp.zeros_like(s_ref)

    q = q_ref[...].astype(F32)
    f = f_ref[...].astype(F32)
    loglb = lb_ref[0:1, :]
    log1mlb = lb_ref[1:2, :]
    oneml = lb_ref[2:3, :]
    u = log1mlb + _log_sigmoid(f)
    a = jnp.maximum(u, loglb) + jnp.log1p(jnp.exp(-jnp.abs(u - loglb)))
    k = oneml * jax.nn.sigmoid(-f)
    qs = q * jax.nn.sigmoid(q) * (HG_DK ** -0.5)
    heads = [(LANES * h, None, LANES * h) for h in range(HG_HEADS)]
    _run_block(qs, k, v_ref[...].astype(F32), a, s_ref, o_ref, rev, heads)


def _gla_kernel(q_ref, k_ref, v_ref, r_ref, w2_ref, b2_ref, o_ref, s_ref, *, rev):
    @pl.when(pl.program_id(1) == 0)
    def _():
        s_ref[...] = jnp.zeros_like(s_ref)

    g = _dot(r_ref[...].astype(BF16), w2_ref[...]) + b2_ref[...]
    a = _log_sigmoid(g) * (1.0 / GLA_NORMALIZER)
    qs = q_ref[...].astype(F32) * (GLA_DK ** -0.5)
    half0 = lax.broadcasted_iota(jnp.int32, (1, LANES), 1) < GLA_DK
    heads = [(LANES * (h // 2), half0 if h % 2 == 0 else jnp.logical_not(half0), LANES * h)
             for h in range(GLA_HEADS)]
    _run_block(qs, k_ref[...].astype(F32), v_ref[...].astype(F32), a, s_ref, o_ref, rev, heads)


def _scan_order(rev, n_lat_blk):
    if rev:
        return lambda s: jnp.where(s == 0, n_lat_blk, n_lat_blk - s)
    return lambda s: jnp.where(s == 0, n_lat_blk, s - 1)


def _hgrn2_scan(p3, lbvec, rev, n_lat):
    B, Nt, _ = p3.shape
    nb = Nt // TOK_BLK
    blk = _scan_order(rev, n_lat // TOK_BLK)
    c0 = COL_HG // BRANCH_W
    cf = c0 + (2 if rev else 1)
    spec = lambda cidx: pl.BlockSpec((None, TOK_BLK, BRANCH_W), lambda b, s: (b, blk(s), cidx))
    return pl.pallas_call(
        functools.partial(_hgrn2_kernel, rev=rev),
        out_shape=jax.ShapeDtypeStruct((B, Nt, BRANCH_W), F32),
        grid=(B, nb),
        in_specs=[spec(c0), spec(cf), spec(c0 + 3), pl.BlockSpec((3, BRANCH_W), lambda b, s: (0, 0))],
        out_specs=pl.BlockSpec((None, TOK_BLK, BRANCH_W), lambda b, s: (b, blk(s), 0)),
        scratch_shapes=[pltpu.VMEM((HG_HEADS, LANES, LANES), F32)],
        compiler_params=_cparams(("parallel", "arbitrary")),
        name="hgrn2_scan_bwd" if rev else "hgrn2_scan_fwd",
    )(p3, p3, p3, lbvec)


def _gla_scan(p3, w2pad, b2, rev, n_lat):
    B, Nt, _ = p3.shape
    nb = Nt // TOK_BLK
    blk = _scan_order(rev, n_lat // TOK_BLK)
    wqk = GLA_HEADS * GLA_DK
    spec = lambda w, cidx: pl.BlockSpec((None, TOK_BLK, w), lambda b, s: (b, blk(s), cidx))
    return pl.pallas_call(
        functools.partial(_gla_kernel, rev=rev),
        out_shape=jax.ShapeDtypeStruct((B, Nt, BRANCH_W), F32),
        grid=(B, nb),
        in_specs=[
            spec(wqk, COL_GLA // wqk), spec(wqk, COL_GLA // wqk + 1), spec(BRANCH_W, (COL_GLA + 2 * wqk) // BRANCH_W),
            spec(LANES, COL_GLR // LANES),
            pl.BlockSpec((LANES, wqk), lambda b, s: (0, 0)),
            pl.BlockSpec((1, wqk), lambda b, s: (0, 0)),
        ],
        out_specs=pl.BlockSpec((None, TOK_BLK, BRANCH_W), lambda b, s: (b, blk(s), 0)),
        scratch_shapes=[pltpu.VMEM((GLA_HEADS, LANES, LANES), F32)],
        compiler_params=_cparams(("parallel", "arbitrary")),
        name="gla_scan_bwd" if rev else "gla_scan_fwd",
    )(p3, p3, p3, p3, w2pad, b2)


def _group_norm(o, g):
    parts = []
    for hd in range(o.shape[1] // LANES):
        oh = o[:, LANES * hd:LANES * (hd + 1)]
        parts.append(oh * lax.rsqrt(jnp.mean(oh * oh, axis=-1, keepdims=True) + EPS))
    return jnp.concatenate(parts, axis=-1) * g


def _merge_kernel(x_ref, h_ref, za_ref, zb_ref, zc_ref, zd_ref, ya_ref, yb_ref, hf_ref, hb_ref, gf_ref, gb_ref,
                  wbr_ref, wmg_ref, bmg_ref, wout_ref, ghg_ref, ggl_ref, m_ref, o_ref):
    h = h_ref[...]
    ys = (
        ya_ref[...].astype(F32),
        yb_ref[...].astype(F32),
        _group_norm(hf_ref[...].astype(F32) + hb_ref[...].astype(F32), ghg_ref[...]),
        _group_norm(gf_ref[...].astype(F32) + gb_ref[...].astype(F32), ggl_ref[...]),
    )
    zs = (za_ref, zb_ref, zc_ref, zd_ref)
    acc = None
    for br in range(N_BRANCH):
        gate = jax.nn.sigmoid(_dot(h, wmg_ref[br]) + bmg_ref[br])
        z = zs[br][...].astype(F32)
        yz = (ys[br] * (z * jax.nn.sigmoid(z))).astype(BF16)
        part = gate * _dot(yz, wbr_ref[br])
        acc = part if acc is None else acc + part
    out = _dot(acc.astype(BF16), wout_ref[...])
    o_ref[...] = x_ref[...] + m_ref[2:3, :] * out


def _merge(xs, h, p3, ys, w_br, w_merge, b_merge, w_out, g_hg, g_gla, modsel, n_lat_blk, n_blk):
    B, _, D = xs.shape
    tok = lambda w: pl.BlockSpec((None, TOK_BLK, w), lambda b, i: (b, i, 0))
    zspec = lambda br: pl.BlockSpec((None, TOK_BLK, BRANCH_W), lambda b, i: (b, i, COL_Z // BRANCH_W + br))
    once = pl.Buffered(1)
    const = lambda shape: pl.BlockSpec(shape, lambda b, i: (0,) * len(shape), pipeline_mode=once)
    ghg = jnp.tile(g_hg, HG_HEADS).reshape(1, BRANCH_W)
    ggl = jnp.tile(g_gla, GLA_HEADS).reshape(1, BRANCH_W)
    bm = b_merge.reshape(N_BRANCH, 1, D)
    return pl.pallas_call(
        _merge_kernel,
        out_shape=jax.ShapeDtypeStruct((B, n_blk * TOK_BLK, D), F32),
        grid=(B, n_blk),
        in_specs=[tok(D), tok(D)] + [zspec(br) for br in range(N_BRANCH)] + [tok(BRANCH_W)] * 6 + [
            const(w_br.shape), const(w_merge.shape), const(bm.shape), const(w_out.shape),
            const(ghg.shape), const(ggl.shape),
            pl.BlockSpec((None, None, 3, D), lambda b, i: (b, jnp.where(i >= n_lat_blk, 1, 0), 0, 0)),
        ],
        out_specs=tok(D),
        compiler_params=_cparams(("parallel", "parallel"), vmem_mb=56),
        name="branch_merge",
    )(xs, h, p3, p3, p3, p3, *ys, w_br, w_merge, bm, w_out, ghg, ggl, modsel)


def kernel(x, c, ctx, c_ctx, ada_w, ada_b, norm_g, w_in, mla_w_uq, mla_w_ukv, mla_g_cq, mla_g_ckv, mla_g_q,
           mla_g_k, nat_rpb, nat_g_q, nat_g_k, hg_lb_logits, hg_g_o, gla_w2, gla_b2, gla_g_o, w_br, w_merge,
           b_merge, w_out):
    B, n_lat, D = x.shape
    n_ctx = ctx.shape[1]
    depth = ada_w.shape[0]
    Nt = n_lat + n_ctx
    n_lat_blk = n_lat // TOK_BLK
    assert n_ctx == TOK_BLK and n_lat % NAT_PREP_BLK == 0 and (B * Nt) % MM_ROW_BLK == 0

    cc = jnp.zeros((8, D), F32).at[:B].set(c).at[B].set(c_ctx)
    ada = _ada_vectors(cc, ada_w.astype(BF16), ada_b).reshape(depth, 8, 3, D)
    sm = jax.nn.softmax(hg_lb_logits.astype(F32), axis=0)
    lower = jnp.maximum(jnp.cumsum(sm, axis=0) - sm[0], 0.0)
    tabs = _rope_tables(n_lat, n_ctx)

    xs = jnp.concatenate([x, ctx], axis=1)
    for l in range(depth):
        need_ctx = l < depth - 1
        modsel = jnp.stack([ada[l, :B], jnp.broadcast_to(ada[l, B], (B, 3, D))], axis=1)
        h = _norm_modulate(xs, norm_g[l], modsel, n_lat_blk)
        p3 = _in_projection(h.reshape(B * Nt, D), _pack_w_in(w_in[l]), BF16).reshape(B, Nt, IN_W_PAD)

        mw = _mla_weights(mla_w_uq[l], mla_w_ukv[l], mla_g_cq[l], mla_g_ckv[l], mla_g_q[l], mla_g_k[l])
        qp, qr, km, vm = _mla_prep(p3, mw, tabs)
        y_a = _mla_attention(qp, qr, km, vm, n_lat, need_ctx)
        y_b = _nat_attention(p3, nat_rpb[l], nat_g_q[l], nat_g_k[l], n_lat, need_ctx)

        scans = []
        for d in range(2):
            lb = lower[l, d]
            lbvec = jnp.stack([jnp.log(lb), jnp.log1p(-lb), 1.0 - lb], axis=0)
            scans.append(_hgrn2_scan(p3, lbvec, d == 1, n_lat))
        for d in range(2):
            w2pad = jnp.zeros((LANES, GLA_HEADS * GLA_DK), F32)
            w2pad = w2pad.at[GLA_LOWRANK * d:GLA_LOWRANK * (d + 1)].set(gla_w2[l, d]).astype(BF16)
            scans.append(_gla_scan(p3, w2pad, gla_b2[l, d].reshape(1, -1), d == 1, n_lat))

        n_blk = Nt // TOK_BLK if need_ctx else n_lat_blk
        xs = _merge(xs, h, p3, (y_a, y_b, *scans), w_br[l].astype(BF16), w_merge[l].astype(BF16), b_merge[l],
                    w_out[l].astype(BF16), hg_g_o[l], gla_g_o[l], modsel, n_lat_blk, n_blk)
    return xs[:, :n_lat]
```

```python
import functools

import jax
import jax.numpy as jnp
import numpy as np
from jax import lax
from jax.experimental import pallas as pl
from jax.experimental.pallas import tpu as pltpu

F32 = jnp.float32
BF16 = jnp.bfloat16

GRID_W = 64
N_BRANCH = 4
BRANCH_W = 512
MLA_HEADS = 8
MLA_NOPE = 64
MLA_ROPE = 32
MLA_V = 64
MLA_QK = MLA_NOPE + MLA_ROPE
Q_LORA = 256
KV_LORA = 128
ROPE_BASE = 10000.0
NAT_HEADS = 8
NAT_DH = 64
NAT_KH = 8
NAT_KW = 16
HG_HEADS = 4
HG_DK = 128
GLA_HEADS = 4
GLA_DK = 64
GLA_LOWRANK = 16
GLA_NORMALIZER = 16.0
EPS = 1e-6

W_MLA = Q_LORA + KV_LORA + MLA_ROPE
W_NAT = 3 * BRANCH_W
W_HG = 4 * BRANCH_W
W_GLA = 2 * GLA_HEADS * GLA_DK + BRANCH_W + 2 * GLA_LOWRANK

LANES = 128
HEAD_PAD = 128
TOK_BLK = 256

COL_HG = 0
COL_Z = COL_HG + W_HG
COL_NAT = COL_Z + N_BRANCH * BRANCH_W
COL_GLA = COL_NAT + W_NAT
COL_MLA = COL_GLA + 1024
COL_GLR = COL_MLA + 512
IN_W_PAD = COL_GLR + LANES
MM_COL_BLK = IN_W_PAD // 3
MM_ROW_BLK = 512

SUB = 16
CHUNK = 64
EXP_CAP = 80.0
NEG_BIG = -1e30
LOG2E = 1.4426950408889634


def _cparams(sem, vmem_mb=48):
    return pltpu.CompilerParams(dimension_semantics=sem, vmem_limit_bytes=vmem_mb * 1024 * 1024)


def _dot(a, b):
    return jnp.dot(a, b, preferred_element_type=F32)


def _dot_nt(a, b):
    return lax.dot_general(a, b, (((1,), (1,)), ((), ())), preferred_element_type=F32)


def _ada_kernel(c_ref, w_ref, b_ref, o_ref):
    c = c_ref[...]
    a = (c * jax.nn.sigmoid(c)).astype(BF16)
    o_ref[...] = _dot(a, w_ref[...]) + b_ref[...]


def _ada_vectors(cc, ada_w, ada_b):
    L, D, D3 = ada_w.shape
    tn = 1024
    return pl.pallas_call(
        _ada_kernel,
        out_shape=jax.ShapeDtypeStruct((L, cc.shape[0], D3), F32),
        grid=(L, D3 // tn),
        in_specs=[
            pl.BlockSpec((cc.shape[0], D), lambda l, j: (0, 0)),
            pl.BlockSpec((None, D, tn), lambda l, j: (l, 0, j)),
            pl.BlockSpec((None, 1, tn), lambda l, j: (l, 0, j)),
        ],
        out_specs=pl.BlockSpec((None, cc.shape[0], tn), lambda l, j: (l, 0, j)),
        compiler_params=_cparams(("parallel", "parallel")),
        name="ada_vectors",
    )(cc, ada_w, ada_b.reshape(L, 1, D3))


def _normmod_kernel(x_ref, g_ref, m_ref, h_ref):
    x = x_ref[...]
    y = x * lax.rsqrt(jnp.mean(x * x, axis=-1, keepdims=True) + EPS) * g_ref[...]
    h_ref[...] = (y * (1.0 + m_ref[1:2, :]) + m_ref[0:1, :]).astype(h_ref.dtype)


def _norm_modulate(xs, g, modsel, n_lat_blk):
    B, Nt, D = xs.shape
    return pl.pallas_call(
        _normmod_kernel,
        out_shape=jax.ShapeDtypeStruct((B, Nt, D), BF16),
        grid=(B, Nt // TOK_BLK),
        in_specs=[
            pl.BlockSpec((None, TOK_BLK, D), lambda b, i: (b, i, 0)),
            pl.BlockSpec((1, D), lambda b, i: (0, 0)),
            pl.BlockSpec((None, None, 3, D), lambda b, i: (b, jnp.where(i >= n_lat_blk, 1, 0), 0, 0)),
        ],
        out_specs=pl.BlockSpec((None, TOK_BLK, D), lambda b, i: (b, i, 0)),
        compiler_params=_cparams(("parallel", "parallel")),
        name="norm_modulate",
    )(xs, g.reshape(1, D), modsel)


def _mm_kernel(a_ref, w_ref, o_ref):
    o_ref[...] = _dot(a_ref[...], w_ref[...]).astype(o_ref.dtype)


def _in_projection(h2, w, out_dtype):
    M, K = h2.shape
    N = w.shape[1]
    return pl.pallas_call(
        _mm_kernel,
        out_shape=jax.ShapeDtypeStruct((M, N), out_dtype),
        grid=(N // MM_COL_BLK, M // MM_ROW_BLK),
        in_specs=[
            pl.BlockSpec((MM_ROW_BLK, K), lambda j, i: (i, 0)),
            pl.BlockSpec((K, MM_COL_BLK), lambda j, i: (0, j)),
        ],
        out_specs=pl.BlockSpec((MM_ROW_BLK, MM_COL_BLK), lambda j, i: (i, j)),
        compiler_params=_cparams(("parallel", "parallel")),
        name="in_projection",
    )(h2, w)


def _pack_w_in(w_in_l):
    o_nat = W_MLA
    o_hg = o_nat + W_NAT
    o_gla = o_hg + W_HG
    o_z = o_gla + W_GLA
    D = w_in_l.shape[0]
    zeros = lambda n: jnp.zeros((D, n), w_in_l.dtype)
    cols = [
        w_in_l[:, o_hg:o_hg + W_HG],
        w_in_l[:, o_z:o_z + N_BRANCH * BRANCH_W],
        w_in_l[:, o_nat:o_nat + W_NAT],
        w_in_l[:, o_gla:o_gla + 1024],
        w_in_l[:, :W_MLA], zeros(512 - W_MLA),
        w_in_l[:, o_gla + 1024:o_gla + W_GLA], zeros(LANES - 2 * GLA_LOWRANK),
    ]
    return jnp.concatenate(cols, axis=1).astype(BF16)


MLA_TQ = 1024
MLA_TK = 768


def _mla_prep_kernel(p_ref, wq_ref, wk_ref, wv_ref, gcq_ref, gckv_ref, gq_ref, gk_ref, vone_ref,
                     c_ref, s1_ref, s2_ref, qc_ref, kc_ref, v_ref, *, n_lat_blk):
    ctx_w = jnp.where(pl.program_id(1) == n_lat_blk, 1.0, 0.0)
    p = p_ref[...].astype(F32)
    cq = p[:, :Q_LORA]
    r = lax.rsqrt(jnp.mean(cq * cq, axis=-1, keepdims=True) + EPS)
    cqn = (cq * r * gcq_ref[...]).astype(BF16)
    qraw = _dot(cqn, wq_ref[...])
    kv = p[:, Q_LORA:]
    ckv = kv[:, :KV_LORA]
    r2 = lax.rsqrt(jnp.mean(ckv * ckv, axis=-1, keepdims=True) + EPS)
    lane = lax.broadcasted_iota(jnp.int32, kv.shape, 1)
    lhs = (kv * jnp.where(lane < KV_LORA, r2, 1.0) * gckv_ref[...]).astype(BF16)
    kraw = _dot(lhs, wk_ref[...])
    vraw = _dot(lhs, wv_ref[...]) + vone_ref[...]
    cs = c_ref[...]
    s1 = s1_ref[...]
    s2 = s2_ref[...]

    def rope(t):
        return t * cs + pltpu.roll(t, HEAD_PAD - MLA_ROPE // 2, 1) * s1 + pltpu.roll(t, MLA_ROPE // 2, 1) * s2

    inv = 1.0 / MLA_QK
    for hd in range(MLA_HEADS):
        sl = slice(HEAD_PAD * hd, HEAD_PAD * (hd + 1))
        qh = qraw[:, sl]
        qn = qh * lax.rsqrt(jnp.sum(qh * qh, axis=-1, keepdims=True) * inv + EPS) * gq_ref[...]
        qc_ref[hd] = jnp.concatenate([rope(qn), qn], axis=1).astype(qc_ref.dtype)
        kh = kraw[:, sl]
        kn = kh * lax.rsqrt(jnp.sum(kh * kh, axis=-1, keepdims=True) * inv + EPS) * gk_ref[...]
        kc_ref[hd] = jnp.concatenate([rope(kn) * (1.0 - ctx_w), kn * ctx_w], axis=1).astype(kc_ref.dtype)
        v_ref[hd] = vraw[:, sl].astype(v_ref.dtype)


def _mla_weights(w_uq, w_ukv, g_cq, g_ckv, g_q, g_k):
    H = MLA_HEADS
    wq = jnp.pad(w_uq.reshape(Q_LORA, H, MLA_QK), ((0, 0), (0, 0), (0, HEAD_PAD - MLA_QK)))
    wq = wq.reshape(Q_LORA, H * HEAD_PAD).astype(BF16)
    wkv = w_ukv.reshape(KV_LORA, H, MLA_NOPE + MLA_V)
    wk = jnp.pad(wkv[:, :, :MLA_NOPE], ((0, 0), (0, 0), (0, HEAD_PAD - MLA_NOPE)))
    place = jnp.zeros((MLA_ROPE, H, HEAD_PAD), F32)
    place = place.at[jnp.arange(MLA_ROPE), :, MLA_NOPE + jnp.arange(MLA_ROPE)].set(1.0)
    wk = jnp.concatenate([wk, place, jnp.zeros((256 - KV_LORA - MLA_ROPE, H, HEAD_PAD), F32)], axis=0)
    wk = wk.reshape(256, H * HEAD_PAD).astype(BF16)
    wv = jnp.pad(wkv[:, :, MLA_NOPE:], ((0, 256 - KV_LORA), (0, 0), (0, HEAD_PAD - MLA_V)))
    wv = wv.reshape(256, H * HEAD_PAD).astype(BF16)
    vone = jnp.zeros((H, HEAD_PAD), F32).at[:, MLA_V].set(1.0).reshape(1, H * HEAD_PAD)
    gckv = jnp.concatenate([g_ckv, jnp.ones((256 - KV_LORA,), F32)]).reshape(1, 256)
    gq = jnp.pad(g_q * (MLA_QK ** -0.5 * LOG2E), (0, HEAD_PAD - MLA_QK)).reshape(1, HEAD_PAD)
    gk = jnp.pad(g_k, (0, HEAD_PAD - MLA_QK)).reshape(1, HEAD_PAD)
    return wq, wk, wv, g_cq.reshape(1, Q_LORA), gckv, gq, gk, vone


def _rope_tables(n_lat, n_ctx):
    quarter = MLA_ROPE // 4
    inv_freq = ROPE_BASE ** (-jnp.arange(quarter, dtype=F32) / quarter)
    t = jnp.arange(n_lat, dtype=jnp.int32)
    row = (t // GRID_W).astype(F32)
    col = (t % GRID_W).astype(F32)
    ang = jnp.concatenate([row[:, None] * inv_freq, col[:, None] * inv_freq], axis=-1)
    cos, sin = jnp.cos(ang), jnp.sin(ang)
    half = MLA_ROPE // 2
    ones = jnp.ones((n_lat, MLA_NOPE), F32)
    tail = jnp.ones((n_lat, HEAD_PAD - MLA_QK), F32)
    zl = jnp.zeros((n_lat, MLA_NOPE), F32)
    zh = jnp.zeros((n_lat, half), F32)
    zt = jnp.zeros((n_lat, HEAD_PAD - MLA_QK), F32)
    c = jnp.concatenate([ones, cos, cos, tail], axis=1)
    s1 = jnp.concatenate([zl, -sin, zh, zt], axis=1)
    s2 = jnp.concatenate([zl, zh, sin, zt], axis=1)
    c = jnp.concatenate([c, jnp.ones((n_ctx, HEAD_PAD), F32)], axis=0)
    s1 = jnp.concatenate([s1, jnp.zeros((n_ctx, HEAD_PAD), F32)], axis=0)
    s2 = jnp.concatenate([s2, jnp.zeros((n_ctx, HEAD_PAD), F32)], axis=0)
    return c, s1, s2


def _mla_prep(p3, mw, tabs):
    B, Nt, _ = p3.shape
    H = MLA_HEADS
    wq, wk, wv, gcq, gckv, gq, gk, vone = mw
    n_lat_blk = Nt // TOK_BLK - 1
    const = lambda shape: pl.BlockSpec(shape, lambda b, i: (0,) * len(shape))
    tab = pl.BlockSpec((TOK_BLK, HEAD_PAD), lambda b, i: (i, 0))
    out = lambda w: pl.BlockSpec((None, H, TOK_BLK, w), lambda b, i: (b, 0, i, 0))
    shp = lambda w: jax.ShapeDtypeStruct((B, H, Nt, w), BF16)
    return pl.pallas_call(
        functools.partial(_mla_prep_kernel, n_lat_blk=n_lat_blk),
        out_shape=(shp(2 * HEAD_PAD), shp(2 * HEAD_PAD), shp(HEAD_PAD)),
        grid=(B, Nt // TOK_BLK),
        in_specs=[
            pl.BlockSpec((None, TOK_BLK, 512), lambda b, i: (b, i, COL_MLA // 512)),
            const(wq.shape), const(wk.shape), const(wv.shape), const(gcq.shape), const(gckv.shape),
            const(gq.shape), const(gk.shape), const(vone.shape), tab, tab, tab,
        ],
        out_specs=(out(2 * HEAD_PAD), out(2 * HEAD_PAD), out(HEAD_PAD)),
        compiler_params=_cparams(("parallel", "parallel")),
        name="mla_prep",
    )(p3, wq, wk, wv, gcq, gckv, gq, gk, vone, *tabs)


def _softmax_step(q, kc, vc, m, acc):
    s = _dot_nt(q, kc)
    m_new = jnp.maximum(m, jnp.max(s, axis=-1, keepdims=True))
    p = jnp.exp2((s - m_new).astype(BF16))
    acc = jnp.exp2(m - m_new) * acc + _dot(p, vc)
    return m_new, acc


def _mla_attn_kernel(qc_ref, kc_ref, v_ref, o_ref, *, n_chunks):
    tq = qc_ref.shape[1]

    def body(c, carry):
        off = pl.multiple_of(c * MLA_TK, MLA_TK)
        return tuple(
            _softmax_step(qc_ref[hh], kc_ref[hh, pl.ds(off, MLA_TK), :], v_ref[hh, pl.ds(off, MLA_TK), :], *carry[hh])
            for hh in range(2))

    m0 = jnp.full((tq, 1), NEG_BIG, F32)
    a0 = jnp.zeros((tq, HEAD_PAD), F32)
    carry = lax.fori_loop(0, n_chunks, body, ((m0, a0), (m0, a0)), unroll=True)
    outs = [acc[:, :MLA_V] / acc[:, MLA_V:MLA_V + 1] for _, acc in carry]
    o_ref[...] = jnp.concatenate(outs, axis=-1).astype(o_ref.dtype)


def _mla_ctx_kernel(qc_ref, kc_ref, v_ref, o_ref):
    outs = []
    for hh in range(2):
        s = _dot_nt(qc_ref[hh], kc_ref[hh])
        p = jnp.exp2(s - jnp.max(s, axis=-1, keepdims=True))
        acc = _dot(p.astype(BF16), v_ref[hh])
        outs.append(acc[:, :MLA_V] / acc[:, MLA_V:MLA_V + 1])
    o_ref[...] = jnp.concatenate(outs, axis=-1).astype(o_ref.dtype)


def _mla_attention(qc, kc, v, n_lat, need_ctx):
    B, H, Nt, _ = kc.shape
    n_ctx = Nt - n_lat
    o_lat = pl.pallas_call(
        functools.partial(_mla_attn_kernel, n_chunks=Nt // MLA_TK),
        out_shape=jax.ShapeDtypeStruct((B, n_lat, BRANCH_W), BF16),
        grid=(B, H // 2, n_lat // MLA_TQ),
        in_specs=[
            pl.BlockSpec((None, 2, MLA_TQ, 2 * HEAD_PAD), lambda b, g, i: (b, g, i, 0)),
            pl.BlockSpec((None, 2, Nt, 2 * HEAD_PAD), lambda b, g, i: (b, g, 0, 0)),
            pl.BlockSpec((None, 2, Nt, HEAD_PAD), lambda b, g, i: (b, g, 0, 0)),
        ],
        out_specs=pl.BlockSpec((None, MLA_TQ, 2 * MLA_V), lambda b, g, i: (b, i, g)),
        compiler_params=_cparams(("parallel", "parallel", "arbitrary")),
        name="mla_attention",
    )(qc, kc, v)
    if not need_ctx:
        return o_lat
    cblk = n_lat // n_ctx
    cspec = lambda w: pl.BlockSpec((None, 2, n_ctx, w), lambda b, g: (b, g, cblk, 0))
    o_ctx = pl.pallas_call(
        _mla_ctx_kernel,
        out_shape=jax.ShapeDtypeStruct((B, n_ctx, BRANCH_W), BF16),
        grid=(B, H // 2),
        in_specs=[cspec(2 * HEAD_PAD), cspec(2 * HEAD_PAD), cspec(HEAD_PAD)],
        out_specs=pl.BlockSpec((None, n_ctx, 2 * MLA_V), lambda b, g: (b, 0, g)),
        compiler_params=_cparams(("parallel", "parallel")),
        name="mla_attention_ctx",
    )(qc, kc, v)
    return jnp.concatenate([o_lat, o_ctx], axis=1)


NAT_ROWS_BLK = 8
NAT_TQ = NAT_ROWS_BLK * GRID_W
NAT_WIN_ROWS = 16
NAT_WIN = NAT_WIN_ROWS * GRID_W
NAT_PREP_BLK = 1024


def _pair_norm(x, g, half0):
    x2 = x * x
    s0 = jnp.sum(jnp.where(half0, x2, 0.0), axis=-1, keepdims=True)
    s1 = jnp.sum(jnp.where(half0, 0.0, x2), axis=-1, keepdims=True)
    ms = jnp.where(half0, s0, s1) * (1.0 / NAT_DH)
    return x * lax.rsqrt(ms + EPS) * g


def _nat_kernel(q_ref, kl_ref, vl_ref, kc_ref, vc_ref, bias_ref, gq_ref, gk_ref, o_ref, kn_ref, vb_ref,
                *, n_lat, rows_total):
    rb = pl.program_id(2)
    half0 = lax.broadcasted_iota(jnp.int32, (1, LANES), 1) < NAT_DH

    @pl.when(rb == 0)
    def _():
        def body(c, carry):
            off = pl.multiple_of(c * NAT_PREP_BLK, NAT_PREP_BLK)
            kn_ref[pl.ds(off, NAT_PREP_BLK), :] = _pair_norm(
                kl_ref[pl.ds(off, NAT_PREP_BLK), :].astype(F32), gk_ref[...], half0).astype(BF16)
            vb_ref[pl.ds(off, NAT_PREP_BLK), :] = vl_ref[pl.ds(off, NAT_PREP_BLK), :].astype(BF16)
            return carry

        lax.fori_loop(0, n_lat // NAT_PREP_BLK, body, 0)
        kn_ref[n_lat:, :] = _pair_norm(kc_ref[...].astype(F32), gk_ref[...], half0).astype(BF16)
        vb_ref[n_lat:, :] = vc_ref[...].astype(BF16)

    q = _pair_norm(q_ref[...].astype(F32), gq_ref[...], half0)
    ws = jnp.clip(rb * NAT_ROWS_BLK - NAT_KH // 2, 0, rows_total - NAT_WIN_ROWS)
    off = pl.multiple_of(ws * GRID_W, GRID_W)
    kw = kn_ref[pl.ds(off, NAT_WIN), :]
    vw = vb_ref[pl.ds(off, NAT_WIN), :]
    kc = kn_ref[n_lat:, :]
    vc = vb_ref[n_lat:, :]
    outs = []
    for hh in range(2):
        sel = half0 if hh == 0 else jnp.logical_not(half0)
        qm = jnp.where(sel, q, 0.0).astype(BF16)
        sw = _dot_nt(qm, kw) + bias_ref[hh].astype(F32)
        sc = _dot_nt(qm, kc)
        m = jnp.maximum(jnp.max(sw, axis=-1, keepdims=True), jnp.max(sc, axis=-1, keepdims=True))
        pw = jnp.exp2(sw - m)
        pc = jnp.exp2(sc - m)
        l = jnp.sum(pw, axis=-1, keepdims=True) + jnp.sum(pc, axis=-1, keepdims=True)
        o = _dot(pw.astype(BF16), vw) + _dot(pc.astype(BF16), vc)
        outs.append(o / l)
    o_ref[...] = jnp.where(half0, outs[0], outs[1]).astype(o_ref.dtype)


def _nat_ctx_kernel(q_ref, kc_ref, vc_ref, gq_ref, gk_ref, o_ref):
    half0 = lax.broadcasted_iota(jnp.int32, (1, LANES), 1) < NAT_DH
    q = _pair_norm(q_ref[...].astype(F32), gq_ref[...], half0)
    kc = _pair_norm(kc_ref[...].astype(F32), gk_ref[...], half0).astype(BF16)
    vc = vc_ref[...].astype(BF16)
    outs = []
    for hh in range(2):
        sel = half0 if hh == 0 else jnp.logical_not(half0)
        s = _dot_nt(jnp.where(sel, q, 0.0).astype(BF16), kc)
        p = jnp.exp2(s - jnp.max(s, axis=-1, keepdims=True))
        outs.append(_dot(p.astype(BF16), vc) / jnp.sum(p, axis=-1, keepdims=True))
    o_ref[...] = jnp.where(half0, outs[0], outs[1]).astype(o_ref.dtype)


def _nat_bias_tables(rpb, rows_total):
    rbs = np.array([0, min(NAT_ROWS_BLK, rows_total - NAT_ROWS_BLK), rows_total - NAT_ROWS_BLK])
    ws = np.clip(rbs - NAT_KH // 2, 0, rows_total - NAT_WIN_ROWS)
    qrow = rbs[:, None] + np.arange(NAT_ROWS_BLK)[None, :]
    r0 = np.clip(qrow - NAT_KH // 2, 0, rows_total - NAT_KH)
    krow = ws[:, None] + np.arange(NAT_WIN_ROWS)[None, :]
    row_ok = (krow[:, None, :] >= r0[:, :, None]) & (krow[:, None, :] < r0[:, :, None] + NAT_KH)
    dr = np.clip(krow[:, None, :] - qrow[:, :, None] + NAT_KH - 1, 0, 2 * NAT_KH - 2)
    col = np.arange(GRID_W)
    c0 = np.clip(col - NAT_KW // 2, 0, GRID_W - NAT_KW)
    col_ok = (col[None, :] >= c0[:, None]) & (col[None, :] < c0[:, None] + NAT_KW)
    dc = np.clip(col[None, :] - col[:, None], -(NAT_KW - 1), NAT_KW - 1) + NAT_KW - 1
    oh_r = np.eye(2 * NAT_KH - 1, dtype=np.float32)[dr]
    oh_c = np.eye(2 * NAT_KW - 1, dtype=np.float32)[dc]
    hi = lax.Precision.HIGHEST
    by_col = jnp.einsum('hrc,qkc->hrqk', rpb.astype(F32), oh_c, precision=hi)
    tab = jnp.einsum('vijr,hrqk->vhiqjk', oh_r, by_col, precision=hi)
    ok = row_ok[:, None, :, None, :, None] & col_ok[None, None, None, :, None, :]
    tab = jnp.where(jnp.asarray(ok), tab * LOG2E, NEG_BIG).astype(BF16)
    return tab.reshape(3, NAT_HEADS // 2, 2, NAT_TQ, NAT_WIN)


def _nat_attention(p3, rpb, g_q, g_k, n_lat, need_ctx):
    B, Nt, _ = p3.shape
    n_ctx = Nt - n_lat
    rows_total = n_lat // GRID_W
    n_rb = rows_total // NAT_ROWS_BLK
    bias = _nat_bias_tables(rpb, rows_total)
    gq = jnp.tile(g_q * (NAT_DH ** -0.5 * LOG2E), 2).reshape(1, LANES)
    gk = jnp.tile(g_k, 2).reshape(1, LANES)
    cq, ck, cv = ((COL_NAT + j * BRANCH_W) // LANES for j in range(3))
    cblk = n_lat // n_ctx

    def variant(rb):
        return jnp.where(rb == 0, 0, jnp.where(rb == n_rb - 1, 2, 1))

    vec = pl.BlockSpec((1, LANES), lambda b, g, rb: (0, 0))
    o_lat = pl.pallas_call(
        functools.partial(_nat_kernel, n_lat=n_lat, rows_total=rows_total),
        out_shape=jax.ShapeDtypeStruct((B, n_lat, BRANCH_W), BF16),
        grid=(B, NAT_HEADS // 2, n_rb),
        in_specs=[
            pl.BlockSpec((None, NAT_TQ, LANES), lambda b, g, rb: (b, rb, cq + g)),
            pl.BlockSpec((None, n_lat, LANES), lambda b, g, rb: (b, 0, ck + g)),
            pl.BlockSpec((None, n_lat, LANES), lambda b, g, rb: (b, 0, cv + g)),
            pl.BlockSpec((None, n_ctx, LANES), lambda b, g, rb: (b, cblk, ck + g)),
            pl.BlockSpec((None, n_ctx, LANES), lambda b, g, rb: (b, cblk, cv + g)),
            pl.BlockSpec((None, None, 2, NAT_TQ, NAT_WIN), lambda b, g, rb: (variant(rb), g, 0, 0, 0)),
            vec, vec,
        ],
        out_specs=pl.BlockSpec((None, NAT_TQ, LANES), lambda b, g, rb: (b, rb, g)),
        scratch_shapes=[pltpu.VMEM((Nt, LANES), BF16), pltpu.VMEM((Nt, LANES), BF16)],
        compiler_params=_cparams(("parallel", "parallel", "arbitrary")),
        name="nat_attention",
    )(p3, p3, p3, p3, p3, bias, gq, gk)
    if not need_ctx:
        return o_lat
    vec2 = pl.BlockSpec((1, LANES), lambda b, g: (0, 0))
    o_ctx = pl.pallas_call(
        _nat_ctx_kernel,
        out_shape=jax.ShapeDtypeStruct((B, n_ctx, BRANCH_W), BF16),
        grid=(B, NAT_HEADS // 2),
        in_specs=[
            pl.BlockSpec((None, n_ctx, LANES), lambda b, g: (b, cblk, cq + g)),
            pl.BlockSpec((None, n_ctx, LANES), lambda b, g: (b, cblk, ck + g)),
            pl.BlockSpec((None, n_ctx, LANES), lambda b, g: (b, cblk, cv + g)),
            vec2, vec2,
        ],
        out_specs=pl.BlockSpec((None, n_ctx, LANES), lambda b, g: (b, 0, g)),
        compiler_params=_cparams(("parallel", "parallel")),
        name="nat_attention_ctx",
    )(p3, p3, p3, gq, gk)
    return jnp.concatenate([o_lat, o_ctx], axis=1)


def _log_sigmoid(x):
    return jnp.minimum(x, 0.0) - jnp.log(1.0 + jnp.exp(-jnp.abs(x)))


def _run_block(operands, s_ref, o_ref, rev):
    qs, k, v, a, heads = operands()
    T, W = qs.shape
    n_ch, nsub, nsb = T // CHUNK, CHUNK // SUB, T // SUB
    row = lax.broadcasted_iota(jnp.int32, (T, T), 0)
    col = lax.broadcasted_iota(jnp.int32, (T, T), 1)
    same_chunk = (row // CHUNK) == (col // CHUNK)
    tri = jnp.where(same_chunk & ((row <= col) if rev else (row >= col)), 1.0, 0.0).astype(BF16)
    r64 = lax.broadcasted_iota(jnp.int32, (CHUNK, CHUNK), 0)
    c64 = lax.broadcasted_iota(jnp.int32, (CHUNK, CHUNK), 1)
    causal = (r64 <= c64) if rev else (r64 >= c64)
    hi = a.astype(BF16)
    mid = (a - hi.astype(F32)).astype(BF16)
    bc = _dot(tri, hi) + _dot(tri, mid)
    bex = bc - a
    first = [SUB * s + (SUB - 1 if rev else 0) for s in range(nsb)]
    last = [SUB * s + (0 if rev else SUB - 1) for s in range(nsb)]
    r_start = [bex[r:r + 1, :] for r in first]
    r_end = [bc[r:r + 1, :] for r in last]
    bend = [r_end[nsub * c + (0 if rev else nsub - 1)] for c in range(n_ch)]

    def spread(rows):
        return jnp.concatenate([jnp.broadcast_to(r, (SUB, W)) for r in rows], axis=0)

    rs_full = spread(r_start)
    qt = qs * jnp.exp(bc - rs_full)
    q_in = qt * spread([jnp.exp(r) for r in r_start])
    ke = k * jnp.exp(spread(r_end) - bc)
    k_out = ke * spread([jnp.exp(bend[s // nsub] - r_end[s]) for s in range(nsb)])
    kd = k * jnp.exp(jnp.minimum(rs_full - bc, EXP_CAP))
    zeros = jnp.zeros((SUB, W), F32)

    def keys_for(c, i):
        rows = []
        for j in range(nsub):
            sj, si = nsub * c + j, nsub * c + i
            blk = slice(SUB * sj, SUB * (sj + 1))
            if j == i:
                rows.append(kd[blk])
            elif (j > i) if rev else (j < i):
                rows.append(ke[blk] if abs(i - j) == 1 else ke[blk] * jnp.exp(r_start[si] - r_end[sj]))
            else:
                rows.append(zeros)
        return jnp.concatenate(rows, axis=0).astype(BF16)

    kts = [[keys_for(c, i) for i in range(nsub)] for c in range(n_ch)]
    vb = v.astype(BF16)
    yield
    intra, incr, q_state = [], [], []
    for hd, (ks, kmask, vs) in enumerate(heads):
        ksl = slice(ks, ks + LANES)
        vsl = slice(vs, vs + LANES)

        def msk(t, kmask=kmask):
            return t if kmask is None else jnp.where(kmask, t, 0.0)

        qt_h = msk(qt[:, ksl]).astype(BF16)
        ko_h = msk(k_out[:, ksl]).astype(BF16)
        q_state.append(msk(q_in[:, ksl]).astype(BF16))
        intra_h, incr_h = [], []
        for c in range(n_ch):
            cs = slice(CHUNK * c, CHUNK * (c + 1))
            blocks = [_dot_nt(qt_h[SUB * (nsub * c + i):SUB * (nsub * c + i + 1), :], kts[c][i][:, ksl])
                      for i in range(nsub)]
            sc = jnp.where(causal, jnp.concatenate(blocks, axis=0), 0.0).astype(BF16)
            intra_h.append(_dot(sc, vb[cs, vsl]))
            incr_h.append(_dot(v[cs, vsl].T.astype(BF16), ko_h[cs]))
        intra.append(intra_h)
        incr.append(incr_h)
    yield
    states = [s_ref[hd] for hd in range(len(heads))]
    for c in (range(n_ch - 1, -1, -1) if rev else range(n_ch)):
        cs = slice(CHUNK * c, CHUNK * (c + 1))
        dec = jnp.exp(bend[c])
        outs = []
        for hd, (ks, _, _) in enumerate(heads):
            outs.append(intra[hd][c] + _dot_nt(q_state[hd][cs], states[hd].astype(BF16)))
            states[hd] = states[hd] * dec[:, ks:ks + LANES] + incr[hd][c]
        o_ref[cs, :] = jnp.concatenate(outs, axis=-1).astype(o_ref.dtype)
        yield
    for hd, st in enumerate(states):
        s_ref[hd] = st


def _hgrn2_operands(q_ref, f_ref, v_ref, lb):
    q = q_ref[...].astype(F32)
    f = f_ref[...].astype(F32)
    loglb, log1mlb, oneml = lb[0:1, :], lb[1:2, :], lb[2:3, :]
    u = log1mlb + _log_sigmoid(f)
    a = jnp.maximum(u, loglb) + jnp.log(1.0 + jnp.exp(-jnp.abs(u - loglb)))
    k = oneml * jax.nn.sigmoid(-f)
    qs = q * jax.nn.sigmoid(q) * (HG_DK ** -0.5)
    heads = [(LANES * h, None, LANES * h) for h in range(HG_HEADS)]
    return qs, k, v_ref[...].astype(F32), a, heads


def _gla_operands(q_ref, k_ref, v_ref, r_ref, w2, b2):
    g = _dot(r_ref[...].astype(BF16), w2) + b2
    a = _log_sigmoid(g) * (1.0 / GLA_NORMALIZER)
    qs = q_ref[...].astype(F32) * (GLA_DK ** -0.5)
    half0 = lax.broadcasted_iota(jnp.int32, (1, LANES), 1) < GLA_DK
    heads = [(LANES * (h // 2), half0 if h % 2 == 0 else jnp.logical_not(half0), LANES * h)
             for h in range(GLA_HEADS)]
    return qs, k_ref[...].astype(F32), v_ref[...].astype(F32), a, heads


def _scans_kernel(hq_f, hf_f, hv_f, hq_b, hf_b, hv_b, lb_ref, gq_f, gk_f, gv_f, gr_f, gq_b, gk_b, gv_b, gr_b,
                  w2_ref, b2_ref, oh_f, oh_b, og_f, og_b, s_ref):
    @pl.when(pl.program_id(1) == 0)
    def _():
        s_ref[...] = jnp.zeros_like(s_ref)

    streams = [
        _run_block(functools.partial(_hgrn2_operands, hq_f, hf_f, hv_f, lb_ref[0]), s_ref.at[0], oh_f, False),
        _run_block(functools.partial(_hgrn2_operands, hq_b, hf_b, hv_b, lb_ref[1]), s_ref.at[1], oh_b, True),
        _run_block(functools.partial(_gla_operands, gq_f, gk_f, gv_f, gr_f, w2_ref[0], b2_ref[0]),
                   s_ref.at[2], og_f, False),
        _run_block(functools.partial(_gla_operands, gq_b, gk_b, gv_b, gr_b, w2_ref[1], b2_ref[1]),
                   s_ref.at[3], og_b, True),
    ]
    done = object()
    while streams:
        streams = [g for g in streams if next(g, done) is not done]


def _linear_scans(p3, lbvec, w2pad, b2, n_lat):
    B, Nt, _ = p3.shape
    nb = Nt // TOK_BLK
    n_lat_blk = n_lat // TOK_BLK
    fwd = lambda s: jnp.where(s == 0, n_lat_blk, s - 1)
    bwd = lambda s: jnp.where(s == 0, n_lat_blk, n_lat_blk - s)
    wqk = GLA_HEADS * GLA_DK
    spec = lambda blk, w, col: pl.BlockSpec((None, TOK_BLK, w), lambda b, s: (b, blk(s), col // w))
    hg = lambda blk, f_col: [spec(blk, BRANCH_W, COL_HG), spec(blk, BRANCH_W, f_col),
                             spec(blk, BRANCH_W, COL_HG + 3 * BRANCH_W)]
    gla = lambda blk: [spec(blk, wqk, COL_GLA), spec(blk, wqk, COL_GLA + wqk), spec(blk, BRANCH_W, COL_GLA + 2 * wqk),
                       spec(blk, LANES, COL_GLR)]
    const = lambda shape: pl.BlockSpec(shape, lambda b, s: (0,) * len(shape))
    out = lambda blk: pl.BlockSpec((None, TOK_BLK, BRANCH_W), lambda b, s: (b, blk(s), 0))
    shp = jax.ShapeDtypeStruct((B, Nt, BRANCH_W), BF16)
    return pl.pallas_call(
        _scans_kernel,
        out_shape=(shp, shp, shp, shp),
        grid=(B, nb),
        in_specs=(hg(fwd, COL_HG + BRANCH_W) + hg(bwd, COL_HG + 2 * BRANCH_W) + [const(lbvec.shape)]
                  + gla(fwd) + gla(bwd) + [const(w2pad.shape), const(b2.shape)]),
        out_specs=(out(fwd), out(bwd), out(fwd), out(bwd)),
        scratch_shapes=[pltpu.VMEM((4, HG_HEADS, LANES, LANES), F32)],
        compiler_params=_cparams(("parallel", "arbitrary")),
        name="linear_scans",
    )(*([p3] * 6), lbvec, *([p3] * 8), w2pad, b2)


def _group_norm(o, g):
    parts = []
    for hd in range(o.shape[1] // LANES):
        oh = o[:, LANES * hd:LANES * (hd + 1)]
        parts.append(oh * lax.rsqrt(jnp.mean(oh * oh, axis=-1, keepdims=True) + EPS))
    return jnp.concatenate(parts, axis=-1) * g


def _merge_kernel(x_ref, h_ref, za_ref, zb_ref, zc_ref, zd_ref, ya_ref, yb_ref, hf_ref, hb_ref, gf_ref, gb_ref,
                  wbr_ref, wmg_ref, bmg_ref, wout_ref, ghg_ref, ggl_ref, m_ref, o_ref):
    h = h_ref[...]
    ys = (
        ya_ref[...].astype(F32),
        yb_ref[...].astype(F32),
        _group_norm(hf_ref[...].astype(F32) + hb_ref[...].astype(F32), ghg_ref[...]),
        _group_norm(gf_ref[...].astype(F32) + gb_ref[...].astype(F32), ggl_ref[...]),
    )
    zs = (za_ref, zb_ref, zc_ref, zd_ref)
    acc = None
    for br in range(N_BRANCH):
        gate = jax.nn.sigmoid(_dot(h, wmg_ref[br]) + bmg_ref[br])
        z = zs[br][...].astype(F32)
        yz = (ys[br] * (z * jax.nn.sigmoid(z))).astype(BF16)
        part = gate * _dot(yz, wbr_ref[br])
        acc = part if acc is None else acc + part
    out = _dot(acc.astype(BF16), wout_ref[...])
    o_ref[...] = x_ref[...] + m_ref[2:3, :] * out


def _merge(xs, h, p3, ys, w_br, w_merge, b_merge, w_out, g_hg, g_gla, modsel, n_lat_blk, n_blk):
    B, _, D = xs.shape
    tok = lambda w: pl.BlockSpec((None, TOK_BLK, w), lambda b, i: (b, i, 0))
    zspec = lambda br: pl.BlockSpec((None, TOK_BLK, BRANCH_W), lambda b, i: (b, i, COL_Z // BRANCH_W + br))
    once = pl.Buffered(1)
    const = lambda shape: pl.BlockSpec(shape, lambda b, i: (0,) * len(shape), pipeline_mode=once)
    ghg = jnp.tile(g_hg, HG_HEADS).reshape(1, BRANCH_W)
    ggl = jnp.tile(g_gla, GLA_HEADS).reshape(1, BRANCH_W)
    bm = b_merge.reshape(N_BRANCH, 1, D)
    return pl.pallas_call(
        _merge_kernel,
        out_shape=jax.ShapeDtypeStruct((B, n_blk * TOK_BLK, D), F32),
        grid=(B, n_blk),
        in_specs=[tok(D), tok(D)] + [zspec(br) for br in range(N_BRANCH)] + [tok(BRANCH_W)] * 6 + [
            const(w_br.shape), const(w_merge.shape), const(bm.shape), const(w_out.shape),
            const(ghg.shape), const(ggl.shape),
            pl.BlockSpec((None, None, 3, D), lambda b, i: (b, jnp.where(i >= n_lat_blk, 1, 0), 0, 0)),
        ],
        out_specs=tok(D),
        compiler_params=_cparams(("parallel", "parallel"), vmem_mb=56),
        name="branch_merge",
    )(xs, h, p3, p3, p3, p3, *ys, w_br, w_merge, bm, w_out, ghg, ggl, modsel)


def kernel(x, c, ctx, c_ctx, ada_w, ada_b, norm_g, w_in, mla_w_uq, mla_w_ukv, mla_g_cq, mla_g_ckv, mla_g_q,
           mla_g_k, nat_rpb, nat_g_q, nat_g_k, hg_lb_logits, hg_g_o, gla_w2, gla_b2, gla_g_o, w_br, w_merge,
           b_merge, w_out):
    B, n_lat, D = x.shape
    n_ctx = ctx.shape[1]
    depth = ada_w.shape[0]
    Nt = n_lat + n_ctx
    n_lat_blk = n_lat // TOK_BLK
    assert n_ctx == TOK_BLK and n_lat % NAT_PREP_BLK == 0 and (B * Nt) % MM_ROW_BLK == 0
    assert Nt % MLA_TK == 0 and n_lat % MLA_TQ == 0

    cc = jnp.zeros((8, D), F32).at[:B].set(c).at[B].set(c_ctx)
    ada = _ada_vectors(cc, ada_w.astype(BF16), ada_b).reshape(depth, 8, 3, D)
    sm = jax.nn.softmax(hg_lb_logits.astype(F32), axis=0)
    lower = jnp.maximum(jnp.cumsum(sm, axis=0) - sm[0], 0.0)
    tabs = _rope_tables(n_lat, n_ctx)

    xs = jnp.concatenate([x, ctx], axis=1)
    for l in range(depth):
        need_ctx = l < depth - 1
        modsel = jnp.stack([ada[l, :B], jnp.broadcast_to(ada[l, B], (B, 3, D))], axis=1)
        h = _norm_modulate(xs, norm_g[l], modsel, n_lat_blk)
        p3 = _in_projection(h.reshape(B * Nt, D), _pack_w_in(w_in[l]), BF16).reshape(B, Nt, IN_W_PAD)

        mw = _mla_weights(mla_w_uq[l], mla_w_ukv[l], mla_g_cq[l], mla_g_ckv[l], mla_g_q[l], mla_g_k[l])
        y_a = _mla_attention(*_mla_prep(p3, mw, tabs), n_lat, need_ctx)
        y_b = _nat_attention(p3, nat_rpb[l], nat_g_q[l], nat_g_k[l], n_lat, need_ctx)

        lb = lower[l]
        lbvec = jnp.stack([jnp.log(lb), jnp.log1p(-lb), 1.0 - lb], axis=1)
        w2pad = jnp.zeros((2, LANES, GLA_HEADS * GLA_DK), F32)
        for d in range(2):
            w2pad = w2pad.at[d, GLA_LOWRANK * d:GLA_LOWRANK * (d + 1)].set(gla_w2[l, d])
        scans = _linear_scans(p3, lbvec, w2pad.astype(BF16), gla_b2[l].reshape(2, 1, -1), n_lat)

        n_blk = Nt // TOK_BLK if need_ctx else n_lat_blk
        xs = _merge(xs, h, p3, (y_a, y_b, *scans), w_br[l].astype(BF16), w_merge[l].astype(BF16), b_merge[l],
                    w_out[l].astype(BF16), hg_g_o[l], gla_g_o[l], modsel, n_lat_blk, n_blk)
    return xs[:, :n_lat]
```

```python
import functools

import jax
import jax.numpy as jnp
import numpy as np
from jax import lax
from jax.experimental import pallas as pl
from jax.experimental.pallas import tpu as pltpu

F32 = jnp.float32
BF16 = jnp.bfloat16

GRID_W = 64
N_BRANCH = 4
BRANCH_W = 512
MLA_HEADS = 8
MLA_NOPE = 64
MLA_ROPE = 32
MLA_V = 64
MLA_QK = MLA_NOPE + MLA_ROPE
Q_LORA = 256
KV_LORA = 128
ROPE_BASE = 10000.0
NAT_HEADS = 8
NAT_DH = 64
NAT_KH = 8
NAT_KW = 16
HG_HEADS = 4
HG_DK = 128
GLA_HEADS = 4
GLA_DK = 64
GLA_LOWRANK = 16
GLA_NORMALIZER = 16.0
EPS = 1e-6

W_MLA = Q_LORA + KV_LORA + MLA_ROPE
W_NAT = 3 * BRANCH_W
W_HG = 4 * BRANCH_W
W_GLA = 2 * GLA_HEADS * GLA_DK + BRANCH_W + 2 * GLA_LOWRANK

LANES = 128
HEAD_PAD = 128
TOK_BLK = 256

COL_HG = 0
COL_Z = COL_HG + W_HG
COL_NAT = COL_Z + N_BRANCH * BRANCH_W
COL_GLA = COL_NAT + W_NAT
COL_MLA = COL_GLA + 1024
COL_GLR = COL_MLA + 512
IN_W_PAD = COL_GLR + LANES
MM_COL_BLK = IN_W_PAD // 3
MM_ROW_BLK = 512

SUB = 16
CHUNK = 64
EXP_CAP = 80.0
NEG_BIG = -1e30
LOG2E = 1.4426950408889634


def _cparams(sem, vmem_mb=48):
    return pltpu.CompilerParams(dimension_semantics=sem, vmem_limit_bytes=vmem_mb * 1024 * 1024)


def _dot(a, b):
    return jnp.dot(a, b, preferred_element_type=F32)


def _dot_nt(a, b):
    return lax.dot_general(a, b, (((1,), (1,)), ((), ())), preferred_element_type=F32)


def _ada_kernel(c_ref, w_ref, b_ref, o_ref):
    c = c_ref[...]
    a = (c * jax.nn.sigmoid(c)).astype(BF16)
    o_ref[...] = _dot(a, w_ref[...]) + b_ref[...]


def _ada_vectors(cc, ada_w, ada_b):
    L, D, D3 = ada_w.shape
    tn = 1024
    return pl.pallas_call(
        _ada_kernel,
        out_shape=jax.ShapeDtypeStruct((L, cc.shape[0], D3), F32),
        grid=(L, D3 // tn),
        in_specs=[
            pl.BlockSpec((cc.shape[0], D), lambda l, j: (0, 0)),
            pl.BlockSpec((None, D, tn), lambda l, j: (l, 0, j)),
            pl.BlockSpec((None, 1, tn), lambda l, j: (l, 0, j)),
        ],
        out_specs=pl.BlockSpec((None, cc.shape[0], tn), lambda l, j: (l, 0, j)),
        compiler_params=_cparams(("parallel", "parallel")),
        name="ada_vectors",
    )(cc, ada_w, ada_b.reshape(L, 1, D3))


def _normmod_kernel(x_ref, g_ref, m_ref, h_ref):
    x = x_ref[...]
    y = x * lax.rsqrt(jnp.mean(x * x, axis=-1, keepdims=True) + EPS) * g_ref[...]
    h_ref[...] = (y * (1.0 + m_ref[1:2, :]) + m_ref[0:1, :]).astype(h_ref.dtype)


def _norm_modulate(xs, g, modsel, n_lat_blk):
    B, Nt, D = xs.shape
    return pl.pallas_call(
        _normmod_kernel,
        out_shape=jax.ShapeDtypeStruct((B, Nt, D), BF16),
        grid=(B, Nt // TOK_BLK),
        in_specs=[
            pl.BlockSpec((None, TOK_BLK, D), lambda b, i: (b, i, 0)),
            pl.BlockSpec((1, D), lambda b, i: (0, 0)),
            pl.BlockSpec((None, None, 3, D), lambda b, i: (b, jnp.where(i >= n_lat_blk, 1, 0), 0, 0)),
        ],
        out_specs=pl.BlockSpec((None, TOK_BLK, D), lambda b, i: (b, i, 0)),
        compiler_params=_cparams(("parallel", "parallel")),
        name="norm_modulate",
    )(xs, g.reshape(1, D), modsel)


def _mm_kernel(a_ref, w_ref, o_ref):
    o_ref[...] = _dot(a_ref[...], w_ref[...]).astype(o_ref.dtype)


def _in_projection(h2, w, out_dtype):
    M, K = h2.shape
    N = w.shape[1]
    return pl.pallas_call(
        _mm_kernel,
        out_shape=jax.ShapeDtypeStruct((M, N), out_dtype),
        grid=(N // MM_COL_BLK, M // MM_ROW_BLK),
        in_specs=[
            pl.BlockSpec((MM_ROW_BLK, K), lambda j, i: (i, 0)),
            pl.BlockSpec((K, MM_COL_BLK), lambda j, i: (0, j)),
        ],
        out_specs=pl.BlockSpec((MM_ROW_BLK, MM_COL_BLK), lambda j, i: (i, j)),
        compiler_params=_cparams(("parallel", "parallel")),
        name="in_projection",
    )(h2, w)


def _pack_w_in(w_in_l):
    o_nat = W_MLA
    o_hg = o_nat + W_NAT
    o_gla = o_hg + W_HG
    o_z = o_gla + W_GLA
    D = w_in_l.shape[0]
    zeros = lambda n: jnp.zeros((D, n), w_in_l.dtype)
    cols = [
        w_in_l[:, o_hg:o_hg + W_HG],
        w_in_l[:, o_z:o_z + N_BRANCH * BRANCH_W],
        w_in_l[:, o_nat:o_nat + W_NAT],
        w_in_l[:, o_gla:o_gla + 1024],
        w_in_l[:, :W_MLA], zeros(512 - W_MLA),
        w_in_l[:, o_gla + 1024:o_gla + W_GLA], zeros(LANES - 2 * GLA_LOWRANK),
    ]
    return jnp.concatenate(cols, axis=1).astype(BF16)


MLA_TQ = 1024
MLA_TK = 768


def _mla_prep_kernel(p_ref, wq_ref, wk_ref, wv_ref, gcq_ref, gckv_ref, gq_ref, gk_ref, vone_ref,
                     c_ref, s_ref, qc_ref, kc_ref, v_ref, *, n_lat_blk):
    ctx_w = jnp.where(pl.program_id(1) == n_lat_blk, 1.0, 0.0)
    hw = MLA_HEADS * HEAD_PAD
    p = p_ref[...].astype(F32)
    cq = p[:, :Q_LORA]
    r = lax.rsqrt(jnp.mean(cq * cq, axis=-1, keepdims=True) + EPS)
    cqn = (cq * r * gcq_ref[...]).astype(BF16)
    qraw = _dot(cqn, wq_ref[...])
    kv = p[:, Q_LORA:]
    ckv = kv[:, :KV_LORA]
    r2 = lax.rsqrt(jnp.mean(ckv * ckv, axis=-1, keepdims=True) + EPS)
    lane = lax.broadcasted_iota(jnp.int32, kv.shape, 1)
    lhs = (kv * jnp.where(lane < KV_LORA, r2, 1.0) * gckv_ref[...]).astype(BF16)
    kraw = _dot(lhs, wk_ref[...])
    vraw = _dot(lhs, wv_ref[...]) + vone_ref[...]
    cs = c_ref[...]
    sn = s_ref[...]
    inv = 1.0 / MLA_QK
    for hd in range(MLA_HEADS):
        sl = slice(HEAD_PAD * hd, HEAD_PAD * (hd + 1))
        sw = slice(hw + HEAD_PAD * hd, hw + HEAD_PAD * (hd + 1))
        qh = qraw[:, sl]
        rq = lax.rsqrt(jnp.sum(qh * qh, axis=-1, keepdims=True) * inv + EPS)
        qn = qh * rq * gq_ref[0:1, :]
        q_rot = qn * cs + qraw[:, sw] * rq * gq_ref[1:2, :] * sn
        qc_ref[hd] = jnp.concatenate([q_rot, qn], axis=1).astype(qc_ref.dtype)
        kh = kraw[:, sl]
        rk = lax.rsqrt(jnp.sum(kh * kh, axis=-1, keepdims=True) * inv + EPS)
        kn = kh * rk * gk_ref[0:1, :]
        k_rot = kn * cs + kraw[:, sw] * rk * gk_ref[1:2, :] * sn
        kc_ref[hd] = jnp.concatenate([k_rot * (1.0 - ctx_w), kn * ctx_w], axis=1).astype(kc_ref.dtype)
        v_ref[hd] = vraw[:, sl].astype(v_ref.dtype)


def _mla_weights(w_uq, w_ukv, g_cq, g_ckv, g_q, g_k):
    H = MLA_HEADS
    half = MLA_ROPE // 2
    lo, mid, hi = MLA_NOPE, MLA_NOPE + half, MLA_QK

    def swapped(t):
        z = jnp.zeros_like(t)
        return z.at[..., lo:mid].set(-t[..., mid:hi]).at[..., mid:hi].set(t[..., lo:mid])

    def both(t):
        return jnp.concatenate([t, swapped(t)], axis=1).reshape(t.shape[0], 2 * H * HEAD_PAD).astype(BF16)

    wq = jnp.pad(w_uq.reshape(Q_LORA, H, MLA_QK), ((0, 0), (0, 0), (0, HEAD_PAD - MLA_QK)))
    wkv = w_ukv.reshape(KV_LORA, H, MLA_NOPE + MLA_V)
    wk = jnp.pad(wkv[:, :, :MLA_NOPE], ((0, 0), (0, 0), (0, HEAD_PAD - MLA_NOPE)))
    place = jnp.zeros((MLA_ROPE, H, HEAD_PAD), F32)
    place = place.at[jnp.arange(MLA_ROPE), :, MLA_NOPE + jnp.arange(MLA_ROPE)].set(1.0)
    wk = jnp.concatenate([wk, place, jnp.zeros((256 - KV_LORA - MLA_ROPE, H, HEAD_PAD), F32)], axis=0)
    wv = jnp.pad(wkv[:, :, MLA_NOPE:], ((0, 256 - KV_LORA), (0, 0), (0, HEAD_PAD - MLA_V)))
    wv = wv.reshape(256, H * HEAD_PAD).astype(BF16)
    vone = jnp.zeros((H, HEAD_PAD), F32).at[:, MLA_V].set(1.0).reshape(1, H * HEAD_PAD)
    gckv = jnp.concatenate([g_ckv, jnp.ones((256 - KV_LORA,), F32)]).reshape(1, 256)
    gq = jnp.pad(g_q * (MLA_QK ** -0.5 * LOG2E), (0, HEAD_PAD - MLA_QK))
    gk = jnp.pad(g_k, (0, HEAD_PAD - MLA_QK))
    unsign = jnp.where(jnp.arange(HEAD_PAD) < mid, -1.0, 1.0)
    gq2 = jnp.stack([gq, swapped(gq) * unsign], axis=0)
    gk2 = jnp.stack([gk, swapped(gk) * unsign], axis=0)
    return both(wq), both(wk), wv, g_cq.reshape(1, Q_LORA), gckv, gq2, gk2, vone


def _rope_tables(n_lat, n_ctx):
    quarter = MLA_ROPE // 4
    inv_freq = ROPE_BASE ** (-jnp.arange(quarter, dtype=F32) / quarter)
    t = jnp.arange(n_lat, dtype=jnp.int32)
    row = (t // GRID_W).astype(F32)
    col = (t % GRID_W).astype(F32)
    ang = jnp.concatenate([row[:, None] * inv_freq, col[:, None] * inv_freq], axis=-1)
    cos, sin = jnp.cos(ang), jnp.sin(ang)
    ones = jnp.ones((n_lat, MLA_NOPE), F32)
    tail = jnp.ones((n_lat, HEAD_PAD - MLA_QK), F32)
    zl = jnp.zeros((n_lat, MLA_NOPE), F32)
    zt = jnp.zeros((n_lat, HEAD_PAD - MLA_QK), F32)
    c = jnp.concatenate([ones, cos, cos, tail], axis=1)
    s = jnp.concatenate([zl, sin, sin, zt], axis=1)
    c = jnp.concatenate([c, jnp.ones((n_ctx, HEAD_PAD), F32)], axis=0)
    s = jnp.concatenate([s, jnp.zeros((n_ctx, HEAD_PAD), F32)], axis=0)
    return c, s


def _mla_prep(p3, mw, tabs):
    B, Nt, _ = p3.shape
    H = MLA_HEADS
    wq, wk, wv, gcq, gckv, gq, gk, vone = mw
    n_lat_blk = Nt // TOK_BLK - 1
    const = lambda shape: pl.BlockSpec(shape, lambda b, i: (0,) * len(shape))
    tab = pl.BlockSpec((TOK_BLK, HEAD_PAD), lambda b, i: (i, 0))
    out = lambda w: pl.BlockSpec((None, H, TOK_BLK, w), lambda b, i: (b, 0, i, 0))
    shp = lambda w: jax.ShapeDtypeStruct((B, H, Nt, w), BF16)
    return pl.pallas_call(
        functools.partial(_mla_prep_kernel, n_lat_blk=n_lat_blk),
        out_shape=(shp(2 * HEAD_PAD), shp(2 * HEAD_PAD), shp(HEAD_PAD)),
        grid=(B, Nt // TOK_BLK),
        in_specs=[
            pl.BlockSpec((None, TOK_BLK, 512), lambda b, i: (b, i, COL_MLA // 512)),
            const(wq.shape), const(wk.shape), const(wv.shape), const(gcq.shape), const(gckv.shape),
            const(gq.shape), const(gk.shape), const(vone.shape), tab, tab,
        ],
        out_specs=(out(2 * HEAD_PAD), out(2 * HEAD_PAD), out(HEAD_PAD)),
        compiler_params=_cparams(("parallel", "parallel")),
        name="mla_prep",
    )(p3, wq, wk, wv, gcq, gckv, gq, gk, vone, *tabs)


def _softmax_step(q, kc, vc, m, acc):
    s = _dot_nt(q, kc)
    m_new = jnp.maximum(m, jnp.max(s, axis=-1, keepdims=True))
    p = jnp.exp2((s - m_new).astype(BF16))
    acc = jnp.exp2(m - m_new) * acc + _dot(p, vc)
    return m_new, acc


def _mla_attn_kernel(qc_ref, kc_ref, v_ref, o_ref, *, n_chunks):
    tq = qc_ref.shape[1]

    def body(c, carry):
        off = pl.multiple_of(c * MLA_TK, MLA_TK)
        return tuple(
            _softmax_step(qc_ref[hh], kc_ref[hh, pl.ds(off, MLA_TK), :], v_ref[hh, pl.ds(off, MLA_TK), :], *carry[hh])
            for hh in range(2))

    m0 = jnp.full((tq, 1), NEG_BIG, F32)
    a0 = jnp.zeros((tq, HEAD_PAD), F32)
    carry = lax.fori_loop(0, n_chunks, body, ((m0, a0), (m0, a0)), unroll=True)
    outs = [acc[:, :MLA_V] / acc[:, MLA_V:MLA_V + 1] for _, acc in carry]
    o_ref[...] = jnp.concatenate(outs, axis=-1).astype(o_ref.dtype)


def _mla_ctx_kernel(qc_ref, kc_ref, v_ref, o_ref):
    outs = []
    for hh in range(2):
        s = _dot_nt(qc_ref[hh], kc_ref[hh])
        p = jnp.exp2(s - jnp.max(s, axis=-1, keepdims=True))
        acc = _dot(p.astype(BF16), v_ref[hh])
        outs.append(acc[:, :MLA_V] / acc[:, MLA_V:MLA_V + 1])
    o_ref[...] = jnp.concatenate(outs, axis=-1).astype(o_ref.dtype)


def _mla_attention(qc, kc, v, n_lat, need_ctx):
    B, H, Nt, _ = kc.shape
    n_ctx = Nt - n_lat
    o_lat = pl.pallas_call(
        functools.partial(_mla_attn_kernel, n_chunks=Nt // MLA_TK),
        out_shape=jax.ShapeDtypeStruct((B, n_lat, BRANCH_W), BF16),
        grid=(B, H // 2, n_lat // MLA_TQ),
        in_specs=[
            pl.BlockSpec((None, 2, MLA_TQ, 2 * HEAD_PAD), lambda b, g, i: (b, g, i, 0)),
            pl.BlockSpec((None, 2, Nt, 2 * HEAD_PAD), lambda b, g, i: (b, g, 0, 0)),
            pl.BlockSpec((None, 2, Nt, HEAD_PAD), lambda b, g, i: (b, g, 0, 0)),
        ],
        out_specs=pl.BlockSpec((None, MLA_TQ, 2 * MLA_V), lambda b, g, i: (b, i, g)),
        compiler_params=_cparams(("parallel", "parallel", "arbitrary")),
        name="mla_attention",
    )(qc, kc, v)
    if not need_ctx:
        return o_lat
    cblk = n_lat // n_ctx
    cspec = lambda w: pl.BlockSpec((None, 2, n_ctx, w), lambda b, g: (b, g, cblk, 0))
    o_ctx = pl.pallas_call(
        _mla_ctx_kernel,
        out_shape=jax.ShapeDtypeStruct((B, n_ctx, BRANCH_W), BF16),
        grid=(B, H // 2),
        in_specs=[cspec(2 * HEAD_PAD), cspec(2 * HEAD_PAD), cspec(HEAD_PAD)],
        out_specs=pl.BlockSpec((None, n_ctx, 2 * MLA_V), lambda b, g: (b, 0, g)),
        compiler_params=_cparams(("parallel", "parallel")),
        name="mla_attention_ctx",
    )(qc, kc, v)
    return jnp.concatenate([o_lat, o_ctx], axis=1)


NAT_ROWS_BLK = 8
NAT_TQ = NAT_ROWS_BLK * GRID_W
NAT_WIN_ROWS = 16
NAT_WIN = NAT_WIN_ROWS * GRID_W
NAT_PREP_BLK = 1024


def _pair_norm(x, g, half0):
    x2 = x * x
    s0 = jnp.sum(jnp.where(half0, x2, 0.0), axis=-1, keepdims=True)
    s1 = jnp.sum(jnp.where(half0, 0.0, x2), axis=-1, keepdims=True)
    ms = jnp.where(half0, s0, s1) * (1.0 / NAT_DH)
    return x * lax.rsqrt(ms + EPS) * g


def _nat_kernel(q_ref, kl_ref, vl_ref, kc_ref, vc_ref, bias_ref, gq_ref, gk_ref, o_ref, kn_ref, vb_ref,
                *, n_lat, rows_total):
    rb = pl.program_id(2)
    half0 = lax.broadcasted_iota(jnp.int32, (1, LANES), 1) < NAT_DH

    @pl.when(rb == 0)
    def _():
        def body(c, carry):
            off = pl.multiple_of(c * NAT_PREP_BLK, NAT_PREP_BLK)
            kn_ref[pl.ds(off, NAT_PREP_BLK), :] = _pair_norm(
                kl_ref[pl.ds(off, NAT_PREP_BLK), :].astype(F32), gk_ref[...], half0).astype(BF16)
            vb_ref[pl.ds(off, NAT_PREP_BLK), :LANES] = vl_ref[pl.ds(off, NAT_PREP_BLK), :].astype(BF16)
            vb_ref[pl.ds(off, NAT_PREP_BLK), LANES:] = jnp.ones((NAT_PREP_BLK, LANES), BF16)
            return carry

        lax.fori_loop(0, n_lat // NAT_PREP_BLK, body, 0)
        kn_ref[n_lat:, :] = _pair_norm(kc_ref[...].astype(F32), gk_ref[...], half0).astype(BF16)
        vb_ref[n_lat:, :LANES] = vc_ref[...].astype(BF16)
        vb_ref[n_lat:, LANES:] = jnp.ones((vc_ref.shape[0], LANES), BF16)

    q = _pair_norm(q_ref[...].astype(F32), gq_ref[...], half0)
    ws = jnp.clip(rb * NAT_ROWS_BLK - NAT_KH // 2, 0, rows_total - NAT_WIN_ROWS)
    off = pl.multiple_of(ws * GRID_W, GRID_W)
    kw = kn_ref[pl.ds(off, NAT_WIN), :]
    vw = vb_ref[pl.ds(off, NAT_WIN), :]
    kc = kn_ref[n_lat:, :]
    vc = vb_ref[n_lat:, :]
    qms = [jnp.where(half0 if hh == 0 else jnp.logical_not(half0), q, 0.0).astype(BF16) for hh in range(2)]
    sws = [_dot_nt(qm, kw) + bias_ref[hh].astype(F32) for hh, qm in enumerate(qms)]
    scs = [_dot_nt(qm, kc) for qm in qms]
    ms = [jnp.maximum(jnp.max(sw, axis=-1, keepdims=True), jnp.max(sc, axis=-1, keepdims=True))
          for sw, sc in zip(sws, scs)]
    accs = [_dot(jnp.exp2((sw - m).astype(BF16)), vw) + _dot(jnp.exp2((sc - m).astype(BF16)), vc)
            for sw, sc, m in zip(sws, scs, ms)]
    outs = [acc[:, :LANES] / acc[:, LANES:] for acc in accs]
    o_ref[...] = jnp.where(half0, outs[0], outs[1]).astype(o_ref.dtype)


def _nat_ctx_kernel(q_ref, kc_ref, vc_ref, gq_ref, gk_ref, o_ref):
    half0 = lax.broadcasted_iota(jnp.int32, (1, LANES), 1) < NAT_DH
    q = _pair_norm(q_ref[...].astype(F32), gq_ref[...], half0)
    kc = _pair_norm(kc_ref[...].astype(F32), gk_ref[...], half0).astype(BF16)
    vc = vc_ref[...].astype(BF16)
    outs = []
    for hh in range(2):
        sel = half0 if hh == 0 else jnp.logical_not(half0)
        s = _dot_nt(jnp.where(sel, q, 0.0).astype(BF16), kc)
        p = jnp.exp2(s - jnp.max(s, axis=-1, keepdims=True))
        outs.append(_dot(p.astype(BF16), vc) / jnp.sum(p, axis=-1, keepdims=True))
    o_ref[...] = jnp.where(half0, outs[0], outs[1]).astype(o_ref.dtype)


def _nat_bias_tables(rpb, rows_total):
    rbs = np.array([0, min(NAT_ROWS_BLK, rows_total - NAT_ROWS_BLK), rows_total - NAT_ROWS_BLK])
    ws = np.clip(rbs - NAT_KH // 2, 0, rows_total - NAT_WIN_ROWS)
    qrow = rbs[:, None] + np.arange(NAT_ROWS_BLK)[None, :]
    r0 = np.clip(qrow - NAT_KH // 2, 0, rows_total - NAT_KH)
    krow = ws[:, None] + np.arange(NAT_WIN_ROWS)[None, :]
    row_ok = (krow[:, None, :] >= r0[:, :, None]) & (krow[:, None, :] < r0[:, :, None] + NAT_KH)
    dr = np.clip(krow[:, None, :] - qrow[:, :, None] + NAT_KH - 1, 0, 2 * NAT_KH - 2)
    col = np.arange(GRID_W)
    c0 = np.clip(col - NAT_KW // 2, 0, GRID_W - NAT_KW)
    col_ok = (col[None, :] >= c0[:, None]) & (col[None, :] < c0[:, None] + NAT_KW)
    dc = np.clip(col[None, :] - col[:, None], -(NAT_KW - 1), NAT_KW - 1) + NAT_KW - 1
    oh_c = np.eye(2 * NAT_KW - 1, dtype=np.float32)[dc]
    by_col = jnp.einsum('hrc,qkc->hrqk', rpb.astype(F32), oh_c, precision=lax.Precision.HIGHEST)
    by_col = jnp.where(jnp.asarray(col_ok), by_col * LOG2E, NEG_BIG).astype(BF16)
    masked = jnp.full(by_col[:, 0].shape, NEG_BIG, BF16)
    variants = []
    for v in range(3):
        rows = [jnp.concatenate([by_col[:, dr[v, i, j]] if row_ok[v, i, j] else masked
                                 for j in range(NAT_WIN_ROWS)], axis=-1) for i in range(NAT_ROWS_BLK)]
        variants.append(jnp.concatenate(rows, axis=-2))
    return jnp.stack(variants).reshape(3, NAT_HEADS // 2, 2, NAT_TQ, NAT_WIN)


def _nat_attention(p3, rpb, g_q, g_k, n_lat, need_ctx):
    B, Nt, _ = p3.shape
    n_ctx = Nt - n_lat
    rows_total = n_lat // GRID_W
    n_rb = rows_total // NAT_ROWS_BLK
    bias = _nat_bias_tables(rpb, rows_total)
    gq = jnp.tile(g_q * (NAT_DH ** -0.5 * LOG2E), 2).reshape(1, LANES)
    gk = jnp.tile(g_k, 2).reshape(1, LANES)
    cq, ck, cv = ((COL_NAT + j * BRANCH_W) // LANES for j in range(3))
    cblk = n_lat // n_ctx

    def variant(rb):
        return jnp.where(rb == 0, 0, jnp.where(rb == n_rb - 1, 2, 1))

    vec = pl.BlockSpec((1, LANES), lambda b, g, rb: (0, 0))
    o_lat = pl.pallas_call(
        functools.partial(_nat_kernel, n_lat=n_lat, rows_total=rows_total),
        out_shape=jax.ShapeDtypeStruct((B, n_lat, BRANCH_W), BF16),
        grid=(B, NAT_HEADS // 2, n_rb),
        in_specs=[
            pl.BlockSpec((None, NAT_TQ, LANES), lambda b, g, rb: (b, rb, cq + g)),
            pl.BlockSpec((None, n_lat, LANES), lambda b, g, rb: (b, 0, ck + g)),
            pl.BlockSpec((None, n_lat, LANES), lambda b, g, rb: (b, 0, cv + g)),
            pl.BlockSpec((None, n_ctx, LANES), lambda b, g, rb: (b, cblk, ck + g)),
            pl.BlockSpec((None, n_ctx, LANES), lambda b, g, rb: (b, cblk, cv + g)),
            pl.BlockSpec((None, None, 2, NAT_TQ, NAT_WIN), lambda b, g, rb: (variant(rb), g, 0, 0, 0)),
            vec, vec,
        ],
        out_specs=pl.BlockSpec((None, NAT_TQ, LANES), lambda b, g, rb: (b, rb, g)),
        scratch_shapes=[pltpu.VMEM((Nt, LANES), BF16), pltpu.VMEM((Nt, 2 * LANES), BF16)],
        compiler_params=_cparams(("parallel", "parallel", "arbitrary")),
        name="nat_attention",
    )(p3, p3, p3, p3, p3, bias, gq, gk)
    if not need_ctx:
        return o_lat
    vec2 = pl.BlockSpec((1, LANES), lambda b, g: (0, 0))
    o_ctx = pl.pallas_call(
        _nat_ctx_kernel,
        out_shape=jax.ShapeDtypeStruct((B, n_ctx, BRANCH_W), BF16),
        grid=(B, NAT_HEADS // 2),
        in_specs=[
            pl.BlockSpec((None, n_ctx, LANES), lambda b, g: (b, cblk, cq + g)),
            pl.BlockSpec((None, n_ctx, LANES), lambda b, g: (b, cblk, ck + g)),
            pl.BlockSpec((None, n_ctx, LANES), lambda b, g: (b, cblk, cv + g)),
            vec2, vec2,
        ],
        out_specs=pl.BlockSpec((None, n_ctx, LANES), lambda b, g: (b, 0, g)),
        compiler_params=_cparams(("parallel", "parallel")),
        name="nat_attention_ctx",
    )(p3, p3, p3, gq, gk)
    return jnp.concatenate([o_lat, o_ctx], axis=1)


def _log_sigmoid(x):
    return jnp.minimum(x, 0.0) - jnp.log(1.0 + jnp.exp(-jnp.abs(x)))


def _run_block(operands, s_ref, o_ref, rev):
    qs, k, v, a, heads = operands()
    T, W = qs.shape
    n_ch, nsub, nsb = T // CHUNK, CHUNK // SUB, T // SUB
    row = lax.broadcasted_iota(jnp.int32, (T, T), 0)
    col = lax.broadcasted_iota(jnp.int32, (T, T), 1)
    same_chunk = (row // CHUNK) == (col // CHUNK)
    tri = jnp.where(same_chunk & ((row <= col) if rev else (row >= col)), 1.0, 0.0).astype(BF16)
    r64 = lax.broadcasted_iota(jnp.int32, (CHUNK, CHUNK), 0)
    c64 = lax.broadcasted_iota(jnp.int32, (CHUNK, CHUNK), 1)
    causal = (r64 <= c64) if rev else (r64 >= c64)
    hi = a.astype(BF16)
    mid = (a - hi.astype(F32)).astype(BF16)
    bc = _dot(tri, hi) + _dot(tri, mid)
    yield
    bex = bc - a
    first = [SUB * s + (SUB - 1 if rev else 0) for s in range(nsb)]
    last = [SUB * s + (0 if rev else SUB - 1) for s in range(nsb)]
    r_start = [bex[r:r + 1, :] for r in first]
    r_end = [bc[r:r + 1, :] for r in last]
    bend = [r_end[nsub * c + (0 if rev else nsub - 1)] for c in range(n_ch)]

    def spread(rows):
        return jnp.concatenate([jnp.broadcast_to(r, (SUB, W)) for r in rows], axis=0)

    rs_full = spread(r_start)
    qt = qs * jnp.exp(bc - rs_full)
    q_in = qt * spread([jnp.exp(r) for r in r_start])
    ke = k * jnp.exp(spread(r_end) - bc)
    k_out = ke * spread([jnp.exp(bend[s // nsub] - r_end[s]) for s in range(nsb)])
    kd = k * jnp.exp(jnp.minimum(rs_full - bc, EXP_CAP))
    zeros = jnp.zeros((SUB, W), F32)

    def keys_for(c, i):
        rows = []
        for j in range(nsub):
            sj, si = nsub * c + j, nsub * c + i
            blk = slice(SUB * sj, SUB * (sj + 1))
            if j == i:
                rows.append(kd[blk])
            elif (j > i) if rev else (j < i):
                rows.append(ke[blk] if abs(i - j) == 1 else ke[blk] * jnp.exp(r_start[si] - r_end[sj]))
            else:
                rows.append(zeros)
        return jnp.concatenate(rows, axis=0).astype(BF16)

    kts = [[keys_for(c, i) for i in range(nsub)] for c in range(n_ch)]
    vb = v.astype(BF16)
    yield
    intra, incr, q_state = [], [], []
    for hd, (ks, kmask, vs) in enumerate(heads):
        ksl = slice(ks, ks + LANES)
        vsl = slice(vs, vs + LANES)

        def msk(t, kmask=kmask):
            return t if kmask is None else jnp.where(kmask, t, 0.0)

        qt_h = msk(qt[:, ksl]).astype(BF16)
        ko_h = msk(k_out[:, ksl]).astype(BF16)
        q_state.append(msk(q_in[:, ksl]).astype(BF16))
        intra_h, incr_h = [], []
        for c in range(n_ch):
            cs = slice(CHUNK * c, CHUNK * (c + 1))
            blocks = [_dot_nt(qt_h[SUB * (nsub * c + i):SUB * (nsub * c + i + 1), :], kts[c][i][:, ksl])
                      for i in range(nsub)]
            sc = jnp.where(causal, jnp.concatenate(blocks, axis=0), 0.0).astype(BF16)
            intra_h.append(_dot(sc, vb[cs, vsl]))
            incr_h.append(_dot(v[cs, vsl].T.astype(BF16), ko_h[cs]))
        intra.append(intra_h)
        incr.append(incr_h)
        yield
    states = [s_ref[hd] for hd in range(len(heads))]
    for c in (range(n_ch - 1, -1, -1) if rev else range(n_ch)):
        cs = slice(CHUNK * c, CHUNK * (c + 1))
        dec = jnp.exp(bend[c])
        outs = []
        for hd, (ks, _, _) in enumerate(heads):
            outs.append(intra[hd][c] + _dot_nt(q_state[hd][cs], states[hd].astype(BF16)))
            states[hd] = states[hd] * dec[:, ks:ks + LANES] + incr[hd][c]
        o_ref[cs, :] = jnp.concatenate(outs, axis=-1).astype(o_ref.dtype)
        yield
    for hd, st in enumerate(states):
        s_ref[hd] = st


def _hgrn2_operands(q_ref, f_ref, v_ref, lb):
    q = q_ref[...].astype(F32)
    f = f_ref[...].astype(F32)
    loglb, log1mlb, oneml = lb[0:1, :], lb[1:2, :], lb[2:3, :]
    u = log1mlb + _log_sigmoid(f)
    a = jnp.maximum(u, loglb) + jnp.log(1.0 + jnp.exp(-jnp.abs(u - loglb)))
    k = oneml * jax.nn.sigmoid(-f)
    qs = q * jax.nn.sigmoid(q) * (HG_DK ** -0.5)
    heads = [(LANES * h, None, LANES * h) for h in range(HG_HEADS)]
    return qs, k, v_ref[...].astype(F32), a, heads


def _gla_operands(q_ref, k_ref, v_ref, r_ref, w2, b2):
    g = _dot(r_ref[...].astype(BF16), w2) + b2
    a = _log_sigmoid(g) * (1.0 / GLA_NORMALIZER)
    qs = q_ref[...].astype(F32) * (GLA_DK ** -0.5)
    half0 = lax.broadcasted_iota(jnp.int32, (1, LANES), 1) < GLA_DK
    heads = [(LANES * (h // 2), half0 if h % 2 == 0 else jnp.logical_not(half0), LANES * h)
             for h in range(GLA_HEADS)]
    return qs, k_ref[...].astype(F32), v_ref[...].astype(F32), a, heads


def _scans_kernel(hq_f, hf_f, hv_f, hq_b, hf_b, hv_b, lb_ref, gq_f, gk_f, gv_f, gr_f, gq_b, gk_b, gv_b, gr_b,
                  w2_ref, b2_ref, oh_f, oh_b, og_f, og_b, s_ref):
    @pl.when(pl.program_id(1) == 0)
    def _():
        s_ref[...] = jnp.zeros_like(s_ref)

    streams = [
        _run_block(functools.partial(_hgrn2_operands, hq_f, hf_f, hv_f, lb_ref[0]), s_ref.at[0], oh_f, False),
        _run_block(functools.partial(_hgrn2_operands, hq_b, hf_b, hv_b, lb_ref[1]), s_ref.at[1], oh_b, True),
        _run_block(functools.partial(_gla_operands, gq_f, gk_f, gv_f, gr_f, w2_ref[0], b2_ref[0]),
                   s_ref.at[2], og_f, False),
        _run_block(functools.partial(_gla_operands, gq_b, gk_b, gv_b, gr_b, w2_ref[1], b2_ref[1]),
                   s_ref.at[3], og_b, True),
    ]
    done = object()
    while streams:
        streams = [g for g in streams if next(g, done) is not done]


def _linear_scans(p3, lbvec, w2pad, b2, n_lat):
    B, Nt, _ = p3.shape
    nb = Nt // TOK_BLK
    n_lat_blk = n_lat // TOK_BLK
    fwd = lambda s: jnp.where(s == 0, n_lat_blk, s - 1)
    bwd = lambda s: jnp.where(s == 0, n_lat_blk, n_lat_blk - s)
    wqk = GLA_HEADS * GLA_DK
    spec = lambda blk, w, col: pl.BlockSpec((None, TOK_BLK, w), lambda b, s: (b, blk(s), col // w))
    hg = lambda blk, f_col: [spec(blk, BRANCH_W, COL_HG), spec(blk, BRANCH_W, f_col),
                             spec(blk, BRANCH_W, COL_HG + 3 * BRANCH_W)]
    gla = lambda blk: [spec(blk, wqk, COL_GLA), spec(blk, wqk, COL_GLA + wqk), spec(blk, BRANCH_W, COL_GLA + 2 * wqk),
                       spec(blk, LANES, COL_GLR)]
    const = lambda shape: pl.BlockSpec(shape, lambda b, s: (0,) * len(shape))
    out = lambda blk: pl.BlockSpec((None, TOK_BLK, BRANCH_W), lambda b, s: (b, blk(s), 0))
    shp = jax.ShapeDtypeStruct((B, Nt, BRANCH_W), BF16)
    return pl.pallas_call(
        _scans_kernel,
        out_shape=(shp, shp, shp, shp),
        grid=(B, nb),
        in_specs=(hg(fwd, COL_HG + BRANCH_W) + hg(bwd, COL_HG + 2 * BRANCH_W) + [const(lbvec.shape)]
                  + gla(fwd) + gla(bwd) + [const(w2pad.shape), const(b2.shape)]),
        out_specs=(out(fwd), out(bwd), out(fwd), out(bwd)),
        scratch_shapes=[pltpu.VMEM((4, HG_HEADS, LANES, LANES), F32)],
        compiler_params=_cparams(("parallel", "arbitrary")),
        name="linear_scans",
    )(*([p3] * 6), lbvec, *([p3] * 8), w2pad, b2)


def _group_norm(o, g):
    parts = []
    for hd in range(o.shape[1] // LANES):
        oh = o[:, LANES * hd:LANES * (hd + 1)]
        parts.append(oh * lax.rsqrt(jnp.mean(oh * oh, axis=-1, keepdims=True) + EPS))
    return jnp.concatenate(parts, axis=-1) * g


def _merge_kernel(x_ref, h_ref, za_ref, zb_ref, zc_ref, zd_ref, ya_ref, yb_ref, hf_ref, hb_ref, gf_ref, gb_ref,
                  wbr_ref, wmg_ref, bmg_ref, wout_ref, ghg_ref, ggl_ref, m_ref, o_ref):
    h = h_ref[...]
    ys = (
        ya_ref[...].astype(F32),
        yb_ref[...].astype(F32),
        _group_norm(hf_ref[...].astype(F32) + hb_ref[...].astype(F32), ghg_ref[...]),
        _group_norm(gf_ref[...].astype(F32) + gb_ref[...].astype(F32), ggl_ref[...]),
    )
    zs = (za_ref, zb_ref, zc_ref, zd_ref)
    acc = None
    for br in range(N_BRANCH):
        gate = jax.nn.sigmoid(_dot(h, wmg_ref[br]) + bmg_ref[br])
        z = zs[br][...].astype(F32)
        yz = (ys[br] * (z * jax.nn.sigmoid(z))).astype(BF16)
        part = gate * _dot(yz, wbr_ref[br])
        acc = part if acc is None else acc + part
    out = _dot(acc.astype(BF16), wout_ref[...])
    o_ref[...] = x_ref[...] + m_ref[2:3, :] * out


def _merge(xs, h, p3, ys, w_br, w_merge, b_merge, w_out, g_hg, g_gla, modsel, n_lat_blk, n_blk):
    B, _, D = xs.shape
    tok = lambda w: pl.BlockSpec((None, TOK_BLK, w), lambda b, i: (b, i, 0))
    zspec = lambda br: pl.BlockSpec((None, TOK_BLK, BRANCH_W), lambda b, i: (b, i, COL_Z // BRANCH_W + br))
    once = pl.Buffered(1)
    const = lambda shape: pl.BlockSpec(shape, lambda b, i: (0,) * len(shape), pipeline_mode=once)
    ghg = jnp.tile(g_hg, HG_HEADS).reshape(1, BRANCH_W)
    ggl = jnp.tile(g_gla, GLA_HEADS).reshape(1, BRANCH_W)
    bm = b_merge.reshape(N_BRANCH, 1, D)
    return pl.pallas_call(
        _merge_kernel,
        out_shape=jax.ShapeDtypeStruct((B, n_blk * TOK_BLK, D), F32),
        grid=(B, n_blk),
        in_specs=[tok(D), tok(D)] + [zspec(br) for br in range(N_BRANCH)] + [tok(BRANCH_W)] * 6 + [
            const(w_br.shape), const(w_merge.shape), const(bm.shape), const(w_out.shape),
            const(ghg.shape), const(ggl.shape),
            pl.BlockSpec((None, None, 3, D), lambda b, i: (b, jnp.where(i >= n_lat_blk, 1, 0), 0, 0)),
        ],
        out_specs=tok(D),
        compiler_params=_cparams(("parallel", "parallel"), vmem_mb=56),
        name="branch_merge",
    )(xs, h, p3, p3, p3, p3, *ys, w_br, w_merge, bm, w_out, ghg, ggl, modsel)


def kernel(x, c, ctx, c_ctx, ada_w, ada_b, norm_g, w_in, mla_w_uq, mla_w_ukv, mla_g_cq, mla_g_ckv, mla_g_q,
           mla_g_k, nat_rpb, nat_g_q, nat_g_k, hg_lb_logits, hg_g_o, gla_w2, gla_b2, gla_g_o, w_br, w_merge,
           b_merge, w_out):
    B, n_lat, D = x.shape
    n_ctx = ctx.shape[1]
    depth = ada_w.shape[0]
    Nt = n_lat + n_ctx
    n_lat_blk = n_lat // TOK_BLK
    assert n_ctx == TOK_BLK and n_lat % NAT_PREP_BLK == 0 and (B * Nt) % MM_ROW_BLK == 0
    assert Nt % MLA_TK == 0 and n_lat % MLA_TQ == 0

    cc = jnp.zeros((8, D), F32).at[:B].set(c).at[B].set(c_ctx)
    ada = _ada_vectors(cc, ada_w.astype(BF16), ada_b).reshape(depth, 8, 3, D)
    sm = jax.nn.softmax(hg_lb_logits.astype(F32), axis=0)
    lower = jnp.maximum(jnp.cumsum(sm, axis=0) - sm[0], 0.0)
    tabs = _rope_tables(n_lat, n_ctx)

    xs = jnp.concatenate([x, ctx], axis=1)
    for l in range(depth):
        need_ctx = l < depth - 1
        modsel = jnp.stack([ada[l, :B], jnp.broadcast_to(ada[l, B], (B, 3, D))], axis=1)
        h = _norm_modulate(xs, norm_g[l], modsel, n_lat_blk)
        p3 = _in_projection(h.reshape(B * Nt, D), _pack_w_in(w_in[l]), BF16).reshape(B, Nt, IN_W_PAD)

        mw = _mla_weights(mla_w_uq[l], mla_w_ukv[l], mla_g_cq[l], mla_g_ckv[l], mla_g_q[l], mla_g_k[l])
        y_a = _mla_attention(*_mla_prep(p3, mw, tabs), n_lat, need_ctx)
        y_b = _nat_attention(p3, nat_rpb[l], nat_g_q[l], nat_g_k[l], n_lat, need_ctx)

        lb = lower[l]
        lbvec = jnp.stack([jnp.log(lb), jnp.log1p(-lb), 1.0 - lb], axis=1)
        w2pad = jnp.zeros((2, LANES, GLA_HEADS * GLA_DK), F32)
        for d in range(2):
            w2pad = w2pad.at[d, GLA_LOWRANK * d:GLA_LOWRANK * (d + 1)].set(gla_w2[l, d])
        scans = _linear_scans(p3, lbvec, w2pad.astype(BF16), gla_b2[l].reshape(2, 1, -1), n_lat)

        n_blk = Nt // TOK_BLK if need_ctx else n_lat_blk
        xs = _merge(xs, h, p3, (y_a, y_b, *scans), w_br[l].astype(BF16), w_merge[l].astype(BF16), b_merge[l],
                    w_out[l].astype(BF16), hg_g_o[l], gla_g_o[l], modsel, n_lat_blk, n_blk)
    return xs[:, :n_lat]
```

```python
import functools

import jax
import jax.numpy as jnp
import numpy as np
from jax import lax
from jax.experimental import pallas as pl
from jax.experimental.pallas import tpu as pltpu

F32 = jnp.float32
BF16 = jnp.bfloat16

GRID_W = 64
N_BRANCH = 4
BRANCH_W = 512
MLA_HEADS = 8
MLA_NOPE = 64
MLA_ROPE = 32
MLA_V = 64
MLA_QK = MLA_NOPE + MLA_ROPE
Q_LORA = 256
KV_LORA = 128
ROPE_BASE = 10000.0
NAT_HEADS = 8
NAT_DH = 64
NAT_KH = 8
NAT_KW = 16
HG_HEADS = 4
HG_DK = 128
GLA_HEADS = 4
GLA_DK = 64
GLA_LOWRANK = 16
GLA_NORMALIZER = 16.0
EPS = 1e-6

W_MLA = Q_LORA + KV_LORA + MLA_ROPE
W_NAT = 3 * BRANCH_W
W_HG = 4 * BRANCH_W
W_GLA = 2 * GLA_HEADS * GLA_DK + BRANCH_W + 2 * GLA_LOWRANK

LANES = 128
HEAD_PAD = 128
TOK_BLK = 256

COL_HG = 0
COL_Z = COL_HG + W_HG
COL_NAT = COL_Z + N_BRANCH * BRANCH_W
COL_GLA = COL_NAT + W_NAT
COL_MLA = COL_GLA + 1024
COL_GLR = COL_MLA + 512
IN_W_PAD = COL_GLR + LANES
MM_COL_BLK = IN_W_PAD // 3
MM_ROW_BLK = 768

SUB = 16
CHUNK = 64
EXP_CAP = 80.0
NEG_BIG = -1e30
LOG2E = 1.4426950408889634


def _cparams(sem, vmem_mb=48):
    return pltpu.CompilerParams(dimension_semantics=sem, vmem_limit_bytes=vmem_mb * 1024 * 1024)


def _dot(a, b):
    return jnp.dot(a, b, preferred_element_type=F32)


def _dot_nt(a, b):
    return lax.dot_general(a, b, (((1,), (1,)), ((), ())), preferred_element_type=F32)


def _ada_kernel(c_ref, w_ref, b_ref, o_ref):
    c = c_ref[...]
    a = (c * jax.nn.sigmoid(c)).astype(BF16)
    o_ref[...] = _dot(a, w_ref[...]) + b_ref[...]


def _ada_vectors(cc, ada_w, ada_b):
    L, D, D3 = ada_w.shape
    tn = 1024
    return pl.pallas_call(
        _ada_kernel,
        out_shape=jax.ShapeDtypeStruct((L, cc.shape[0], D3), F32),
        grid=(L, D3 // tn),
        in_specs=[
            pl.BlockSpec((cc.shape[0], D), lambda l, j: (0, 0)),
            pl.BlockSpec((None, D, tn), lambda l, j: (l, 0, j)),
            pl.BlockSpec((None, 1, tn), lambda l, j: (l, 0, j)),
        ],
        out_specs=pl.BlockSpec((None, cc.shape[0], tn), lambda l, j: (l, 0, j)),
        compiler_params=_cparams(("parallel", "parallel")),
        name="ada_vectors",
    )(cc, ada_w, ada_b.reshape(L, 1, D3))


def _norm_proj_kernel(x_ref, g_ref, m_ref, w_ref, p_ref, h_ref, hs_ref, *, n_lat):
    @pl.when(pl.program_id(2) == 0)
    def _():
        x = x_ref[...]
        y = x * lax.rsqrt(jnp.mean(x * x, axis=-1, keepdims=True) + EPS) * g_ref[...]
        rows = x.shape[0]
        row = lax.broadcasted_iota(jnp.int32, (rows, 1), 0) + pl.program_id(1) * rows
        is_ctx = row >= n_lat
        scale = jnp.where(is_ctx, m_ref[1, 1:2, :], m_ref[0, 1:2, :])
        shift = jnp.where(is_ctx, m_ref[1, 0:1, :], m_ref[0, 0:1, :])
        h = (y * (1.0 + scale) + shift).astype(BF16)
        hs_ref[...] = h
        h_ref[...] = h

    p_ref[...] = _dot(hs_ref[...], w_ref[...]).astype(p_ref.dtype)


def _norm_project(xs, g, modsel, w, n_lat):
    B, Nt, D = xs.shape
    N = w.shape[1]
    return pl.pallas_call(
        functools.partial(_norm_proj_kernel, n_lat=n_lat),
        out_shape=(jax.ShapeDtypeStruct((B, Nt, N), BF16), jax.ShapeDtypeStruct((B, Nt, D), BF16)),
        grid=(B, Nt // MM_ROW_BLK, N // MM_COL_BLK),
        in_specs=[
            pl.BlockSpec((None, MM_ROW_BLK, D), lambda b, i, j: (b, i, 0)),
            pl.BlockSpec((1, D), lambda b, i, j: (0, 0)),
            pl.BlockSpec((None, 2, 3, D), lambda b, i, j: (b, 0, 0, 0)),
            pl.BlockSpec((D, MM_COL_BLK), lambda b, i, j: (0, j)),
        ],
        out_specs=(
            pl.BlockSpec((None, MM_ROW_BLK, MM_COL_BLK), lambda b, i, j: (b, i, j)),
            pl.BlockSpec((None, MM_ROW_BLK, D), lambda b, i, j: (b, i, 0)),
        ),
        scratch_shapes=[pltpu.VMEM((MM_ROW_BLK, D), BF16)],
        compiler_params=_cparams(("parallel", "parallel", "arbitrary")),
        name="norm_project",
    )(xs, g.reshape(1, D), modsel, w)


def _pack_w_in(w_in_l):
    o_nat = W_MLA
    o_hg = o_nat + W_NAT
    o_gla = o_hg + W_HG
    o_z = o_gla + W_GLA
    D = w_in_l.shape[0]
    zeros = lambda n: jnp.zeros((D, n), w_in_l.dtype)
    cols = [
        w_in_l[:, o_hg:o_hg + W_HG],
        w_in_l[:, o_z:o_z + N_BRANCH * BRANCH_W],
        w_in_l[:, o_nat:o_nat + W_NAT],
        w_in_l[:, o_gla:o_gla + 1024],
        w_in_l[:, :W_MLA], zeros(512 - W_MLA),
        w_in_l[:, o_gla + 1024:o_gla + W_GLA], zeros(LANES - 2 * GLA_LOWRANK),
    ]
    return jnp.concatenate(cols, axis=1).astype(BF16)


MLA_TQ = 1024
MLA_TK = 768


def _mla_prep_kernel(p_ref, wq_ref, wk_ref, wv_ref, gcq_ref, gckv_ref, gq_ref, gk_ref, vone_ref,
                     c_ref, s_ref, qc_ref, kc_ref, v_ref, *, n_lat_blk):
    ctx_w = jnp.where(pl.program_id(1) == n_lat_blk, 1.0, 0.0)
    hw = MLA_HEADS * HEAD_PAD
    p = p_ref[...].astype(F32)
    cq = p[:, :Q_LORA]
    r = lax.rsqrt(jnp.mean(cq * cq, axis=-1, keepdims=True) + EPS)
    cqn = (cq * r * gcq_ref[...]).astype(BF16)
    qraw = _dot(cqn, wq_ref[...])
    kv = p[:, Q_LORA:]
    ckv = kv[:, :KV_LORA]
    r2 = lax.rsqrt(jnp.mean(ckv * ckv, axis=-1, keepdims=True) + EPS)
    lane = lax.broadcasted_iota(jnp.int32, kv.shape, 1)
    lhs = (kv * jnp.where(lane < KV_LORA, r2, 1.0) * gckv_ref[...]).astype(BF16)
    kraw = _dot(lhs, wk_ref[...])
    vraw = _dot(lhs, wv_ref[...]) + vone_ref[...]
    cs = c_ref[...]
    sn = s_ref[...]
    inv = 1.0 / MLA_QK
    for hd in range(MLA_HEADS):
        sl = slice(HEAD_PAD * hd, HEAD_PAD * (hd + 1))
        sw = slice(hw + HEAD_PAD * hd, hw + HEAD_PAD * (hd + 1))
        qh = qraw[:, sl]
        rq = lax.rsqrt(jnp.sum(qh * qh, axis=-1, keepdims=True) * inv + EPS)
        qn = qh * rq * gq_ref[0:1, :]
        q_rot = qn * cs + qraw[:, sw] * rq * gq_ref[1:2, :] * sn
        qc_ref[hd] = jnp.concatenate([q_rot, qn], axis=1).astype(qc_ref.dtype)
        kh = kraw[:, sl]
        rk = lax.rsqrt(jnp.sum(kh * kh, axis=-1, keepdims=True) * inv + EPS)
        kn = kh * rk * gk_ref[0:1, :]
        k_rot = kn * cs + kraw[:, sw] * rk * gk_ref[1:2, :] * sn
        kc_ref[hd] = jnp.concatenate([k_rot * (1.0 - ctx_w), kn * ctx_w], axis=1).astype(kc_ref.dtype)
        v_ref[hd] = vraw[:, sl].astype(v_ref.dtype)


def _mla_weights(w_uq, w_ukv, g_cq, g_ckv, g_q, g_k):
    H = MLA_HEADS
    half = MLA_ROPE // 2
    lo, mid, hi = MLA_NOPE, MLA_NOPE + half, MLA_QK

    def swapped(t):
        z = jnp.zeros_like(t)
        return z.at[..., lo:mid].set(-t[..., mid:hi]).at[..., mid:hi].set(t[..., lo:mid])

    def both(t):
        return jnp.concatenate([t, swapped(t)], axis=1).reshape(t.shape[0], 2 * H * HEAD_PAD).astype(BF16)

    wq = jnp.pad(w_uq.reshape(Q_LORA, H, MLA_QK), ((0, 0), (0, 0), (0, HEAD_PAD - MLA_QK)))
    wkv = w_ukv.reshape(KV_LORA, H, MLA_NOPE + MLA_V)
    wk = jnp.pad(wkv[:, :, :MLA_NOPE], ((0, 0), (0, 0), (0, HEAD_PAD - MLA_NOPE)))
    place = jnp.zeros((MLA_ROPE, H, HEAD_PAD), F32)
    place = place.at[jnp.arange(MLA_ROPE), :, MLA_NOPE + jnp.arange(MLA_ROPE)].set(1.0)
    wk = jnp.concatenate([wk, place, jnp.zeros((256 - KV_LORA - MLA_ROPE, H, HEAD_PAD), F32)], axis=0)
    wv = jnp.pad(wkv[:, :, MLA_NOPE:], ((0, 256 - KV_LORA), (0, 0), (0, HEAD_PAD - MLA_V)))
    wv = wv.reshape(256, H * HEAD_PAD).astype(BF16)
    vone = jnp.zeros((H, HEAD_PAD), F32).at[:, MLA_V].set(1.0).reshape(1, H * HEAD_PAD)
    gckv = jnp.concatenate([g_ckv, jnp.ones((256 - KV_LORA,), F32)]).reshape(1, 256)
    gq = jnp.pad(g_q * (MLA_QK ** -0.5 * LOG2E), (0, HEAD_PAD - MLA_QK))
    gk = jnp.pad(g_k, (0, HEAD_PAD - MLA_QK))
    unsign = jnp.where(jnp.arange(HEAD_PAD) < mid, -1.0, 1.0)
    gq2 = jnp.stack([gq, swapped(gq) * unsign], axis=0)
    gk2 = jnp.stack([gk, swapped(gk) * unsign], axis=0)
    return both(wq), both(wk), wv, g_cq.reshape(1, Q_LORA), gckv, gq2, gk2, vone


def _rope_tables(n_lat, n_ctx):
    quarter = MLA_ROPE // 4
    inv_freq = ROPE_BASE ** (-jnp.arange(quarter, dtype=F32) / quarter)
    t = jnp.arange(n_lat, dtype=jnp.int32)
    row = (t // GRID_W).astype(F32)
    col = (t % GRID_W).astype(F32)
    ang = jnp.concatenate([row[:, None] * inv_freq, col[:, None] * inv_freq], axis=-1)
    cos, sin = jnp.cos(ang), jnp.sin(ang)
    ones = jnp.ones((n_lat, MLA_NOPE), F32)
    tail = jnp.ones((n_lat, HEAD_PAD - MLA_QK), F32)
    zl = jnp.zeros((n_lat, MLA_NOPE), F32)
    zt = jnp.zeros((n_lat, HEAD_PAD - MLA_QK), F32)
    c = jnp.concatenate([ones, cos, cos, tail], axis=1)
    s = jnp.concatenate([zl, sin, sin, zt], axis=1)
    c = jnp.concatenate([c, jnp.ones((n_ctx, HEAD_PAD), F32)], axis=0)
    s = jnp.concatenate([s, jnp.zeros((n_ctx, HEAD_PAD), F32)], axis=0)
    return c, s


def _mla_prep(p3, mw, tabs):
    B, Nt, _ = p3.shape
    H = MLA_HEADS
    wq, wk, wv, gcq, gckv, gq, gk, vone = mw
    n_lat_blk = Nt // TOK_BLK - 1
    const = lambda shape: pl.BlockSpec(shape, lambda b, i: (0,) * len(shape))
    tab = pl.BlockSpec((TOK_BLK, HEAD_PAD), lambda b, i: (i, 0))
    out = lambda w: pl.BlockSpec((None, H, TOK_BLK, w), lambda b, i: (b, 0, i, 0))
    shp = lambda w: jax.ShapeDtypeStruct((B, H, Nt, w), BF16)
    return pl.pallas_call(
        functools.partial(_mla_prep_kernel, n_lat_blk=n_lat_blk),
        out_shape=(shp(2 * HEAD_PAD), shp(2 * HEAD_PAD), shp(HEAD_PAD)),
        grid=(B, Nt // TOK_BLK),
        in_specs=[
            pl.BlockSpec((None, TOK_BLK, 512), lambda b, i: (b, i, COL_MLA // 512)),
            const(wq.shape), const(wk.shape), const(wv.shape), const(gcq.shape), const(gckv.shape),
            const(gq.shape), const(gk.shape), const(vone.shape), tab, tab,
        ],
        out_specs=(out(2 * HEAD_PAD), out(2 * HEAD_PAD), out(HEAD_PAD)),
        compiler_params=_cparams(("parallel", "parallel")),
        name="mla_prep",
    )(p3, wq, wk, wv, gcq, gckv, gq, gk, vone, *tabs)


def _softmax_step(q, kc, vc, m, acc):
    s = _dot_nt(q, kc)
    m_new = jnp.maximum(m, jnp.max(s, axis=-1, keepdims=True))
    p = jnp.exp2((s - m_new).astype(BF16))
    acc = jnp.exp2(m - m_new) * acc + _dot(p, vc)
    return m_new, acc


def _mla_attn_kernel(qc_ref, kc_ref, v_ref, o_ref, *, n_chunks):
    tq = qc_ref.shape[1]

    def body(c, carry):
        off = pl.multiple_of(c * MLA_TK, MLA_TK)
        return tuple(
            _softmax_step(qc_ref[hh], kc_ref[hh, pl.ds(off, MLA_TK), :], v_ref[hh, pl.ds(off, MLA_TK), :], *carry[hh])
            for hh in range(2))

    m0 = jnp.full((tq, 1), NEG_BIG, F32)
    a0 = jnp.zeros((tq, HEAD_PAD), F32)
    carry = lax.fori_loop(0, n_chunks, body, ((m0, a0), (m0, a0)), unroll=True)
    outs = [acc[:, :MLA_V] / acc[:, MLA_V:MLA_V + 1] for _, acc in carry]
    o_ref[...] = jnp.concatenate(outs, axis=-1).astype(o_ref.dtype)


def _mla_ctx_kernel(qc_ref, kc_ref, v_ref, o_ref):
    outs = []
    for hh in range(2):
        s = _dot_nt(qc_ref[hh], kc_ref[hh])
        p = jnp.exp2(s - jnp.max(s, axis=-1, keepdims=True))
        acc = _dot(p.astype(BF16), v_ref[hh])
        outs.append(acc[:, :MLA_V] / acc[:, MLA_V:MLA_V + 1])
    o_ref[...] = jnp.concatenate(outs, axis=-1).astype(o_ref.dtype)


def _mla_attention(qc, kc, v, n_lat, need_ctx):
    B, H, Nt, _ = kc.shape
    n_ctx = Nt - n_lat
    o_lat = pl.pallas_call(
        functools.partial(_mla_attn_kernel, n_chunks=Nt // MLA_TK),
        out_shape=jax.ShapeDtypeStruct((B, n_lat, BRANCH_W), BF16),
        grid=(B, H // 2, n_lat // MLA_TQ),
        in_specs=[
            pl.BlockSpec((None, 2, MLA_TQ, 2 * HEAD_PAD), lambda b, g, i: (b, g, i, 0)),
            pl.BlockSpec((None, 2, Nt, 2 * HEAD_PAD), lambda b, g, i: (b, g, 0, 0)),
            pl.BlockSpec((None, 2, Nt, HEAD_PAD), lambda b, g, i: (b, g, 0, 0)),
        ],
        out_specs=pl.BlockSpec((None, MLA_TQ, 2 * MLA_V), lambda b, g, i: (b, i, g)),
        compiler_params=_cparams(("parallel", "parallel", "arbitrary")),
        name="mla_attention",
    )(qc, kc, v)
    if not need_ctx:
        return o_lat
    cblk = n_lat // n_ctx
    cspec = lambda w: pl.BlockSpec((None, 2, n_ctx, w), lambda b, g: (b, g, cblk, 0))
    o_ctx = pl.pallas_call(
        _mla_ctx_kernel,
        out_shape=jax.ShapeDtypeStruct((B, n_ctx, BRANCH_W), BF16),
        grid=(B, H // 2),
        in_specs=[cspec(2 * HEAD_PAD), cspec(2 * HEAD_PAD), cspec(HEAD_PAD)],
        out_specs=pl.BlockSpec((None, n_ctx, 2 * MLA_V), lambda b, g: (b, 0, g)),
        compiler_params=_cparams(("parallel", "parallel")),
        name="mla_attention_ctx",
    )(qc, kc, v)
    return jnp.concatenate([o_lat, o_ctx], axis=1)


NAT_ROWS_BLK = 8
NAT_TQ = NAT_ROWS_BLK * GRID_W
NAT_WIN_ROWS = 16
NAT_WIN = NAT_WIN_ROWS * GRID_W
NAT_PREP_BLK = 1024


def _pair_norm(x, g, half0):
    x2 = x * x
    s0 = jnp.sum(jnp.where(half0, x2, 0.0), axis=-1, keepdims=True)
    s1 = jnp.sum(jnp.where(half0, 0.0, x2), axis=-1, keepdims=True)
    ms = jnp.where(half0, s0, s1) * (1.0 / NAT_DH)
    return x * lax.rsqrt(ms + EPS) * g


def _nat_kernel(q_ref, kl_ref, vl_ref, kc_ref, vc_ref, bias_ref, gq_ref, gk_ref, o_ref, kn_ref, vb_ref,
                *, n_lat, rows_total):
    rb = pl.program_id(2)
    half0 = lax.broadcasted_iota(jnp.int32, (1, LANES), 1) < NAT_DH

    @pl.when(rb == 0)
    def _():
        def body(c, carry):
            off = pl.multiple_of(c * NAT_PREP_BLK, NAT_PREP_BLK)
            kn_ref[pl.ds(off, NAT_PREP_BLK), :] = _pair_norm(
                kl_ref[pl.ds(off, NAT_PREP_BLK), :].astype(F32), gk_ref[...], half0).astype(BF16)
            vb_ref[pl.ds(off, NAT_PREP_BLK), :LANES] = vl_ref[pl.ds(off, NAT_PREP_BLK), :].astype(BF16)
            vb_ref[pl.ds(off, NAT_PREP_BLK), LANES:] = jnp.ones((NAT_PREP_BLK, LANES), BF16)
            return carry

        lax.fori_loop(0, n_lat // NAT_PREP_BLK, body, 0)
        kn_ref[n_lat:, :] = _pair_norm(kc_ref[...].astype(F32), gk_ref[...], half0).astype(BF16)
        vb_ref[n_lat:, :LANES] = vc_ref[...].astype(BF16)
        vb_ref[n_lat:, LANES:] = jnp.ones((vc_ref.shape[0], LANES), BF16)

    q = _pair_norm(q_ref[...].astype(F32), gq_ref[...], half0)
    ws = jnp.clip(rb * NAT_ROWS_BLK - NAT_KH // 2, 0, rows_total - NAT_WIN_ROWS)
    off = pl.multiple_of(ws * GRID_W, GRID_W)
    kw = kn_ref[pl.ds(off, NAT_WIN), :]
    vw = vb_ref[pl.ds(off, NAT_WIN), :]
    kc = kn_ref[n_lat:, :]
    vc = vb_ref[n_lat:, :]
    qms = [jnp.where(half0 if hh == 0 else jnp.logical_not(half0), q, 0.0).astype(BF16) for hh in range(2)]
    sws = [_dot_nt(qm, kw) + bias_ref[hh].astype(F32) for hh, qm in enumerate(qms)]
    scs = [_dot_nt(qm, kc) for qm in qms]
    ms = [jnp.maximum(jnp.max(sw, axis=-1, keepdims=True), jnp.max(sc, axis=-1, keepdims=True))
          for sw, sc in zip(sws, scs)]
    accs = [_dot(jnp.exp2((sw - m).astype(BF16)), vw) + _dot(jnp.exp2((sc - m).astype(BF16)), vc)
            for sw, sc, m in zip(sws, scs, ms)]
    outs = [acc[:, :LANES] / acc[:, LANES:] for acc in accs]
    o_ref[...] = jnp.where(half0, outs[0], outs[1]).astype(o_ref.dtype)


def _nat_ctx_kernel(q_ref, kc_ref, vc_ref, gq_ref, gk_ref, o_ref):
    half0 = lax.broadcasted_iota(jnp.int32, (1, LANES), 1) < NAT_DH
    q = _pair_norm(q_ref[...].astype(F32), gq_ref[...], half0)
    kc = _pair_norm(kc_ref[...].astype(F32), gk_ref[...], half0).astype(BF16)
    vc = vc_ref[...].astype(BF16)
    outs = []
    for hh in range(2):
        sel = half0 if hh == 0 else jnp.logical_not(half0)
        s = _dot_nt(jnp.where(sel, q, 0.0).astype(BF16), kc)
        p = jnp.exp2(s - jnp.max(s, axis=-1, keepdims=True))
        outs.append(_dot(p.astype(BF16), vc) / jnp.sum(p, axis=-1, keepdims=True))
    o_ref[...] = jnp.where(half0, outs[0], outs[1]).astype(o_ref.dtype)


def _nat_bias_tables(rpb, rows_total):
    rbs = np.array([0, min(NAT_ROWS_BLK, rows_total - NAT_ROWS_BLK), rows_total - NAT_ROWS_BLK])
    ws = np.clip(rbs - NAT_KH // 2, 0, rows_total - NAT_WIN_ROWS)
    qrow = rbs[:, None] + np.arange(NAT_ROWS_BLK)[None, :]
    r0 = np.clip(qrow - NAT_KH // 2, 0, rows_total - NAT_KH)
    krow = ws[:, None] + np.arange(NAT_WIN_ROWS)[None, :]
    row_ok = (krow[:, None, :] >= r0[:, :, None]) & (krow[:, None, :] < r0[:, :, None] + NAT_KH)
    dr = np.clip(krow[:, None, :] - qrow[:, :, None] + NAT_KH - 1, 0, 2 * NAT_KH - 2)
    col = np.arange(GRID_W)
    c0 = np.clip(col - NAT_KW // 2, 0, GRID_W - NAT_KW)
    col_ok = (col[None, :] >= c0[:, None]) & (col[None, :] < c0[:, None] + NAT_KW)
    dc = np.clip(col[None, :] - col[:, None], -(NAT_KW - 1), NAT_KW - 1) + NAT_KW - 1
    oh_c = np.eye(2 * NAT_KW - 1, dtype=np.float32)[dc]
    by_col = jnp.einsum('hrc,qkc->hrqk', rpb.astype(F32), oh_c, precision=lax.Precision.HIGHEST)
    by_col = jnp.where(jnp.asarray(col_ok), by_col * LOG2E, NEG_BIG).astype(BF16)
    masked = jnp.full(by_col[:, 0].shape, NEG_BIG, BF16)
    variants = []
    for v in range(3):
        rows = [jnp.concatenate([by_col[:, dr[v, i, j]] if row_ok[v, i, j] else masked
                                 for j in range(NAT_WIN_ROWS)], axis=-1) for i in range(NAT_ROWS_BLK)]
        variants.append(jnp.concatenate(rows, axis=-2))
    return jnp.stack(variants).reshape(3, NAT_HEADS // 2, 2, NAT_TQ, NAT_WIN)


def _nat_attention(p3, rpb, g_q, g_k, n_lat, need_ctx):
    B, Nt, _ = p3.shape
    n_ctx = Nt - n_lat
    rows_total = n_lat // GRID_W
    n_rb = rows_total // NAT_ROWS_BLK
    bias = _nat_bias_tables(rpb, rows_total)
    gq = jnp.tile(g_q * (NAT_DH ** -0.5 * LOG2E), 2).reshape(1, LANES)
    gk = jnp.tile(g_k, 2).reshape(1, LANES)
    cq, ck, cv = ((COL_NAT + j * BRANCH_W) // LANES for j in range(3))
    cblk = n_lat // n_ctx

    def variant(rb):
        return jnp.where(rb == 0, 0, jnp.where(rb == n_rb - 1, 2, 1))

    vec = pl.BlockSpec((1, LANES), lambda b, g, rb: (0, 0))
    o_lat = pl.pallas_call(
        functools.partial(_nat_kernel, n_lat=n_lat, rows_total=rows_total),
        out_shape=jax.ShapeDtypeStruct((B, n_lat, BRANCH_W), BF16),
        grid=(B, NAT_HEADS // 2, n_rb),
        in_specs=[
            pl.BlockSpec((None, NAT_TQ, LANES), lambda b, g, rb: (b, rb, cq + g)),
            pl.BlockSpec((None, n_lat, LANES), lambda b, g, rb: (b, 0, ck + g)),
            pl.BlockSpec((None, n_lat, LANES), lambda b, g, rb: (b, 0, cv + g)),
            pl.BlockSpec((None, n_ctx, LANES), lambda b, g, rb: (b, cblk, ck + g)),
            pl.BlockSpec((None, n_ctx, LANES), lambda b, g, rb: (b, cblk, cv + g)),
            pl.BlockSpec((None, None, 2, NAT_TQ, NAT_WIN), lambda b, g, rb: (variant(rb), g, 0, 0, 0)),
            vec, vec,
        ],
        out_specs=pl.BlockSpec((None, NAT_TQ, LANES), lambda b, g, rb: (b, rb, g)),
        scratch_shapes=[pltpu.VMEM((Nt, LANES), BF16), pltpu.VMEM((Nt, 2 * LANES), BF16)],
        compiler_params=_cparams(("parallel", "parallel", "arbitrary")),
        name="nat_attention",
    )(p3, p3, p3, p3, p3, bias, gq, gk)
    if not need_ctx:
        return o_lat
    vec2 = pl.BlockSpec((1, LANES), lambda b, g: (0, 0))
    o_ctx = pl.pallas_call(
        _nat_ctx_kernel,
        out_shape=jax.ShapeDtypeStruct((B, n_ctx, BRANCH_W), BF16),
        grid=(B, NAT_HEADS // 2),
        in_specs=[
            pl.BlockSpec((None, n_ctx, LANES), lambda b, g: (b, cblk, cq + g)),
            pl.BlockSpec((None, n_ctx, LANES), lambda b, g: (b, cblk, ck + g)),
            pl.BlockSpec((None, n_ctx, LANES), lambda b, g: (b, cblk, cv + g)),
            vec2, vec2,
        ],
        out_specs=pl.BlockSpec((None, n_ctx, LANES), lambda b, g: (b, 0, g)),
        compiler_params=_cparams(("parallel", "parallel")),
        name="nat_attention_ctx",
    )(p3, p3, p3, gq, gk)
    return jnp.concatenate([o_lat, o_ctx], axis=1)


def _log_sigmoid(x):
    return jnp.minimum(x, 0.0) - jnp.log(1.0 + jnp.exp(-jnp.abs(x)))


def _run_block(operands, s_ref, o_ref, rev):
    qs, k, v, a, heads = operands()
    T, W = qs.shape
    n_ch, nsub, nsb = T // CHUNK, CHUNK // SUB, T // SUB
    row = lax.broadcasted_iota(jnp.int32, (T, T), 0)
    col = lax.broadcasted_iota(jnp.int32, (T, T), 1)
    same_chunk = (row // CHUNK) == (col // CHUNK)
    tri = jnp.where(same_chunk & ((row <= col) if rev else (row >= col)), 1.0, 0.0).astype(BF16)
    r64 = lax.broadcasted_iota(jnp.int32, (CHUNK, CHUNK), 0)
    c64 = lax.broadcasted_iota(jnp.int32, (CHUNK, CHUNK), 1)
    causal = (r64 <= c64) if rev else (r64 >= c64)
    hi = a.astype(BF16)
    mid = (a - hi.astype(F32)).astype(BF16)
    bc = _dot(tri, hi) + _dot(tri, mid)
    bex = bc - a
    first = [SUB * s + (SUB - 1 if rev else 0) for s in range(nsb)]
    last = [SUB * s + (0 if rev else SUB - 1) for s in range(nsb)]
    r_start = [bex[r:r + 1, :] for r in first]
    r_end = [bc[r:r + 1, :] for r in last]
    bend = [r_end[nsub * c + (0 if rev else nsub - 1)] for c in range(n_ch)]

    def spread(rows):
        return jnp.concatenate([jnp.broadcast_to(r, (SUB, W)) for r in rows], axis=0)

    rs_full = spread(r_start)
    qt = qs * jnp.exp(bc - rs_full)
    q_in = qt * spread([jnp.exp(r) for r in r_start])
    ke = k * jnp.exp(spread(r_end) - bc)
    k_out = ke * spread([jnp.exp(bend[s // nsub] - r_end[s]) for s in range(nsb)])
    kd = k * jnp.exp(jnp.minimum(rs_full - bc, EXP_CAP))
    zeros = jnp.zeros((SUB, W), F32)

    def keys_for(c, i):
        rows = []
        for j in range(nsub):
            sj, si = nsub * c + j, nsub * c + i
            blk = slice(SUB * sj, SUB * (sj + 1))
            if j == i:
                rows.append(kd[blk])
            elif (j > i) if rev else (j < i):
                rows.append(ke[blk] if abs(i - j) == 1 else ke[blk] * jnp.exp(r_start[si] - r_end[sj]))
            else:
                rows.append(zeros)
        return jnp.concatenate(rows, axis=0).astype(BF16)

    kts = [[keys_for(c, i) for i in range(nsub)] for c in range(n_ch)]
    vb = v.astype(BF16)
    yield
    intra, incr, q_state = [], [], []
    for hd, (ks, kmask, vs) in enumerate(heads):
        ksl = slice(ks, ks + LANES)
        vsl = slice(vs, vs + LANES)

        def msk(t, kmask=kmask):
            return t if kmask is None else jnp.where(kmask, t, 0.0)

        qt_h = msk(qt[:, ksl]).astype(BF16)
        ko_h = msk(k_out[:, ksl]).astype(BF16)
        q_state.append(msk(q_in[:, ksl]).astype(BF16))
        intra_h, incr_h = [], []
        for c in range(n_ch):
            cs = slice(CHUNK * c, CHUNK * (c + 1))
            blocks = [_dot_nt(qt_h[SUB * (nsub * c + i):SUB * (nsub * c + i + 1), :], kts[c][i][:, ksl])
                      for i in range(nsub)]
            sc = jnp.where(causal, jnp.concatenate(blocks, axis=0), 0.0).astype(BF16)
            intra_h.append(_dot(sc, vb[cs, vsl]))
            incr_h.append(_dot(v[cs, vsl].T.astype(BF16), ko_h[cs]))
        intra.append(intra_h)
        incr.append(incr_h)
    yield
    states = [s_ref[hd] for hd in range(len(heads))]
    for c in (range(n_ch - 1, -1, -1) if rev else range(n_ch)):
        cs = slice(CHUNK * c, CHUNK * (c + 1))
        dec = jnp.exp(bend[c])
        outs = []
        for hd, (ks, _, _) in enumerate(heads):
            outs.append(intra[hd][c] + _dot_nt(q_state[hd][cs], states[hd].astype(BF16)))
            states[hd] = states[hd] * dec[:, ks:ks + LANES] + incr[hd][c]
        o_ref[cs, :] = jnp.concatenate(outs, axis=-1).astype(o_ref.dtype)
        yield
    for hd, st in enumerate(states):
        s_ref[hd] = st


def _hgrn2_operands(q_ref, f_ref, v_ref, lb):
    q = q_ref[...].astype(F32)
    f = f_ref[...].astype(F32)
    loglb, log1mlb, oneml = lb[0:1, :], lb[1:2, :], lb[2:3, :]
    u = log1mlb + _log_sigmoid(f)
    a = jnp.maximum(u, loglb) + jnp.log(1.0 + jnp.exp(-jnp.abs(u - loglb)))
    k = oneml * jax.nn.sigmoid(-f)
    qs = q * jax.nn.sigmoid(q) * (HG_DK ** -0.5)
    heads = [(LANES * h, None, LANES * h) for h in range(HG_HEADS)]
    return qs, k, v_ref[...].astype(F32), a, heads


def _gla_operands(q_ref, k_ref, v_ref, r_ref, w2, b2):
    g = _dot(r_ref[...].astype(BF16), w2) + b2
    a = _log_sigmoid(g) * (1.0 / GLA_NORMALIZER)
    qs = q_ref[...].astype(F32) * (GLA_DK ** -0.5)
    half0 = lax.broadcasted_iota(jnp.int32, (1, LANES), 1) < GLA_DK
    heads = [(LANES * (h // 2), half0 if h % 2 == 0 else jnp.logical_not(half0), LANES * h)
             for h in range(GLA_HEADS)]
    return qs, k_ref[...].astype(F32), v_ref[...].astype(F32), a, heads


def _scans_kernel(hq_f, hf_f, hv_f, hq_b, hf_b, hv_b, lb_ref, gq_f, gk_f, gv_f, gr_f, gq_b, gk_b, gv_b, gr_b,
                  w2_ref, b2_ref, oh_f, oh_b, og_f, og_b, s_ref):
    @pl.when(pl.program_id(1) == 0)
    def _():
        s_ref[...] = jnp.zeros_like(s_ref)

    streams = [
        _run_block(functools.partial(_hgrn2_operands, hq_f, hf_f, hv_f, lb_ref[0]), s_ref.at[0], oh_f, False),
        _run_block(functools.partial(_hgrn2_operands, hq_b, hf_b, hv_b, lb_ref[1]), s_ref.at[1], oh_b, True),
        _run_block(functools.partial(_gla_operands, gq_f, gk_f, gv_f, gr_f, w2_ref[0], b2_ref[0]),
                   s_ref.at[2], og_f, False),
        _run_block(functools.partial(_gla_operands, gq_b, gk_b, gv_b, gr_b, w2_ref[1], b2_ref[1]),
                   s_ref.at[3], og_b, True),
    ]
    done = object()
    while streams:
        streams = [g for g in streams if next(g, done) is not done]


def _linear_scans(p3, lbvec, w2pad, b2, n_lat):
    B, Nt, _ = p3.shape
    nb = Nt // TOK_BLK
    n_lat_blk = n_lat // TOK_BLK
    fwd = lambda s: jnp.where(s == 0, n_lat_blk, s - 1)
    bwd = lambda s: jnp.where(s == 0, n_lat_blk, n_lat_blk - s)
    wqk = GLA_HEADS * GLA_DK
    spec = lambda blk, w, col: pl.BlockSpec((None, TOK_BLK, w), lambda b, s: (b, blk(s), col // w))
    hg = lambda blk, f_col: [spec(blk, BRANCH_W, COL_HG), spec(blk, BRANCH_W, f_col),
                             spec(blk, BRANCH_W, COL_HG + 3 * BRANCH_W)]
    gla = lambda blk: [spec(blk, wqk, COL_GLA), spec(blk, wqk, COL_GLA + wqk), spec(blk, BRANCH_W, COL_GLA + 2 * wqk),
                       spec(blk, LANES, COL_GLR)]
    const = lambda shape: pl.BlockSpec(shape, lambda b, s: (0,) * len(shape))
    out = lambda blk: pl.BlockSpec((None, TOK_BLK, BRANCH_W), lambda b, s: (b, blk(s), 0))
    shp = jax.ShapeDtypeStruct((B, Nt, BRANCH_W), BF16)
    return pl.pallas_call(
        _scans_kernel,
        out_shape=(shp, shp, shp, shp),
        grid=(B, nb),
        in_specs=(hg(fwd, COL_HG + BRANCH_W) + hg(bwd, COL_HG + 2 * BRANCH_W) + [const(lbvec.shape)]
                  + gla(fwd) + gla(bwd) + [const(w2pad.shape), const(b2.shape)]),
        out_specs=(out(fwd), out(bwd), out(fwd), out(bwd)),
        scratch_shapes=[pltpu.VMEM((4, HG_HEADS, LANES, LANES), F32)],
        compiler_params=_cparams(("parallel", "arbitrary")),
        name="linear_scans",
    )(*([p3] * 6), lbvec, *([p3] * 8), w2pad, b2)


def _group_norm(o, g):
    parts = []
    for hd in range(o.shape[1] // LANES):
        oh = o[:, LANES * hd:LANES * (hd + 1)]
        parts.append(oh * lax.rsqrt(jnp.mean(oh * oh, axis=-1, keepdims=True) + EPS))
    return jnp.concatenate(parts, axis=-1) * g


def _merge_kernel(x_ref, h_ref, za_ref, zb_ref, zc_ref, zd_ref, ya_ref, yb_ref, hf_ref, hb_ref, gf_ref, gb_ref,
                  wbr_ref, wmg_ref, bmg_ref, wout_ref, ghg_ref, ggl_ref, m_ref, o_ref):
    h = h_ref[...]
    ys = (
        ya_ref[...].astype(F32),
        yb_ref[...].astype(F32),
        _group_norm(hf_ref[...].astype(F32) + hb_ref[...].astype(F32), ghg_ref[...]),
        _group_norm(gf_ref[...].astype(F32) + gb_ref[...].astype(F32), ggl_ref[...]),
    )
    zs = (za_ref, zb_ref, zc_ref, zd_ref)
    acc = None
    for br in range(N_BRANCH):
        gate = jax.nn.sigmoid(_dot(h, wmg_ref[br]) + bmg_ref[br])
        z = zs[br][...].astype(F32)
        yz = (ys[br] * (z * jax.nn.sigmoid(z))).astype(BF16)
        part = gate * _dot(yz, wbr_ref[br])
        acc = part if acc is None else acc + part
    out = _dot(acc.astype(BF16), wout_ref[...])
    o_ref[...] = x_ref[...] + m_ref[2:3, :] * out


def _merge(xs, h, p3, ys, w_br, w_merge, b_merge, w_out, g_hg, g_gla, modsel, n_lat_blk, n_blk):
    B, _, D = xs.shape
    tok = lambda w: pl.BlockSpec((None, TOK_BLK, w), lambda b, i: (b, i, 0))
    zspec = lambda br: pl.BlockSpec((None, TOK_BLK, BRANCH_W), lambda b, i: (b, i, COL_Z // BRANCH_W + br))
    once = pl.Buffered(1)
    const = lambda shape: pl.BlockSpec(shape, lambda b, i: (0,) * len(shape), pipeline_mode=once)
    ghg = jnp.tile(g_hg, HG_HEADS).reshape(1, BRANCH_W)
    ggl = jnp.tile(g_gla, GLA_HEADS).reshape(1, BRANCH_W)
    bm = b_merge.reshape(N_BRANCH, 1, D)
    return pl.pallas_call(
        _merge_kernel,
        out_shape=jax.ShapeDtypeStruct((B, n_blk * TOK_BLK, D), F32),
        grid=(B, n_blk),
        in_specs=[tok(D), tok(D)] + [zspec(br) for br in range(N_BRANCH)] + [tok(BRANCH_W)] * 6 + [
            const(w_br.shape), const(w_merge.shape), const(bm.shape), const(w_out.shape),
            const(ghg.shape), const(ggl.shape),
            pl.BlockSpec((None, None, 3, D), lambda b, i: (b, jnp.where(i >= n_lat_blk, 1, 0), 0, 0)),
        ],
        out_specs=tok(D),
        compiler_params=_cparams(("parallel", "parallel"), vmem_mb=56),
        name="branch_merge",
    )(xs, h, p3, p3, p3, p3, *ys, w_br, w_merge, bm, w_out, ghg, ggl, modsel)


def kernel(x, c, ctx, c_ctx, ada_w, ada_b, norm_g, w_in, mla_w_uq, mla_w_ukv, mla_g_cq, mla_g_ckv, mla_g_q,
           mla_g_k, nat_rpb, nat_g_q, nat_g_k, hg_lb_logits, hg_g_o, gla_w2, gla_b2, gla_g_o, w_br, w_merge,
           b_merge, w_out):
    B, n_lat, D = x.shape
    n_ctx = ctx.shape[1]
    depth = ada_w.shape[0]
    Nt = n_lat + n_ctx
    n_lat_blk = n_lat // TOK_BLK
    assert n_ctx == TOK_BLK and n_lat % NAT_PREP_BLK == 0 and Nt % MM_ROW_BLK == 0
    assert Nt % MLA_TK == 0 and n_lat % MLA_TQ == 0

    cc = jnp.zeros((8, D), F32).at[:B].set(c).at[B].set(c_ctx)
    ada = _ada_vectors(cc, ada_w.astype(BF16), ada_b).reshape(depth, 8, 3, D)
    sm = jax.nn.softmax(hg_lb_logits.astype(F32), axis=0)
    lower = jnp.maximum(jnp.cumsum(sm, axis=0) - sm[0], 0.0)
    tabs = _rope_tables(n_lat, n_ctx)

    xs = jnp.concatenate([x, ctx], axis=1)
    for l in range(depth):
        need_ctx = l < depth - 1
        modsel = jnp.stack([ada[l, :B], jnp.broadcast_to(ada[l, B], (B, 3, D))], axis=1)
        p3, h = _norm_project(xs, norm_g[l], modsel, _pack_w_in(w_in[l]), n_lat)

        mw = _mla_weights(mla_w_uq[l], mla_w_ukv[l], mla_g_cq[l], mla_g_ckv[l], mla_g_q[l], mla_g_k[l])
        y_a = _mla_attention(*_mla_prep(p3, mw, tabs), n_lat, need_ctx)
        y_b = _nat_attention(p3, nat_rpb[l], nat_g_q[l], nat_g_k[l], n_lat, need_ctx)

        lb = lower[l]
        lbvec = jnp.stack([jnp.log(lb), jnp.log1p(-lb), 1.0 - lb], axis=1)
        w2pad = jnp.zeros((2, LANES, GLA_HEADS * GLA_DK), F32)
        for d in range(2):
            w2pad = w2pad.at[d, GLA_LOWRANK * d:GLA_LOWRANK * (d + 1)].set(gla_w2[l, d])
        scans = _linear_scans(p3, lbvec, w2pad.astype(BF16), gla_b2[l].reshape(2, 1, -1), n_lat)

        n_blk = Nt // TOK_BLK if need_ctx else n_lat_blk
        xs = _merge(xs, h, p3, (y_a, y_b, *scans), w_br[l].astype(BF16), w_merge[l].astype(BF16), b_merge[l],
                    w_out[l].astype(BF16), hg_g_o[l], gla_g_o[l], modsel, n_lat_blk, n_blk)
    return xs[:, :n_lat]
```

```python
import functools

import jax
import jax.numpy as jnp
import numpy as np
from jax import lax
from jax.experimental import pallas as pl
from jax.experimental.pallas import tpu as pltpu

F32 = jnp.float32
BF16 = jnp.bfloat16

GRID_W = 64
N_BRANCH = 4
BRANCH_W = 512
MLA_HEADS = 8
MLA_NOPE = 64
MLA_ROPE = 32
MLA_V = 64
MLA_QK = MLA_NOPE + MLA_ROPE
Q_LORA = 256
KV_LORA = 128
ROPE_BASE = 10000.0
NAT_HEADS = 8
NAT_DH = 64
NAT_KH = 8
NAT_KW = 16
HG_HEADS = 4
HG_DK = 128
GLA_HEADS = 4
GLA_DK = 64
GLA_LOWRANK = 16
GLA_NORMALIZER = 16.0
EPS = 1e-6

W_MLA = Q_LORA + KV_LORA + MLA_ROPE
W_NAT = 3 * BRANCH_W
W_HG = 4 * BRANCH_W
W_GLA = 2 * GLA_HEADS * GLA_DK + BRANCH_W + 2 * GLA_LOWRANK

LANES = 128
HEAD_PAD = 128
TOK_BLK = 256

COL_HG = 0
COL_Z = COL_HG + W_HG
COL_NAT = COL_Z + N_BRANCH * BRANCH_W
COL_GLA = COL_NAT + W_NAT
COL_MLA = COL_GLA + 1024
COL_GLR = COL_MLA + 512
IN_W_PAD = COL_GLR + LANES
MM_COL_BLK = IN_W_PAD // 3
MM_ROW_BLK = 768

SUB = 16
CHUNK = 64
EXP_CAP = 80.0
NEG_BIG = -1e30
LOG2E = 1.4426950408889634


def _cparams(sem, vmem_mb=48):
    return pltpu.CompilerParams(dimension_semantics=sem, vmem_limit_bytes=vmem_mb * 1024 * 1024)


def _dot(a, b):
    return jnp.dot(a, b, preferred_element_type=F32)


def _dot_nt(a, b):
    return lax.dot_general(a, b, (((1,), (1,)), ((), ())), preferred_element_type=F32)


def _sigmoid(x):
    return 0.5 * jnp.tanh(0.5 * x) + 0.5


def _ada_kernel(c_ref, w_ref, b_ref, o_ref):
    c = c_ref[...]
    a = (c * jax.nn.sigmoid(c)).astype(BF16)
    o_ref[...] = _dot(a, w_ref[...]) + b_ref[...]


def _ada_vectors(cc, ada_w, ada_b):
    L, D, D3 = ada_w.shape
    tn = 1024
    return pl.pallas_call(
        _ada_kernel,
        out_shape=jax.ShapeDtypeStruct((L, cc.shape[0], D3), F32),
        grid=(L, D3 // tn),
        in_specs=[
            pl.BlockSpec((cc.shape[0], D), lambda l, j: (0, 0)),
            pl.BlockSpec((None, D, tn), lambda l, j: (l, 0, j)),
            pl.BlockSpec((None, 1, tn), lambda l, j: (l, 0, j)),
        ],
        out_specs=pl.BlockSpec((None, cc.shape[0], tn), lambda l, j: (l, 0, j)),
        compiler_params=_cparams(("parallel", "parallel")),
        name="ada_vectors",
    )(cc, ada_w, ada_b.reshape(L, 1, D3))


def _norm_proj_kernel(x_ref, tail_ref, g_ref, m_ref, w_ref, p_ref, h_ref, hs_ref, *, n_lat):
    @pl.when(pl.program_id(2) == 0)
    def _():
        x = jnp.where(pl.program_id(1) == pl.num_programs(1) - 1, tail_ref[...], x_ref[...])
        y = x * lax.rsqrt(jnp.mean(x * x, axis=-1, keepdims=True) + EPS) * g_ref[...]
        rows = x.shape[0]
        row = lax.broadcasted_iota(jnp.int32, (rows, 1), 0) + pl.program_id(1) * rows
        is_ctx = row >= n_lat
        scale = jnp.where(is_ctx, m_ref[1, 1:2, :], m_ref[0, 1:2, :])
        shift = jnp.where(is_ctx, m_ref[1, 0:1, :], m_ref[0, 0:1, :])
        h = (y * (1.0 + scale) + shift).astype(BF16)
        hs_ref[...] = h
        h_ref[...] = h

    p_ref[...] = _dot(hs_ref[...], w_ref[...]).astype(p_ref.dtype)


def _norm_project(x_main, x_tail, g, modsel, w, n_lat, Nt):
    B, _, D = x_main.shape
    N = w.shape[1]
    last_main = Nt // MM_ROW_BLK - 2
    return pl.pallas_call(
        functools.partial(_norm_proj_kernel, n_lat=n_lat),
        out_shape=(jax.ShapeDtypeStruct((B, Nt, N), BF16), jax.ShapeDtypeStruct((B, Nt, D), BF16)),
        grid=(B, Nt // MM_ROW_BLK, N // MM_COL_BLK),
        in_specs=[
            pl.BlockSpec((None, MM_ROW_BLK, D), lambda b, i, j: (b, jnp.minimum(i, last_main), 0)),
            pl.BlockSpec((None, MM_ROW_BLK, D), lambda b, i, j: (b, 0, 0)),
            pl.BlockSpec((1, D), lambda b, i, j: (0, 0)),
            pl.BlockSpec((None, 2, 3, D), lambda b, i, j: (b, 0, 0, 0)),
            pl.BlockSpec((D, MM_COL_BLK), lambda b, i, j: (0, j)),
        ],
        out_specs=(
            pl.BlockSpec((None, MM_ROW_BLK, MM_COL_BLK), lambda b, i, j: (b, i, j)),
            pl.BlockSpec((None, MM_ROW_BLK, D), lambda b, i, j: (b, i, 0)),
        ),
        scratch_shapes=[pltpu.VMEM((MM_ROW_BLK, D), BF16)],
        compiler_params=_cparams(("parallel", "parallel", "arbitrary")),
        name="norm_project",
    )(x_main, x_tail, g.reshape(1, D), modsel, w)


def _pack_w_in(w_in_l):
    o_nat = W_MLA
    o_hg = o_nat + W_NAT
    o_gla = o_hg + W_HG
    o_z = o_gla + W_GLA
    D = w_in_l.shape[0]
    zeros = lambda n: jnp.zeros((D, n), w_in_l.dtype)
    cols = [
        w_in_l[:, o_hg:o_hg + W_HG],
        w_in_l[:, o_z:o_z + N_BRANCH * BRANCH_W],
        w_in_l[:, o_nat:o_nat + W_NAT],
        w_in_l[:, o_gla:o_gla + 1024],
        w_in_l[:, :W_MLA], zeros(512 - W_MLA),
        w_in_l[:, o_gla + 1024:o_gla + W_GLA], zeros(LANES - 2 * GLA_LOWRANK),
    ]
    return jnp.concatenate(cols, axis=1).astype(BF16)


MLA_TQ = 1024
MLA_TK = 768


def _mla_prep_kernel(p_ref, wq_ref, wk_ref, wv_ref, gcq_ref, gckv_ref, gq_ref, gk_ref, vone_ref,
                     c_ref, s_ref, qc_ref, kc_ref, v_ref, *, n_lat_blk):
    ctx_w = jnp.where(pl.program_id(1) == n_lat_blk, 1.0, 0.0)
    hw = MLA_HEADS * HEAD_PAD
    p = p_ref[...].astype(F32)
    cq = p[:, :Q_LORA]
    r = lax.rsqrt(jnp.mean(cq * cq, axis=-1, keepdims=True) + EPS)
    cqn = (cq * r * gcq_ref[...]).astype(BF16)
    qraw = _dot(cqn, wq_ref[...])
    kv = p[:, Q_LORA:]
    ckv = kv[:, :KV_LORA]
    r2 = lax.rsqrt(jnp.mean(ckv * ckv, axis=-1, keepdims=True) + EPS)
    lane = lax.broadcasted_iota(jnp.int32, kv.shape, 1)
    lhs = (kv * jnp.where(lane < KV_LORA, r2, 1.0) * gckv_ref[...]).astype(BF16)
    kraw = _dot(lhs, wk_ref[...])
    vraw = _dot(lhs, wv_ref[...]) + vone_ref[...]
    cs = c_ref[...]
    sn = s_ref[...]
    inv = 1.0 / MLA_QK
    for hd in range(MLA_HEADS):
        sl = slice(HEAD_PAD * hd, HEAD_PAD * (hd + 1))
        sw = slice(hw + HEAD_PAD * hd, hw + HEAD_PAD * (hd + 1))
        qh = qraw[:, sl]
        rq = lax.rsqrt(jnp.sum(qh * qh, axis=-1, keepdims=True) * inv + EPS)
        qn = qh * rq * gq_ref[0:1, :]
        q_rot = qn * cs + qraw[:, sw] * rq * gq_ref[1:2, :] * sn
        qc_ref[hd] = jnp.concatenate([q_rot, qn], axis=1).astype(qc_ref.dtype)
        kh = kraw[:, sl]
        rk = lax.rsqrt(jnp.sum(kh * kh, axis=-1, keepdims=True) * inv + EPS)
        kn = kh * rk * gk_ref[0:1, :]
        k_rot = kn * cs + kraw[:, sw] * rk * gk_ref[1:2, :] * sn
        kc_ref[hd] = jnp.concatenate([k_rot * (1.0 - ctx_w), kn * ctx_w], axis=1).astype(kc_ref.dtype)
        v_ref[hd] = vraw[:, sl].astype(v_ref.dtype)


def _mla_weights(w_uq, w_ukv, g_cq, g_ckv, g_q, g_k):
    H = MLA_HEADS
    half = MLA_ROPE // 2
    lo, mid, hi = MLA_NOPE, MLA_NOPE + half, MLA_QK

    def swapped(t):
        z = jnp.zeros_like(t)
        return z.at[..., lo:mid].set(-t[..., mid:hi]).at[..., mid:hi].set(t[..., lo:mid])

    def both(t):
        return jnp.concatenate([t, swapped(t)], axis=1).reshape(t.shape[0], 2 * H * HEAD_PAD).astype(BF16)

    wq = jnp.pad(w_uq.reshape(Q_LORA, H, MLA_QK), ((0, 0), (0, 0), (0, HEAD_PAD - MLA_QK)))
    wkv = w_ukv.reshape(KV_LORA, H, MLA_NOPE + MLA_V)
    wk = jnp.pad(wkv[:, :, :MLA_NOPE], ((0, 0), (0, 0), (0, HEAD_PAD - MLA_NOPE)))
    place = jnp.zeros((MLA_ROPE, H, HEAD_PAD), F32)
    place = place.at[jnp.arange(MLA_ROPE), :, MLA_NOPE + jnp.arange(MLA_ROPE)].set(1.0)
    wk = jnp.concatenate([wk, place, jnp.zeros((256 - KV_LORA - MLA_ROPE, H, HEAD_PAD), F32)], axis=0)
    wv = jnp.pad(wkv[:, :, MLA_NOPE:], ((0, 256 - KV_LORA), (0, 0), (0, HEAD_PAD - MLA_V)))
    wv = wv.reshape(256, H * HEAD_PAD).astype(BF16)
    vone = jnp.zeros((H, HEAD_PAD), F32).at[:, MLA_V].set(1.0).reshape(1, H * HEAD_PAD)
    gckv = jnp.concatenate([g_ckv, jnp.ones((256 - KV_LORA,), F32)]).reshape(1, 256)
    gq = jnp.pad(g_q * (MLA_QK ** -0.5 * LOG2E), (0, HEAD_PAD - MLA_QK))
    gk = jnp.pad(g_k, (0, HEAD_PAD - MLA_QK))
    unsign = jnp.where(jnp.arange(HEAD_PAD) < mid, -1.0, 1.0)
    gq2 = jnp.stack([gq, swapped(gq) * unsign], axis=0)
    gk2 = jnp.stack([gk, swapped(gk) * unsign], axis=0)
    return both(wq), both(wk), wv, g_cq.reshape(1, Q_LORA), gckv, gq2, gk2, vone


def _rope_tables(n_lat, n_ctx):
    quarter = MLA_ROPE // 4
    inv_freq = ROPE_BASE ** (-jnp.arange(quarter, dtype=F32) / quarter)
    t = jnp.arange(n_lat, dtype=jnp.int32)
    row = (t // GRID_W).astype(F32)
    col = (t % GRID_W).astype(F32)
    ang = jnp.concatenate([row[:, None] * inv_freq, col[:, None] * inv_freq], axis=-1)
    cos, sin = jnp.cos(ang), jnp.sin(ang)
    ones = jnp.ones((n_lat, MLA_NOPE), F32)
    tail = jnp.ones((n_lat, HEAD_PAD - MLA_QK), F32)
    zl = jnp.zeros((n_lat, MLA_NOPE), F32)
    zt = jnp.zeros((n_lat, HEAD_PAD - MLA_QK), F32)
    c = jnp.concatenate([ones, cos, cos, tail], axis=1)
    s = jnp.concatenate([zl, sin, sin, zt], axis=1)
    c = jnp.concatenate([c, jnp.ones((n_ctx, HEAD_PAD), F32)], axis=0)
    s = jnp.concatenate([s, jnp.zeros((n_ctx, HEAD_PAD), F32)], axis=0)
    return c, s


def _mla_prep(p3, mw, tabs):
    B, Nt, _ = p3.shape
    H = MLA_HEADS
    wq, wk, wv, gcq, gckv, gq, gk, vone = mw
    n_lat_blk = Nt // TOK_BLK - 1
    const = lambda shape: pl.BlockSpec(shape, lambda b, i: (0,) * len(shape))
    tab = pl.BlockSpec((TOK_BLK, HEAD_PAD), lambda b, i: (i, 0))
    out = lambda w: pl.BlockSpec((None, H, TOK_BLK, w), lambda b, i: (b, 0, i, 0))
    shp = lambda w: jax.ShapeDtypeStruct((B, H, Nt, w), BF16)
    return pl.pallas_call(
        functools.partial(_mla_prep_kernel, n_lat_blk=n_lat_blk),
        out_shape=(shp(2 * HEAD_PAD), shp(2 * HEAD_PAD), shp(HEAD_PAD)),
        grid=(B, Nt // TOK_BLK),
        in_specs=[
            pl.BlockSpec((None, TOK_BLK, 512), lambda b, i: (b, i, COL_MLA // 512)),
            const(wq.shape), const(wk.shape), const(wv.shape), const(gcq.shape), const(gckv.shape),
            const(gq.shape), const(gk.shape), const(vone.shape), tab, tab,
        ],
        out_specs=(out(2 * HEAD_PAD), out(2 * HEAD_PAD), out(HEAD_PAD)),
        compiler_params=_cparams(("parallel", "parallel")),
        name="mla_prep",
    )(p3, wq, wk, wv, gcq, gckv, gq, gk, vone, *tabs)


def _softmax_step(q, kc, vc, m, acc):
    s = _dot_nt(q, kc)
    m_new = jnp.maximum(m, jnp.max(s, axis=-1, keepdims=True))
    p = jnp.exp2((s - m_new).astype(BF16))
    acc = jnp.exp2(m - m_new) * acc + _dot(p, vc)
    return m_new, acc


def _mla_attn_kernel(qc_ref, kc_ref, v_ref, o_ref, *, n_chunks):
    tq = qc_ref.shape[1]

    def body(c, carry):
        off = pl.multiple_of(c * MLA_TK, MLA_TK)
        return tuple(
            _softmax_step(qc_ref[hh], kc_ref[hh, pl.ds(off, MLA_TK), :], v_ref[hh, pl.ds(off, MLA_TK), :], *carry[hh])
            for hh in range(2))

    m0 = jnp.full((tq, 1), NEG_BIG, F32)
    a0 = jnp.zeros((tq, HEAD_PAD), F32)
    carry = lax.fori_loop(0, n_chunks, body, ((m0, a0), (m0, a0)), unroll=True)
    outs = [acc[:, :MLA_V] / acc[:, MLA_V:MLA_V + 1] for _, acc in carry]
    o_ref[...] = jnp.concatenate(outs, axis=-1).astype(o_ref.dtype)


def _mla_ctx_kernel(qc_ref, kc_ref, v_ref, o_ref):
    outs = []
    for hh in range(2):
        s = _dot_nt(qc_ref[hh], kc_ref[hh])
        p = jnp.exp2(s - jnp.max(s, axis=-1, keepdims=True))
        acc = _dot(p.astype(BF16), v_ref[hh])
        outs.append(acc[:, :MLA_V] / acc[:, MLA_V:MLA_V + 1])
    o_ref[...] = jnp.concatenate(outs, axis=-1).astype(o_ref.dtype)


def _mla_attention(qc, kc, v, n_lat, need_ctx):
    B, H, Nt, _ = kc.shape
    n_ctx = Nt - n_lat
    o_lat = pl.pallas_call(
        functools.partial(_mla_attn_kernel, n_chunks=Nt // MLA_TK),
        out_shape=jax.ShapeDtypeStruct((B, n_lat, BRANCH_W), BF16),
        grid=(B, H // 2, n_lat // MLA_TQ),
        in_specs=[
            pl.BlockSpec((None, 2, MLA_TQ, 2 * HEAD_PAD), lambda b, g, i: (b, g, i, 0)),
            pl.BlockSpec((None, 2, Nt, 2 * HEAD_PAD), lambda b, g, i: (b, g, 0, 0)),
            pl.BlockSpec((None, 2, Nt, HEAD_PAD), lambda b, g, i: (b, g, 0, 0)),
        ],
        out_specs=pl.BlockSpec((None, MLA_TQ, 2 * MLA_V), lambda b, g, i: (b, i, g)),
        compiler_params=_cparams(("parallel", "parallel", "arbitrary")),
        name="mla_attention",
    )(qc, kc, v)
    if not need_ctx:
        return o_lat
    cblk = n_lat // n_ctx
    cspec = lambda w: pl.BlockSpec((None, 2, n_ctx, w), lambda b, g: (b, g, cblk, 0))
    o_ctx = pl.pallas_call(
        _mla_ctx_kernel,
        out_shape=jax.ShapeDtypeStruct((B, n_ctx, BRANCH_W), BF16),
        grid=(B, H // 2),
        in_specs=[cspec(2 * HEAD_PAD), cspec(2 * HEAD_PAD), cspec(HEAD_PAD)],
        out_specs=pl.BlockSpec((None, n_ctx, 2 * MLA_V), lambda b, g: (b, 0, g)),
        compiler_params=_cparams(("parallel", "parallel")),
        name="mla_attention_ctx",
    )(qc, kc, v)
    return jnp.concatenate([o_lat, o_ctx], axis=1)


NAT_ROWS_BLK = 8
NAT_TQ = NAT_ROWS_BLK * GRID_W
NAT_WIN_ROWS = 16
NAT_WIN = NAT_WIN_ROWS * GRID_W
NAT_PREP_BLK = 1024


def _pair_norm(x, g, half0):
    x2 = x * x
    s0 = jnp.sum(jnp.where(half0, x2, 0.0), axis=-1, keepdims=True)
    s1 = jnp.sum(jnp.where(half0, 0.0, x2), axis=-1, keepdims=True)
    ms = jnp.where(half0, s0, s1) * (1.0 / NAT_DH)
    return x * lax.rsqrt(ms + EPS) * g


def _nat_kernel(q_ref, kl_ref, vl_ref, kc_ref, vc_ref, bias_ref, gq_ref, gk_ref, o_ref, kn_ref, vb_ref,
                *, n_lat, rows_total):
    rb = pl.program_id(2)
    half0 = lax.broadcasted_iota(jnp.int32, (1, LANES), 1) < NAT_DH

    @pl.when(rb == 0)
    def _():
        def body(c, carry):
            off = pl.multiple_of(c * NAT_PREP_BLK, NAT_PREP_BLK)
            kn_ref[pl.ds(off, NAT_PREP_BLK), :] = _pair_norm(
                kl_ref[pl.ds(off, NAT_PREP_BLK), :].astype(F32), gk_ref[...], half0).astype(BF16)
            vb_ref[pl.ds(off, NAT_PREP_BLK), :LANES] = vl_ref[pl.ds(off, NAT_PREP_BLK), :].astype(BF16)
            vb_ref[pl.ds(off, NAT_PREP_BLK), LANES:] = jnp.ones((NAT_PREP_BLK, LANES), BF16)
            return carry

        lax.fori_loop(0, n_lat // NAT_PREP_BLK, body, 0)
        kn_ref[n_lat:, :] = _pair_norm(kc_ref[...].astype(F32), gk_ref[...], half0).astype(BF16)
        vb_ref[n_lat:, :LANES] = vc_ref[...].astype(BF16)
        vb_ref[n_lat:, LANES:] = jnp.ones((vc_ref.shape[0], LANES), BF16)

    q = _pair_norm(q_ref[...].astype(F32), gq_ref[...], half0)
    ws = jnp.clip(rb * NAT_ROWS_BLK - NAT_KH // 2, 0, rows_total - NAT_WIN_ROWS)
    off = pl.multiple_of(ws * GRID_W, GRID_W)
    kw = kn_ref[pl.ds(off, NAT_WIN), :]
    vw = vb_ref[pl.ds(off, NAT_WIN), :]
    kc = kn_ref[n_lat:, :]
    vc = vb_ref[n_lat:, :]
    qms = [jnp.where(half0 if hh == 0 else jnp.logical_not(half0), q, 0.0).astype(BF16) for hh in range(2)]
    sws = [_dot_nt(qm, kw) + bias_ref[hh].astype(F32) for hh, qm in enumerate(qms)]
    scs = [_dot_nt(qm, kc) for qm in qms]
    ms = [jnp.maximum(jnp.max(sw, axis=-1, keepdims=True), jnp.max(sc, axis=-1, keepdims=True))
          for sw, sc in zip(sws, scs)]
    accs = [_dot(jnp.exp2((sw - m).astype(BF16)), vw) + _dot(jnp.exp2((sc - m).astype(BF16)), vc)
            for sw, sc, m in zip(sws, scs, ms)]
    outs = [acc[:, :LANES] / acc[:, LANES:] for acc in accs]
    o_ref[...] = jnp.where(half0, outs[0], outs[1]).astype(o_ref.dtype)


def _nat_ctx_kernel(q_ref, kc_ref, vc_ref, gq_ref, gk_ref, o_ref):
    half0 = lax.broadcasted_iota(jnp.int32, (1, LANES), 1) < NAT_DH
    q = _pair_norm(q_ref[...].astype(F32), gq_ref[...], half0)
    kc = _pair_norm(kc_ref[...].astype(F32), gk_ref[...], half0).astype(BF16)
    vc = vc_ref[...].astype(BF16)
    outs = []
    for hh in range(2):
        sel = half0 if hh == 0 else jnp.logical_not(half0)
        s = _dot_nt(jnp.where(sel, q, 0.0).astype(BF16), kc)
        p = jnp.exp2(s - jnp.max(s, axis=-1, keepdims=True))
        outs.append(_dot(p.astype(BF16), vc) / jnp.sum(p, axis=-1, keepdims=True))
    o_ref[...] = jnp.where(half0, outs[0], outs[1]).astype(o_ref.dtype)


def _nat_bias_tables(rpb, rows_total):
    rbs = np.array([0, min(NAT_ROWS_BLK, rows_total - NAT_ROWS_BLK), rows_total - NAT_ROWS_BLK])
    ws = np.clip(rbs - NAT_KH // 2, 0, rows_total - NAT_WIN_ROWS)
    qrow = rbs[:, None] + np.arange(NAT_ROWS_BLK)[None, :]
    r0 = np.clip(qrow - NAT_KH // 2, 0, rows_total - NAT_KH)
    krow = ws[:, None] + np.arange(NAT_WIN_ROWS)[None, :]
    row_ok = (krow[:, None, :] >= r0[:, :, None]) & (krow[:, None, :] < r0[:, :, None] + NAT_KH)
    dr = np.clip(krow[:, None, :] - qrow[:, :, None] + NAT_KH - 1, 0, 2 * NAT_KH - 2)
    col = np.arange(GRID_W)
    c0 = np.clip(col - NAT_KW // 2, 0, GRID_W - NAT_KW)
    col_ok = (col[None, :] >= c0[:, None]) & (col[None, :] < c0[:, None] + NAT_KW)
    dc = np.clip(col[None, :] - col[:, None], -(NAT_KW - 1), NAT_KW - 1) + NAT_KW - 1
    oh_c = np.eye(2 * NAT_KW - 1, dtype=np.float32)[dc]
    by_col = jnp.einsum('hrc,qkc->hrqk', rpb.astype(F32), oh_c, precision=lax.Precision.HIGHEST)
    by_col = jnp.where(jnp.asarray(col_ok), by_col * LOG2E, NEG_BIG).astype(BF16)
    masked = jnp.full(by_col[:, 0].shape, NEG_BIG, BF16)
    variants = []
    for v in range(3):
        rows = [jnp.concatenate([by_col[:, dr[v, i, j]] if row_ok[v, i, j] else masked
                                 for j in range(NAT_WIN_ROWS)], axis=-1) for i in range(NAT_ROWS_BLK)]
        variants.append(jnp.concatenate(rows, axis=-2))
    return jnp.stack(variants).reshape(3, NAT_HEADS // 2, 2, NAT_TQ, NAT_WIN)


def _nat_attention(p3, rpb, g_q, g_k, n_lat, need_ctx):
    B, Nt, _ = p3.shape
    n_ctx = Nt - n_lat
    rows_total = n_lat // GRID_W
    n_rb = rows_total // NAT_ROWS_BLK
    bias = _nat_bias_tables(rpb, rows_total)
    gq = jnp.tile(g_q * (NAT_DH ** -0.5 * LOG2E), 2).reshape(1, LANES)
    gk = jnp.tile(g_k, 2).reshape(1, LANES)
    cq, ck, cv = ((COL_NAT + j * BRANCH_W) // LANES for j in range(3))
    cblk = n_lat // n_ctx

    def variant(rb):
        return jnp.where(rb == 0, 0, jnp.where(rb == n_rb - 1, 2, 1))

    vec = pl.BlockSpec((1, LANES), lambda b, g, rb: (0, 0))
    o_lat = pl.pallas_call(
        functools.partial(_nat_kernel, n_lat=n_lat, rows_total=rows_total),
        out_shape=jax.ShapeDtypeStruct((B, n_lat, BRANCH_W), BF16),
        grid=(B, NAT_HEADS // 2, n_rb),
        in_specs=[
            pl.BlockSpec((None, NAT_TQ, LANES), lambda b, g, rb: (b, rb, cq + g)),
            pl.BlockSpec((None, n_lat, LANES), lambda b, g, rb: (b, 0, ck + g)),
            pl.BlockSpec((None, n_lat, LANES), lambda b, g, rb: (b, 0, cv + g)),
            pl.BlockSpec((None, n_ctx, LANES), lambda b, g, rb: (b, cblk, ck + g)),
            pl.BlockSpec((None, n_ctx, LANES), lambda b, g, rb: (b, cblk, cv + g)),
            pl.BlockSpec((None, None, 2, NAT_TQ, NAT_WIN), lambda b, g, rb: (variant(rb), g, 0, 0, 0)),
            vec, vec,
        ],
        out_specs=pl.BlockSpec((None, NAT_TQ, LANES), lambda b, g, rb: (b, rb, g)),
        scratch_shapes=[pltpu.VMEM((Nt, LANES), BF16), pltpu.VMEM((Nt, 2 * LANES), BF16)],
        compiler_params=_cparams(("parallel", "parallel", "arbitrary")),
        name="nat_attention",
    )(p3, p3, p3, p3, p3, bias, gq, gk)
    if not need_ctx:
        return o_lat
    vec2 = pl.BlockSpec((1, LANES), lambda b, g: (0, 0))
    o_ctx = pl.pallas_call(
        _nat_ctx_kernel,
        out_shape=jax.ShapeDtypeStruct((B, n_ctx, BRANCH_W), BF16),
        grid=(B, NAT_HEADS // 2),
        in_specs=[
            pl.BlockSpec((None, n_ctx, LANES), lambda b, g: (b, cblk, cq + g)),
            pl.BlockSpec((None, n_ctx, LANES), lambda b, g: (b, cblk, ck + g)),
            pl.BlockSpec((None, n_ctx, LANES), lambda b, g: (b, cblk, cv + g)),
            vec2, vec2,
        ],
        out_specs=pl.BlockSpec((None, n_ctx, LANES), lambda b, g: (b, 0, g)),
        compiler_params=_cparams(("parallel", "parallel")),
        name="nat_attention_ctx",
    )(p3, p3, p3, gq, gk)
    return jnp.concatenate([o_lat, o_ctx], axis=1)


def _log_sigmoid(x):
    return jnp.minimum(x, 0.0) - jnp.log(1.0 + jnp.exp(-jnp.abs(x)))


def _run_block(operands, s_ref, o_ref, rev):
    qs, k, v, a, heads = operands()
    T, W = qs.shape
    n_ch, nsub, nsb = T // CHUNK, CHUNK // SUB, T // SUB
    row = lax.broadcasted_iota(jnp.int32, (T, T), 0)
    col = lax.broadcasted_iota(jnp.int32, (T, T), 1)
    same_chunk = (row // CHUNK) == (col // CHUNK)
    tri = jnp.where(same_chunk & ((row <= col) if rev else (row >= col)), 1.0, 0.0).astype(BF16)
    r64 = lax.broadcasted_iota(jnp.int32, (CHUNK, CHUNK), 0)
    c64 = lax.broadcasted_iota(jnp.int32, (CHUNK, CHUNK), 1)
    causal = (r64 <= c64) if rev else (r64 >= c64)
    hi = a.astype(BF16)
    mid = (a - hi.astype(F32)).astype(BF16)
    bc = _dot(tri, hi) + _dot(tri, mid)
    bex = bc - a
    first = [SUB * s + (SUB - 1 if rev else 0) for s in range(nsb)]
    last = [SUB * s + (0 if rev else SUB - 1) for s in range(nsb)]
    r_start = [bex[r:r + 1, :] for r in first]
    r_end = [bc[r:r + 1, :] for r in last]
    bend = [r_end[nsub * c + (0 if rev else nsub - 1)] for c in range(n_ch)]

    def spread(rows):
        return jnp.concatenate([jnp.broadcast_to(r, (SUB, W)) for r in rows], axis=0)

    rs_full = spread(r_start)
    qt = qs * jnp.exp(bc - rs_full)
    q_in = qt * spread([jnp.exp(r) for r in r_start])
    ke = k * jnp.exp(spread(r_end) - bc)
    k_out = ke * spread([jnp.exp(bend[s // nsub] - r_end[s]) for s in range(nsb)])
    kd = k * jnp.exp(jnp.minimum(rs_full - bc, EXP_CAP))
    zeros = jnp.zeros((SUB, W), F32)

    def keys_for(c, i):
        rows = []
        for j in range(nsub):
            sj, si = nsub * c + j, nsub * c + i
            blk = slice(SUB * sj, SUB * (sj + 1))
            if j == i:
                rows.append(kd[blk])
            elif (j > i) if rev else (j < i):
                rows.append(ke[blk] if abs(i - j) == 1 else ke[blk] * jnp.exp(r_start[si] - r_end[sj]))
            else:
                rows.append(zeros)
        return jnp.concatenate(rows, axis=0).astype(BF16)

    kts = [[keys_for(c, i) for i in range(nsub)] for c in range(n_ch)]
    vb = v.astype(BF16)
    yield
    intra, incr, q_state = [], [], []
    for hd, (ks, kmask, vs) in enumerate(heads):
        ksl = slice(ks, ks + LANES)
        vsl = slice(vs, vs + LANES)

        def msk(t, kmask=kmask):
            return t if kmask is None else jnp.where(kmask, t, 0.0)

        qt_h = msk(qt[:, ksl]).astype(BF16)
        ko_h = msk(k_out[:, ksl]).astype(BF16)
        q_state.append(msk(q_in[:, ksl]).astype(BF16))
        intra_h, incr_h = [], []
        for c in range(n_ch):
            cs = slice(CHUNK * c, CHUNK * (c + 1))
            blocks = [_dot_nt(qt_h[SUB * (nsub * c + i):SUB * (nsub * c + i + 1), :], kts[c][i][:, ksl])
                      for i in range(nsub)]
            sc = jnp.where(causal, jnp.concatenate(blocks, axis=0), 0.0).astype(BF16)
            intra_h.append(_dot(sc, vb[cs, vsl]))
            incr_h.append(_dot(v[cs, vsl].T.astype(BF16), ko_h[cs]))
        intra.append(intra_h)
        incr.append(incr_h)
    yield
    states = [s_ref[hd] for hd in range(len(heads))]
    for c in (range(n_ch - 1, -1, -1) if rev else range(n_ch)):
        cs = slice(CHUNK * c, CHUNK * (c + 1))
        dec = jnp.exp(bend[c])
        outs = []
        for hd, (ks, _, _) in enumerate(heads):
            outs.append(intra[hd][c] + _dot_nt(q_state[hd][cs], states[hd].astype(BF16)))
            states[hd] = states[hd] * dec[:, ks:ks + LANES] + incr[hd][c]
        o_ref[cs, :] = jnp.concatenate(outs, axis=-1).astype(o_ref.dtype)
        yield
    for hd, st in enumerate(states):
        s_ref[hd] = st


def _hgrn2_operands(q_ref, f_ref, v_ref, lb):
    q = q_ref[...].astype(F32)
    f = f_ref[...].astype(F32)
    loglb, log1mlb, oneml = lb[0:1, :], lb[1:2, :], lb[2:3, :]
    u = log1mlb + _log_sigmoid(f)
    a = jnp.maximum(u, loglb) + jnp.log(1.0 + jnp.exp(-jnp.abs(u - loglb)))
    k = oneml * _sigmoid(-f)
    qs = q * _sigmoid(q) * (HG_DK ** -0.5)
    heads = [(LANES * h, None, LANES * h) for h in range(HG_HEADS)]
    return qs, k, v_ref[...].astype(F32), a, heads


def _gla_operands(q_ref, k_ref, v_ref, r_ref, w2, b2):
    g = _dot(r_ref[...].astype(BF16), w2) + b2
    a = _log_sigmoid(g) * (1.0 / GLA_NORMALIZER)
    qs = q_ref[...].astype(F32) * (GLA_DK ** -0.5)
    half0 = lax.broadcasted_iota(jnp.int32, (1, LANES), 1) < GLA_DK
    heads = [(LANES * (h // 2), half0 if h % 2 == 0 else jnp.logical_not(half0), LANES * h)
             for h in range(GLA_HEADS)]
    return qs, k_ref[...].astype(F32), v_ref[...].astype(F32), a, heads


def _scans_kernel(hq_f, hf_f, hv_f, hq_b, hf_b, hv_b, lb_ref, gq_f, gk_f, gv_f, gr_f, gq_b, gk_b, gv_b, gr_b,
                  w2_ref, b2_ref, oh_f, oh_b, og_f, og_b, s_ref):
    @pl.when(pl.program_id(1) == 0)
    def _():
        s_ref[...] = jnp.zeros_like(s_ref)

    streams = [
        _run_block(functools.partial(_hgrn2_operands, hq_f, hf_f, hv_f, lb_ref[0]), s_ref.at[0], oh_f, False),
        _run_block(functools.partial(_hgrn2_operands, hq_b, hf_b, hv_b, lb_ref[1]), s_ref.at[1], oh_b, True),
        _run_block(functools.partial(_gla_operands, gq_f, gk_f, gv_f, gr_f, w2_ref[0], b2_ref[0]),
                   s_ref.at[2], og_f, False),
        _run_block(functools.partial(_gla_operands, gq_b, gk_b, gv_b, gr_b, w2_ref[1], b2_ref[1]),
                   s_ref.at[3], og_b, True),
    ]
    done = object()
    while streams:
        streams = [g for g in streams if next(g, done) is not done]


def _linear_scans(p3, lbvec, w2pad, b2, n_lat):
    B, Nt, _ = p3.shape
    nb = Nt // TOK_BLK
    n_lat_blk = n_lat // TOK_BLK
    fwd = lambda s: jnp.where(s == 0, n_lat_blk, s - 1)
    bwd = lambda s: jnp.where(s == 0, n_lat_blk, n_lat_blk - s)
    wqk = GLA_HEADS * GLA_DK
    spec = lambda blk, w, col: pl.BlockSpec((None, TOK_BLK, w), lambda b, s: (b, blk(s), col // w))
    hg = lambda blk, f_col: [spec(blk, BRANCH_W, COL_HG), spec(blk, BRANCH_W, f_col),
                             spec(blk, BRANCH_W, COL_HG + 3 * BRANCH_W)]
    gla = lambda blk: [spec(blk, wqk, COL_GLA), spec(blk, wqk, COL_GLA + wqk), spec(blk, BRANCH_W, COL_GLA + 2 * wqk),
                       spec(blk, LANES, COL_GLR)]
    const = lambda shape: pl.BlockSpec(shape, lambda b, s: (0,) * len(shape))
    out = lambda blk: pl.BlockSpec((None, TOK_BLK, BRANCH_W), lambda b, s: (b, blk(s), 0))
    shp = jax.ShapeDtypeStruct((B, Nt, BRANCH_W), BF16)
    return pl.pallas_call(
        _scans_kernel,
        out_shape=(shp, shp, shp, shp),
        grid=(B, nb),
        in_specs=(hg(fwd, COL_HG + BRANCH_W) + hg(bwd, COL_HG + 2 * BRANCH_W) + [const(lbvec.shape)]
                  + gla(fwd) + gla(bwd) + [const(w2pad.shape), const(b2.shape)]),
        out_specs=(out(fwd), out(bwd), out(fwd), out(bwd)),
        scratch_shapes=[pltpu.VMEM((4, HG_HEADS, LANES, LANES), F32)],
        compiler_params=_cparams(("parallel", "arbitrary")),
        name="linear_scans",
    )(*([p3] * 6), lbvec, *([p3] * 8), w2pad, b2)


def _group_norm(o, g):
    parts = []
    for hd in range(o.shape[1] // LANES):
        oh = o[:, LANES * hd:LANES * (hd + 1)]
        parts.append(oh * lax.rsqrt(jnp.mean(oh * oh, axis=-1, keepdims=True) + EPS))
    return jnp.concatenate(parts, axis=-1) * g


def _merge_kernel(x_ref, xc_ref, h_ref, za_ref, zb_ref, zc_ref, zd_ref, ya_ref, yb_ref, hf_ref, hb_ref, gf_ref,
                  gb_ref, wbr_ref, wmg_ref, bmg_ref, wout_ref, ghg_ref, ggl_ref, m_ref, o_ref, *, n_lat_blk):
    h = h_ref[...]
    ys = (
        ya_ref[...].astype(F32),
        yb_ref[...].astype(F32),
        _group_norm(hf_ref[...].astype(F32) + hb_ref[...].astype(F32), ghg_ref[...]),
        _group_norm(gf_ref[...].astype(F32) + gb_ref[...].astype(F32), ggl_ref[...]),
    )
    zs = (za_ref, zb_ref, zc_ref, zd_ref)
    acc = None
    for br in range(N_BRANCH):
        gate = _sigmoid(_dot(h, wmg_ref[br]) + bmg_ref[br])
        z = zs[br][...].astype(F32)
        yz = (ys[br] * (z * _sigmoid(z))).astype(BF16)
        part = gate * _dot(yz, wbr_ref[br])
        acc = part if acc is None else acc + part
    out = _dot(acc.astype(BF16), wout_ref[...])
    x = jnp.where(pl.program_id(1) == n_lat_blk, xc_ref[...], x_ref[...])
    o_ref[...] = x + m_ref[2:3, :] * out


def _merge(x_lat, x_ctx, h, p3, ys, w_br, w_merge, b_merge, w_out, g_hg, g_gla, modsel, n_lat_blk, n_blk):
    B, _, D = x_lat.shape
    tok = lambda w: pl.BlockSpec((None, TOK_BLK, w), lambda b, i: (b, i, 0))
    lat = pl.BlockSpec((None, TOK_BLK, D), lambda b, i: (b, jnp.minimum(i, n_lat_blk - 1), 0))
    ctx = pl.BlockSpec((None, TOK_BLK, D), lambda b, i: (b, 0, 0))
    zspec = lambda br: pl.BlockSpec((None, TOK_BLK, BRANCH_W), lambda b, i: (b, i, COL_Z // BRANCH_W + br))
    once = pl.Buffered(1)
    const = lambda shape: pl.BlockSpec(shape, lambda b, i: (0,) * len(shape), pipeline_mode=once)
    ghg = jnp.tile(g_hg, HG_HEADS).reshape(1, BRANCH_W)
    ggl = jnp.tile(g_gla, GLA_HEADS).reshape(1, BRANCH_W)
    bm = b_merge.reshape(N_BRANCH, 1, D)
    return pl.pallas_call(
        functools.partial(_merge_kernel, n_lat_blk=n_lat_blk),
        out_shape=jax.ShapeDtypeStruct((B, n_blk * TOK_BLK, D), F32),
        grid=(B, n_blk),
        in_specs=[lat, ctx, tok(D)] + [zspec(br) for br in range(N_BRANCH)] + [tok(BRANCH_W)] * 6 + [
            const(w_br.shape), const(w_merge.shape), const(bm.shape), const(w_out.shape),
            const(ghg.shape), const(ggl.shape),
            pl.BlockSpec((None, None, 3, D), lambda b, i: (b, jnp.where(i >= n_lat_blk, 1, 0), 0, 0)),
        ],
        out_specs=tok(D),
        compiler_params=_cparams(("parallel", "parallel"), vmem_mb=56),
        name="branch_merge",
    )(x_lat, x_ctx, h, p3, p3, p3, p3, *ys, w_br, w_merge, bm, w_out, ghg, ggl, modsel)


def kernel(x, c, ctx, c_ctx, ada_w, ada_b, norm_g, w_in, mla_w_uq, mla_w_ukv, mla_g_cq, mla_g_ckv, mla_g_q,
           mla_g_k, nat_rpb, nat_g_q, nat_g_k, hg_lb_logits, hg_g_o, gla_w2, gla_b2, gla_g_o, w_br, w_merge,
           b_merge, w_out):
    B, n_lat, D = x.shape
    n_ctx = ctx.shape[1]
    depth = ada_w.shape[0]
    Nt = n_lat + n_ctx
    n_lat_blk = n_lat // TOK_BLK
    assert n_ctx == TOK_BLK and n_lat % NAT_PREP_BLK == 0 and Nt % MM_ROW_BLK == 0
    assert Nt % MLA_TK == 0 and n_lat % MLA_TQ == 0

    cc = jnp.zeros((8, D), F32).at[:B].set(c).at[B].set(c_ctx)
    ada = _ada_vectors(cc, ada_w.astype(BF16), ada_b).reshape(depth, 8, 3, D)
    sm = jax.nn.softmax(hg_lb_logits.astype(F32), axis=0)
    lower = jnp.maximum(jnp.cumsum(sm, axis=0) - sm[0], 0.0)
    tabs = _rope_tables(n_lat, n_ctx)

    x_lat, x_ctx = x, ctx
    for l in range(depth):
        need_ctx = l < depth - 1
        modsel = jnp.stack([ada[l, :B], jnp.broadcast_to(ada[l, B], (B, 3, D))], axis=1)
        x_tail = jnp.concatenate([x_lat[:, n_lat - (MM_ROW_BLK - n_ctx):n_lat], x_ctx], axis=1)
        p3, h = _norm_project(x_lat, x_tail, norm_g[l], modsel, _pack_w_in(w_in[l]), n_lat, Nt)

        mw = _mla_weights(mla_w_uq[l], mla_w_ukv[l], mla_g_cq[l], mla_g_ckv[l], mla_g_q[l], mla_g_k[l])
        y_a = _mla_attention(*_mla_prep(p3, mw, tabs), n_lat, need_ctx)
        y_b = _nat_attention(p3, nat_rpb[l], nat_g_q[l], nat_g_k[l], n_lat, need_ctx)

        lb = lower[l]
        lbvec = jnp.stack([jnp.log(lb), jnp.log1p(-lb), 1.0 - lb], axis=1)
        w2pad = jnp.zeros((2, LANES, GLA_HEADS * GLA_DK), F32)
        for d in range(2):
            w2pad = w2pad.at[d, GLA_LOWRANK * d:GLA_LOWRANK * (d + 1)].set(gla_w2[l, d])
        scans = _linear_scans(p3, lbvec, w2pad.astype(BF16), gla_b2[l].reshape(2, 1, -1), n_lat)

        n_blk = Nt // TOK_BLK if need_ctx else n_lat_blk
        xs = _merge(x_lat, x_ctx, h, p3, (y_a, y_b, *scans), w_br[l].astype(BF16), w_merge[l].astype(BF16),
                    b_merge[l], w_out[l].astype(BF16), hg_g_o[l], gla_g_o[l], modsel, n_lat_blk, n_blk)
        x_lat, x_ctx = xs, xs[:, n_lat:]
    return xs[:, :n_lat]
```

```python
import functools

import jax
import jax.numpy as jnp
import numpy as np
from jax import lax
from jax.experimental import pallas as pl
from jax.experimental.pallas import tpu as pltpu

F32 = jnp.float32
BF16 = jnp.bfloat16

GRID_W = 64
N_BRANCH = 4
BRANCH_W = 512
MLA_HEADS = 8
MLA_NOPE = 64
MLA_ROPE = 32
MLA_V = 64
MLA_QK = MLA_NOPE + MLA_ROPE
Q_LORA = 256
KV_LORA = 128
ROPE_BASE = 10000.0
NAT_HEADS = 8
NAT_DH = 64
NAT_KH = 8
NAT_KW = 16
HG_HEADS = 4
HG_DK = 128
GLA_HEADS = 4
GLA_DK = 64
GLA_LOWRANK = 16
GLA_NORMALIZER = 16.0
EPS = 1e-6

W_MLA = Q_LORA + KV_LORA + MLA_ROPE
W_NAT = 3 * BRANCH_W
W_HG = 4 * BRANCH_W
W_GLA = 2 * GLA_HEADS * GLA_DK + BRANCH_W + 2 * GLA_LOWRANK

LANES = 128
HEAD_PAD = 128
TOK_BLK = 256

COL_HG = 0
COL_Z = COL_HG + W_HG
COL_NAT = COL_Z + N_BRANCH * BRANCH_W
COL_GLA = COL_NAT + W_NAT
COL_MLA = COL_GLA + 1024
COL_GLR = COL_MLA + 512
IN_W_PAD = COL_GLR + LANES
MM_COL_BLK = IN_W_PAD // 3
MM_ROW_BLK = 768

SUB = 16
CHUNK = 64
EXP_CAP = 80.0
NEG_BIG = -1e30
LOG2E = 1.4426950408889634


def _cparams(sem, vmem_mb=48):
    return pltpu.CompilerParams(dimension_semantics=sem, vmem_limit_bytes=vmem_mb * 1024 * 1024)


def _dot(a, b):
    return jnp.dot(a, b, preferred_element_type=F32)


def _dot_nt(a, b):
    return lax.dot_general(a, b, (((1,), (1,)), ((), ())), preferred_element_type=F32)


def _sigmoid(x):
    return 0.5 * jnp.tanh(0.5 * x) + 0.5


def _ada_kernel(c_ref, w_ref, b_ref, o_ref):
    c = c_ref[...]
    a = (c * jax.nn.sigmoid(c)).astype(BF16)
    o_ref[...] = _dot(a, w_ref[...]) + b_ref[...]


def _ada_vectors(cc, ada_w, ada_b):
    L, D, D3 = ada_w.shape
    tn = 1024
    return pl.pallas_call(
        _ada_kernel,
        out_shape=jax.ShapeDtypeStruct((L, cc.shape[0], D3), F32),
        grid=(L, D3 // tn),
        in_specs=[
            pl.BlockSpec((cc.shape[0], D), lambda l, j: (0, 0)),
            pl.BlockSpec((None, D, tn), lambda l, j: (l, 0, j)),
            pl.BlockSpec((None, 1, tn), lambda l, j: (l, 0, j)),
        ],
        out_specs=pl.BlockSpec((None, cc.shape[0], tn), lambda l, j: (l, 0, j)),
        compiler_params=_cparams(("parallel", "parallel")),
        name="ada_vectors",
    )(cc, ada_w, ada_b.reshape(L, 1, D3))


def _norm_proj_kernel(x_ref, tail_ref, g_ref, m_ref, w_ref, p_ref, h_ref, hs_ref, *, n_lat):
    @pl.when(pl.program_id(2) == 0)
    def _():
        x = jnp.where(pl.program_id(1) == pl.num_programs(1) - 1, tail_ref[...], x_ref[...])
        y = x * lax.rsqrt(jnp.mean(x * x, axis=-1, keepdims=True) + EPS) * g_ref[...]
        rows = x.shape[0]
        row = lax.broadcasted_iota(jnp.int32, (rows, 1), 0) + pl.program_id(1) * rows
        is_ctx = row >= n_lat
        scale = jnp.where(is_ctx, m_ref[1, 1:2, :], m_ref[0, 1:2, :])
        shift = jnp.where(is_ctx, m_ref[1, 0:1, :], m_ref[0, 0:1, :])
        h = (y * (1.0 + scale) + shift).astype(BF16)
        hs_ref[...] = h
        h_ref[...] = h

    p_ref[...] = _dot(hs_ref[...], w_ref[...]).astype(p_ref.dtype)


def _norm_project(x_main, x_tail, g, modsel, w, n_lat, Nt):
    B, _, D = x_main.shape
    N = w.shape[1]
    last_main = Nt // MM_ROW_BLK - 2
    return pl.pallas_call(
        functools.partial(_norm_proj_kernel, n_lat=n_lat),
        out_shape=(jax.ShapeDtypeStruct((B, Nt, N), BF16), jax.ShapeDtypeStruct((B, Nt, D), BF16)),
        grid=(B, Nt // MM_ROW_BLK, N // MM_COL_BLK),
        in_specs=[
            pl.BlockSpec((None, MM_ROW_BLK, D), lambda b, i, j: (b, jnp.minimum(i, last_main), 0)),
            pl.BlockSpec((None, MM_ROW_BLK, D), lambda b, i, j: (b, 0, 0)),
            pl.BlockSpec((1, D), lambda b, i, j: (0, 0)),
            pl.BlockSpec((None, 2, 3, D), lambda b, i, j: (b, 0, 0, 0)),
            pl.BlockSpec((D, MM_COL_BLK), lambda b, i, j: (0, j)),
        ],
        out_specs=(
            pl.BlockSpec((None, MM_ROW_BLK, MM_COL_BLK), lambda b, i, j: (b, i, j)),
            pl.BlockSpec((None, MM_ROW_BLK, D), lambda b, i, j: (b, i, 0)),
        ),
        scratch_shapes=[pltpu.VMEM((MM_ROW_BLK, D), BF16)],
        compiler_params=_cparams(("parallel", "parallel", "arbitrary")),
        name="norm_project",
    )(x_main, x_tail, g.reshape(1, D), modsel, w)


def _pack_w_in(w_in_l):
    o_nat = W_MLA
    o_hg = o_nat + W_NAT
    o_gla = o_hg + W_HG
    o_z = o_gla + W_GLA
    D = w_in_l.shape[0]
    zeros = lambda n: jnp.zeros((D, n), w_in_l.dtype)
    cols = [
        w_in_l[:, o_hg:o_hg + W_HG],
        w_in_l[:, o_z:o_z + N_BRANCH * BRANCH_W],
        w_in_l[:, o_nat:o_nat + W_NAT],
        w_in_l[:, o_gla:o_gla + 1024],
        w_in_l[:, :W_MLA], zeros(512 - W_MLA),
        w_in_l[:, o_gla + 1024:o_gla + W_GLA], zeros(LANES - 2 * GLA_LOWRANK),
    ]
    return jnp.concatenate(cols, axis=1).astype(BF16)


MLA_TQ = 1024
MLA_TK = 768


def _mla_prep_kernel(p_ref, wq_ref, wk_ref, wv_ref, gcq_ref, gckv_ref, gq_ref, gk_ref, vone_ref,
                     c_ref, s_ref, qc_ref, kc_ref, v_ref, *, n_lat_blk):
    ctx_w = jnp.where(pl.program_id(1) == n_lat_blk, 1.0, 0.0)
    hw = MLA_HEADS * HEAD_PAD
    p = p_ref[...].astype(F32)
    cq = p[:, :Q_LORA]
    r = lax.rsqrt(jnp.mean(cq * cq, axis=-1, keepdims=True) + EPS)
    cqn = (cq * r * gcq_ref[...]).astype(BF16)
    qraw = _dot(cqn, wq_ref[...])
    kv = p[:, Q_LORA:]
    ckv = kv[:, :KV_LORA]
    r2 = lax.rsqrt(jnp.mean(ckv * ckv, axis=-1, keepdims=True) + EPS)
    lane = lax.broadcasted_iota(jnp.int32, kv.shape, 1)
    lhs = (kv * jnp.where(lane < KV_LORA, r2, 1.0) * gckv_ref[...]).astype(BF16)
    kraw = _dot(lhs, wk_ref[...])
    vraw = _dot(lhs, wv_ref[...]) + vone_ref[...]
    cs = c_ref[...]
    sn = s_ref[...]
    inv = 1.0 / MLA_QK
    for hd in range(MLA_HEADS):
        sl = slice(HEAD_PAD * hd, HEAD_PAD * (hd + 1))
        sw = slice(hw + HEAD_PAD * hd, hw + HEAD_PAD * (hd + 1))
        qh = qraw[:, sl]
        rq = lax.rsqrt(jnp.sum(qh * qh, axis=-1, keepdims=True) * inv + EPS)
        qn = qh * rq * gq_ref[0:1, :]
        q_rot = qn * cs + qraw[:, sw] * rq * gq_ref[1:2, :] * sn
        qc_ref[hd] = jnp.concatenate([q_rot, qn], axis=1).astype(qc_ref.dtype)
        kh = kraw[:, sl]
        rk = lax.rsqrt(jnp.sum(kh * kh, axis=-1, keepdims=True) * inv + EPS)
        kn = kh * rk * gk_ref[0:1, :]
        k_rot = kn * cs + kraw[:, sw] * rk * gk_ref[1:2, :] * sn
        kc_ref[hd] = jnp.concatenate([k_rot * (1.0 - ctx_w), kn * ctx_w], axis=1).astype(kc_ref.dtype)
        v_ref[hd] = vraw[:, sl].astype(v_ref.dtype)


def _mla_weights(w_uq, w_ukv, g_cq, g_ckv, g_q, g_k):
    H = MLA_HEADS
    half = MLA_ROPE // 2
    lo, mid, hi = MLA_NOPE, MLA_NOPE + half, MLA_QK

    def swapped(t):
        z = jnp.zeros_like(t)
        return z.at[..., lo:mid].set(-t[..., mid:hi]).at[..., mid:hi].set(t[..., lo:mid])

    def both(t):
        return jnp.concatenate([t, swapped(t)], axis=1).reshape(t.shape[0], 2 * H * HEAD_PAD).astype(BF16)

    wq = jnp.pad(w_uq.reshape(Q_LORA, H, MLA_QK), ((0, 0), (0, 0), (0, HEAD_PAD - MLA_QK)))
    wkv = w_ukv.reshape(KV_LORA, H, MLA_NOPE + MLA_V)
    wk = jnp.pad(wkv[:, :, :MLA_NOPE], ((0, 0), (0, 0), (0, HEAD_PAD - MLA_NOPE)))
    place = jnp.zeros((MLA_ROPE, H, HEAD_PAD), F32)
    place = place.at[jnp.arange(MLA_ROPE), :, MLA_NOPE + jnp.arange(MLA_ROPE)].set(1.0)
    wk = jnp.concatenate([wk, place, jnp.zeros((256 - KV_LORA - MLA_ROPE, H, HEAD_PAD), F32)], axis=0)
    wv = jnp.pad(wkv[:, :, MLA_NOPE:], ((0, 256 - KV_LORA), (0, 0), (0, HEAD_PAD - MLA_V)))
    wv = wv.reshape(256, H * HEAD_PAD).astype(BF16)
    vone = jnp.zeros((H, HEAD_PAD), F32).at[:, MLA_V].set(1.0).reshape(1, H * HEAD_PAD)
    gckv = jnp.concatenate([g_ckv, jnp.ones((256 - KV_LORA,), F32)]).reshape(1, 256)
    gq = jnp.pad(g_q * (MLA_QK ** -0.5 * LOG2E), (0, HEAD_PAD - MLA_QK))
    gk = jnp.pad(g_k, (0, HEAD_PAD - MLA_QK))
    unsign = jnp.where(jnp.arange(HEAD_PAD) < mid, -1.0, 1.0)
    gq2 = jnp.stack([gq, swapped(gq) * unsign], axis=0)
    gk2 = jnp.stack([gk, swapped(gk) * unsign], axis=0)
    return both(wq), both(wk), wv, g_cq.reshape(1, Q_LORA), gckv, gq2, gk2, vone


def _rope_tables(n_lat, n_ctx):
    quarter = MLA_ROPE // 4
    inv_freq = ROPE_BASE ** (-jnp.arange(quarter, dtype=F32) / quarter)
    t = jnp.arange(n_lat, dtype=jnp.int32)
    row = (t // GRID_W).astype(F32)
    col = (t % GRID_W).astype(F32)
    ang = jnp.concatenate([row[:, None] * inv_freq, col[:, None] * inv_freq], axis=-1)
    cos, sin = jnp.cos(ang), jnp.sin(ang)
    ones = jnp.ones((n_lat, MLA_NOPE), F32)
    tail = jnp.ones((n_lat, HEAD_PAD - MLA_QK), F32)
    zl = jnp.zeros((n_lat, MLA_NOPE), F32)
    zt = jnp.zeros((n_lat, HEAD_PAD - MLA_QK), F32)
    c = jnp.concatenate([ones, cos, cos, tail], axis=1)
    s = jnp.concatenate([zl, sin, sin, zt], axis=1)
    c = jnp.concatenate([c, jnp.ones((n_ctx, HEAD_PAD), F32)], axis=0)
    s = jnp.concatenate([s, jnp.zeros((n_ctx, HEAD_PAD), F32)], axis=0)
    return c, s


def _mla_prep(p3, mw, tabs):
    B, Nt, _ = p3.shape
    H = MLA_HEADS
    wq, wk, wv, gcq, gckv, gq, gk, vone = mw
    n_lat_blk = Nt // TOK_BLK - 1
    const = lambda shape: pl.BlockSpec(shape, lambda b, i: (0,) * len(shape))
    tab = pl.BlockSpec((TOK_BLK, HEAD_PAD), lambda b, i: (i, 0))
    out = lambda w: pl.BlockSpec((None, H, TOK_BLK, w), lambda b, i: (b, 0, i, 0))
    shp = lambda w: jax.ShapeDtypeStruct((B, H, Nt, w), BF16)
    return pl.pallas_call(
        functools.partial(_mla_prep_kernel, n_lat_blk=n_lat_blk),
        out_shape=(shp(2 * HEAD_PAD), shp(2 * HEAD_PAD), shp(HEAD_PAD)),
        grid=(B, Nt // TOK_BLK),
        in_specs=[
            pl.BlockSpec((None, TOK_BLK, 512), lambda b, i: (b, i, COL_MLA // 512)),
            const(wq.shape), const(wk.shape), const(wv.shape), const(gcq.shape), const(gckv.shape),
            const(gq.shape), const(gk.shape), const(vone.shape), tab, tab,
        ],
        out_specs=(out(2 * HEAD_PAD), out(2 * HEAD_PAD), out(HEAD_PAD)),
        compiler_params=_cparams(("parallel", "parallel")),
        name="mla_prep",
    )(p3, wq, wk, wv, gcq, gckv, gq, gk, vone, *tabs)


def _softmax_step(q, kc, vc, m, acc):
    s = _dot_nt(q, kc)
    m_new = jnp.maximum(m, jnp.max(s, axis=-1, keepdims=True))
    p = jnp.exp2((s - m_new).astype(BF16))
    acc = jnp.exp2(m - m_new) * acc + _dot(p, vc)
    return m_new, acc


def _mla_attn_kernel(qc_ref, kc_ref, v_ref, o_ref, *, n_chunks):
    tq = qc_ref.shape[1]

    def body(c, carry):
        off = pl.multiple_of(c * MLA_TK, MLA_TK)
        return tuple(
            _softmax_step(qc_ref[hh], kc_ref[hh, pl.ds(off, MLA_TK), :], v_ref[hh, pl.ds(off, MLA_TK), :], *carry[hh])
            for hh in range(2))

    m0 = jnp.full((tq, 1), NEG_BIG, F32)
    a0 = jnp.zeros((tq, HEAD_PAD), F32)
    carry = lax.fori_loop(0, n_chunks, body, ((m0, a0), (m0, a0)), unroll=True)
    outs = [acc[:, :MLA_V] / acc[:, MLA_V:MLA_V + 1] for _, acc in carry]
    o_ref[...] = jnp.concatenate(outs, axis=-1).astype(o_ref.dtype)


def _mla_ctx_kernel(qc_ref, kc_ref, v_ref, o_ref):
    outs = []
    for hh in range(2):
        s = _dot_nt(qc_ref[hh], kc_ref[hh])
        p = jnp.exp2(s - jnp.max(s, axis=-1, keepdims=True))
        acc = _dot(p.astype(BF16), v_ref[hh])
        outs.append(acc[:, :MLA_V] / acc[:, MLA_V:MLA_V + 1])
    o_ref[...] = jnp.concatenate(outs, axis=-1).astype(o_ref.dtype)


def _mla_attention(qc, kc, v, n_lat, need_ctx):
    B, H, Nt, _ = kc.shape
    n_ctx = Nt - n_lat
    o_lat = pl.pallas_call(
        functools.partial(_mla_attn_kernel, n_chunks=Nt // MLA_TK),
        out_shape=jax.ShapeDtypeStruct((B, n_lat, BRANCH_W), BF16),
        grid=(B, H // 2, n_lat // MLA_TQ),
        in_specs=[
            pl.BlockSpec((None, 2, MLA_TQ, 2 * HEAD_PAD), lambda b, g, i: (b, g, i, 0)),
            pl.BlockSpec((None, 2, Nt, 2 * HEAD_PAD), lambda b, g, i: (b, g, 0, 0)),
            pl.BlockSpec((None, 2, Nt, HEAD_PAD), lambda b, g, i: (b, g, 0, 0)),
        ],
        out_specs=pl.BlockSpec((None, MLA_TQ, 2 * MLA_V), lambda b, g, i: (b, i, g)),
        compiler_params=_cparams(("parallel", "parallel", "arbitrary")),
        name="mla_attention",
    )(qc, kc, v)
    if not need_ctx:
        return o_lat, None
    cblk = n_lat // n_ctx
    cspec = lambda w: pl.BlockSpec((None, 2, n_ctx, w), lambda b, g: (b, g, cblk, 0))
    o_ctx = pl.pallas_call(
        _mla_ctx_kernel,
        out_shape=jax.ShapeDtypeStruct((B, n_ctx, BRANCH_W), BF16),
        grid=(B, H // 2),
        in_specs=[cspec(2 * HEAD_PAD), cspec(2 * HEAD_PAD), cspec(HEAD_PAD)],
        out_specs=pl.BlockSpec((None, n_ctx, 2 * MLA_V), lambda b, g: (b, 0, g)),
        compiler_params=_cparams(("parallel", "parallel")),
        name="mla_attention_ctx",
    )(qc, kc, v)
    return o_lat, o_ctx


NAT_ROWS_BLK = 8
NAT_TQ = NAT_ROWS_BLK * GRID_W
NAT_WIN_ROWS = 16
NAT_WIN = NAT_WIN_ROWS * GRID_W
NAT_PREP_BLK = 1024


def _pair_norm(x, g, half0):
    x2 = x * x
    s0 = jnp.sum(jnp.where(half0, x2, 0.0), axis=-1, keepdims=True)
    s1 = jnp.sum(jnp.where(half0, 0.0, x2), axis=-1, keepdims=True)
    ms = jnp.where(half0, s0, s1) * (1.0 / NAT_DH)
    return x * lax.rsqrt(ms + EPS) * g


def _nat_kernel(q_ref, kl_ref, vl_ref, kc_ref, vc_ref, bias_ref, gq_ref, gk_ref, o_ref, kn_ref, vb_ref,
                *, n_lat, rows_total):
    rb = pl.program_id(2)
    half0 = lax.broadcasted_iota(jnp.int32, (1, LANES), 1) < NAT_DH

    @pl.when(rb == 0)
    def _():
        def body(c, carry):
            off = pl.multiple_of(c * NAT_PREP_BLK, NAT_PREP_BLK)
            kn_ref[pl.ds(off, NAT_PREP_BLK), :] = _pair_norm(
                kl_ref[pl.ds(off, NAT_PREP_BLK), :].astype(F32), gk_ref[...], half0).astype(BF16)
            vb_ref[pl.ds(off, NAT_PREP_BLK), :LANES] = vl_ref[pl.ds(off, NAT_PREP_BLK), :].astype(BF16)
            vb_ref[pl.ds(off, NAT_PREP_BLK), LANES:] = jnp.ones((NAT_PREP_BLK, LANES), BF16)
            return carry

        lax.fori_loop(0, n_lat // NAT_PREP_BLK, body, 0)
        kn_ref[n_lat:, :] = _pair_norm(kc_ref[...].astype(F32), gk_ref[...], half0).astype(BF16)
        vb_ref[n_lat:, :LANES] = vc_ref[...].astype(BF16)
        vb_ref[n_lat:, LANES:] = jnp.ones((vc_ref.shape[0], LANES), BF16)

    q = _pair_norm(q_ref[...].astype(F32), gq_ref[...], half0)
    ws = jnp.clip(rb * NAT_ROWS_BLK - NAT_KH // 2, 0, rows_total - NAT_WIN_ROWS)
    off = pl.multiple_of(ws * GRID_W, GRID_W)
    kw = kn_ref[pl.ds(off, NAT_WIN), :]
    vw = vb_ref[pl.ds(off, NAT_WIN), :]
    kc = kn_ref[n_lat:, :]
    vc = vb_ref[n_lat:, :]
    qms = [jnp.where(half0 if hh == 0 else jnp.logical_not(half0), q, 0.0).astype(BF16) for hh in range(2)]
    sws = [_dot_nt(qm, kw) + bias_ref[hh].astype(F32) for hh, qm in enumerate(qms)]
    scs = [_dot_nt(qm, kc) for qm in qms]
    ms = [jnp.maximum(jnp.max(sw, axis=-1, keepdims=True), jnp.max(sc, axis=-1, keepdims=True))
          for sw, sc in zip(sws, scs)]
    accs = [_dot(jnp.exp2((sw - m).astype(BF16)), vw) + _dot(jnp.exp2((sc - m).astype(BF16)), vc)
            for sw, sc, m in zip(sws, scs, ms)]
    outs = [acc[:, :LANES] / acc[:, LANES:] for acc in accs]
    o_ref[...] = jnp.where(half0, outs[0], outs[1]).astype(o_ref.dtype)


def _nat_ctx_kernel(q_ref, kc_ref, vc_ref, gq_ref, gk_ref, o_ref):
    half0 = lax.broadcasted_iota(jnp.int32, (1, LANES), 1) < NAT_DH
    q = _pair_norm(q_ref[...].astype(F32), gq_ref[...], half0)
    kc = _pair_norm(kc_ref[...].astype(F32), gk_ref[...], half0).astype(BF16)
    vc = vc_ref[...].astype(BF16)
    outs = []
    for hh in range(2):
        sel = half0 if hh == 0 else jnp.logical_not(half0)
        s = _dot_nt(jnp.where(sel, q, 0.0).astype(BF16), kc)
        p = jnp.exp2(s - jnp.max(s, axis=-1, keepdims=True))
        outs.append(_dot(p.astype(BF16), vc) / jnp.sum(p, axis=-1, keepdims=True))
    o_ref[...] = jnp.where(half0, outs[0], outs[1]).astype(o_ref.dtype)


def _nat_bias_tables(rpb, rows_total):
    rbs = np.array([0, min(NAT_ROWS_BLK, rows_total - NAT_ROWS_BLK), rows_total - NAT_ROWS_BLK])
    ws = np.clip(rbs - NAT_KH // 2, 0, rows_total - NAT_WIN_ROWS)
    qrow = rbs[:, None] + np.arange(NAT_ROWS_BLK)[None, :]
    r0 = np.clip(qrow - NAT_KH // 2, 0, rows_total - NAT_KH)
    krow = ws[:, None] + np.arange(NAT_WIN_ROWS)[None, :]
    row_ok = (krow[:, None, :] >= r0[:, :, None]) & (krow[:, None, :] < r0[:, :, None] + NAT_KH)
    dr = np.clip(krow[:, None, :] - qrow[:, :, None] + NAT_KH - 1, 0, 2 * NAT_KH - 2)
    col = np.arange(GRID_W)
    c0 = np.clip(col - NAT_KW // 2, 0, GRID_W - NAT_KW)
    col_ok = (col[None, :] >= c0[:, None]) & (col[None, :] < c0[:, None] + NAT_KW)
    dc = np.clip(col[None, :] - col[:, None], -(NAT_KW - 1), NAT_KW - 1) + NAT_KW - 1
    oh_c = np.eye(2 * NAT_KW - 1, dtype=np.float32)[dc]
    by_col = jnp.einsum('hrc,qkc->hrqk', rpb.astype(F32), oh_c, precision=lax.Precision.HIGHEST)
    by_col = jnp.where(jnp.asarray(col_ok), by_col * LOG2E, NEG_BIG).astype(BF16)
    masked = jnp.full(by_col[:, 0].shape, NEG_BIG, BF16)
    variants = []
    for v in range(3):
        rows = [jnp.concatenate([by_col[:, dr[v, i, j]] if row_ok[v, i, j] else masked
                                 for j in range(NAT_WIN_ROWS)], axis=-1) for i in range(NAT_ROWS_BLK)]
        variants.append(jnp.concatenate(rows, axis=-2))
    return jnp.stack(variants).reshape(3, NAT_HEADS // 2, 2, NAT_TQ, NAT_WIN)


def _nat_attention(p3, rpb, g_q, g_k, n_lat, need_ctx):
    B, Nt, _ = p3.shape
    n_ctx = Nt - n_lat
    rows_total = n_lat // GRID_W
    n_rb = rows_total // NAT_ROWS_BLK
    bias = _nat_bias_tables(rpb, rows_total)
    gq = jnp.tile(g_q * (NAT_DH ** -0.5 * LOG2E), 2).reshape(1, LANES)
    gk = jnp.tile(g_k, 2).reshape(1, LANES)
    cq, ck, cv = ((COL_NAT + j * BRANCH_W) // LANES for j in range(3))
    cblk = n_lat // n_ctx

    def variant(rb):
        return jnp.where(rb == 0, 0, jnp.where(rb == n_rb - 1, 2, 1))

    vec = pl.BlockSpec((1, LANES), lambda b, g, rb: (0, 0))
    o_lat = pl.pallas_call(
        functools.partial(_nat_kernel, n_lat=n_lat, rows_total=rows_total),
        out_shape=jax.ShapeDtypeStruct((B, n_lat, BRANCH_W), BF16),
        grid=(B, NAT_HEADS // 2, n_rb),
        in_specs=[
            pl.BlockSpec((None, NAT_TQ, LANES), lambda b, g, rb: (b, rb, cq + g)),
            pl.BlockSpec((None, n_lat, LANES), lambda b, g, rb: (b, 0, ck + g)),
            pl.BlockSpec((None, n_lat, LANES), lambda b, g, rb: (b, 0, cv + g)),
            pl.BlockSpec((None, n_ctx, LANES), lambda b, g, rb: (b, cblk, ck + g)),
            pl.BlockSpec((None, n_ctx, LANES), lambda b, g, rb: (b, cblk, cv + g)),
            pl.BlockSpec((None, None, 2, NAT_TQ, NAT_WIN), lambda b, g, rb: (variant(rb), g, 0, 0, 0)),
            vec, vec,
        ],
        out_specs=pl.BlockSpec((None, NAT_TQ, LANES), lambda b, g, rb: (b, rb, g)),
        scratch_shapes=[pltpu.VMEM((Nt, LANES), BF16), pltpu.VMEM((Nt, 2 * LANES), BF16)],
        compiler_params=_cparams(("parallel", "parallel", "arbitrary")),
        name="nat_attention",
    )(p3, p3, p3, p3, p3, bias, gq, gk)
    if not need_ctx:
        return o_lat, None
    vec2 = pl.BlockSpec((1, LANES), lambda b, g: (0, 0))
    o_ctx = pl.pallas_call(
        _nat_ctx_kernel,
        out_shape=jax.ShapeDtypeStruct((B, n_ctx, BRANCH_W), BF16),
        grid=(B, NAT_HEADS // 2),
        in_specs=[
            pl.BlockSpec((None, n_ctx, LANES), lambda b, g: (b, cblk, cq + g)),
            pl.BlockSpec((None, n_ctx, LANES), lambda b, g: (b, cblk, ck + g)),
            pl.BlockSpec((None, n_ctx, LANES), lambda b, g: (b, cblk, cv + g)),
            vec2, vec2,
        ],
        out_specs=pl.BlockSpec((None, n_ctx, LANES), lambda b, g: (b, 0, g)),
        compiler_params=_cparams(("parallel", "parallel")),
        name="nat_attention_ctx",
    )(p3, p3, p3, gq, gk)
    return o_lat, o_ctx


def _log_sigmoid(x):
    return jnp.minimum(x, 0.0) - jnp.log(1.0 + jnp.exp(-jnp.abs(x)))


def _run_block(operands, s_ref, o_ref, rev):
    qs, k, v, a, heads = operands()
    T, W = qs.shape
    n_ch, nsub, nsb = T // CHUNK, CHUNK // SUB, T // SUB
    row = lax.broadcasted_iota(jnp.int32, (T, T), 0)
    col = lax.broadcasted_iota(jnp.int32, (T, T), 1)
    same_chunk = (row // CHUNK) == (col // CHUNK)
    tri = jnp.where(same_chunk & ((row <= col) if rev else (row >= col)), 1.0, 0.0).astype(BF16)
    r64 = lax.broadcasted_iota(jnp.int32, (CHUNK, CHUNK), 0)
    c64 = lax.broadcasted_iota(jnp.int32, (CHUNK, CHUNK), 1)
    causal = (r64 <= c64) if rev else (r64 >= c64)
    hi = a.astype(BF16)
    mid = (a - hi.astype(F32)).astype(BF16)
    bc = _dot(tri, hi) + _dot(tri, mid)
    bex = bc - a
    first = [SUB * s + (SUB - 1 if rev else 0) for s in range(nsb)]
    last = [SUB * s + (0 if rev else SUB - 1) for s in range(nsb)]
    r_start = [bex[r:r + 1, :] for r in first]
    r_end = [bc[r:r + 1, :] for r in last]
    bend = [r_end[nsub * c + (0 if rev else nsub - 1)] for c in range(n_ch)]

    def spread(rows):
        return jnp.concatenate([jnp.broadcast_to(r, (SUB, W)) for r in rows], axis=0)

    rs_full = spread(r_start)
    qt = qs * jnp.exp(bc - rs_full)
    q_in = qt * spread([jnp.exp(r) for r in r_start])
    ke = k * jnp.exp(spread(r_end) - bc)
    k_out = ke * spread([jnp.exp(bend[s // nsub] - r_end[s]) for s in range(nsb)])
    kd = k * jnp.exp(jnp.minimum(rs_full - bc, EXP_CAP))
    zeros = jnp.zeros((SUB, W), F32)

    def keys_for(c, i):
        rows = []
        for j in range(nsub):
            sj, si = nsub * c + j, nsub * c + i
            blk = slice(SUB * sj, SUB * (sj + 1))
            if j == i:
                rows.append(kd[blk])
            elif (j > i) if rev else (j < i):
                rows.append(ke[blk] if abs(i - j) == 1 else ke[blk] * jnp.exp(r_start[si] - r_end[sj]))
            else:
                rows.append(zeros)
        return jnp.concatenate(rows, axis=0).astype(BF16)

    kts = [[keys_for(c, i) for i in range(nsub)] for c in range(n_ch)]
    vb = v.astype(BF16)
    yield
    intra, incr, q_state = [], [], []
    for hd, (ks, kmask, vs) in enumerate(heads):
        ksl = slice(ks, ks + LANES)
        vsl = slice(vs, vs + LANES)

        def msk(t, kmask=kmask):
            return t if kmask is None else jnp.where(kmask, t, 0.0)

        qt_h = msk(qt[:, ksl]).astype(BF16)
        ko_h = msk(k_out[:, ksl]).astype(BF16)
        q_state.append(msk(q_in[:, ksl]).astype(BF16))
        intra_h, incr_h = [], []
        for c in range(n_ch):
            cs = slice(CHUNK * c, CHUNK * (c + 1))
            blocks = [_dot_nt(qt_h[SUB * (nsub * c + i):SUB * (nsub * c + i + 1), :], kts[c][i][:, ksl])
                      for i in range(nsub)]
            sc = jnp.where(causal, jnp.concatenate(blocks, axis=0), 0.0).astype(BF16)
            intra_h.append(_dot(sc, vb[cs, vsl]))
            incr_h.append(_dot(v[cs, vsl].T.astype(BF16), ko_h[cs]))
        intra.append(intra_h)
        incr.append(incr_h)
    yield
    states = [s_ref[hd] for hd in range(len(heads))]
    for c in (range(n_ch - 1, -1, -1) if rev else range(n_ch)):
        cs = slice(CHUNK * c, CHUNK * (c + 1))
        dec = jnp.exp(bend[c])
        outs = []
        for hd, (ks, _, _) in enumerate(heads):
            outs.append(intra[hd][c] + _dot_nt(q_state[hd][cs], states[hd].astype(BF16)))
            states[hd] = states[hd] * dec[:, ks:ks + LANES] + incr[hd][c]
        o_ref[cs, :] = jnp.concatenate(outs, axis=-1).astype(o_ref.dtype)
        yield
    for hd, st in enumerate(states):
        s_ref[hd] = st


def _hgrn2_operands(q_ref, f_ref, v_ref, lb):
    q = q_ref[...].astype(F32)
    f = f_ref[...].astype(F32)
    loglb, log1mlb, oneml = lb[0:1, :], lb[1:2, :], lb[2:3, :]
    u = log1mlb + _log_sigmoid(f)
    a = jnp.maximum(u, loglb) + jnp.log(1.0 + jnp.exp(-jnp.abs(u - loglb)))
    k = oneml * _sigmoid(-f)
    qs = q * _sigmoid(q) * (HG_DK ** -0.5)
    heads = [(LANES * h, None, LANES * h) for h in range(HG_HEADS)]
    return qs, k, v_ref[...].astype(F32), a, heads


def _gla_operands(q_ref, k_ref, v_ref, r_ref, w2, b2):
    g = _dot(r_ref[...].astype(BF16), w2) + b2
    a = _log_sigmoid(g) * (1.0 / GLA_NORMALIZER)
    qs = q_ref[...].astype(F32) * (GLA_DK ** -0.5)
    half0 = lax.broadcasted_iota(jnp.int32, (1, LANES), 1) < GLA_DK
    heads = [(LANES * (h // 2), half0 if h % 2 == 0 else jnp.logical_not(half0), LANES * h)
             for h in range(GLA_HEADS)]
    return qs, k_ref[...].astype(F32), v_ref[...].astype(F32), a, heads


def _scans_kernel(hq_f, hf_f, hv_f, hq_b, hf_b, hv_b, lb_ref, gq_f, gk_f, gv_f, gr_f, gq_b, gk_b, gv_b, gr_b,
                  w2_ref, b2_ref, oh_f, oh_b, og_f, og_b, s_ref):
    @pl.when(pl.program_id(1) == 0)
    def _():
        s_ref[...] = jnp.zeros_like(s_ref)

    streams = [
        _run_block(functools.partial(_hgrn2_operands, hq_f, hf_f, hv_f, lb_ref[0]), s_ref.at[0], oh_f, False),
        _run_block(functools.partial(_hgrn2_operands, hq_b, hf_b, hv_b, lb_ref[1]), s_ref.at[1], oh_b, True),
        _run_block(functools.partial(_gla_operands, gq_f, gk_f, gv_f, gr_f, w2_ref[0], b2_ref[0]),
                   s_ref.at[2], og_f, False),
        _run_block(functools.partial(_gla_operands, gq_b, gk_b, gv_b, gr_b, w2_ref[1], b2_ref[1]),
                   s_ref.at[3], og_b, True),
    ]
    done = object()
    while streams:
        streams = [g for g in streams if next(g, done) is not done]


def _linear_scans(p3, lbvec, w2pad, b2, n_lat):
    B, Nt, _ = p3.shape
    nb = Nt // TOK_BLK
    n_lat_blk = n_lat // TOK_BLK
    fwd = lambda s: jnp.where(s == 0, n_lat_blk, s - 1)
    bwd = lambda s: jnp.where(s == 0, n_lat_blk, n_lat_blk - s)
    wqk = GLA_HEADS * GLA_DK
    spec = lambda blk, w, col: pl.BlockSpec((None, TOK_BLK, w), lambda b, s: (b, blk(s), col // w))
    hg = lambda blk, f_col: [spec(blk, BRANCH_W, COL_HG), spec(blk, BRANCH_W, f_col),
                             spec(blk, BRANCH_W, COL_HG + 3 * BRANCH_W)]
    gla = lambda blk: [spec(blk, wqk, COL_GLA), spec(blk, wqk, COL_GLA + wqk), spec(blk, BRANCH_W, COL_GLA + 2 * wqk),
                       spec(blk, LANES, COL_GLR)]
    const = lambda shape: pl.BlockSpec(shape, lambda b, s: (0,) * len(shape))
    out = lambda blk: pl.BlockSpec((None, TOK_BLK, BRANCH_W), lambda b, s: (b, blk(s), 0))
    shp = jax.ShapeDtypeStruct((B, Nt, BRANCH_W), BF16)
    return pl.pallas_call(
        _scans_kernel,
        out_shape=(shp, shp, shp, shp),
        grid=(B, nb),
        in_specs=(hg(fwd, COL_HG + BRANCH_W) + hg(bwd, COL_HG + 2 * BRANCH_W) + [const(lbvec.shape)]
                  + gla(fwd) + gla(bwd) + [const(w2pad.shape), const(b2.shape)]),
        out_specs=(out(fwd), out(bwd), out(fwd), out(bwd)),
        scratch_shapes=[pltpu.VMEM((4, HG_HEADS, LANES, LANES), F32)],
        compiler_params=_cparams(("parallel", "arbitrary")),
        name="linear_scans",
    )(*([p3] * 6), lbvec, *([p3] * 8), w2pad, b2)


def _group_norm(o, g):
    parts = []
    for hd in range(o.shape[1] // LANES):
        oh = o[:, LANES * hd:LANES * (hd + 1)]
        parts.append(oh * lax.rsqrt(jnp.mean(oh * oh, axis=-1, keepdims=True) + EPS))
    return jnp.concatenate(parts, axis=-1) * g


def _merge_kernel(x_ref, h_ref, za_ref, zb_ref, zc_ref, zd_ref, ya_ref, yb_ref, hf_ref, hb_ref, gf_ref,
                  gb_ref, wbr_ref, wmg_ref, bmg_ref, wout_ref, ghg_ref, ggl_ref, m_ref, o_ref):
    h = h_ref[...]
    ys = (
        ya_ref[...].astype(F32),
        yb_ref[...].astype(F32),
        _group_norm(hf_ref[...].astype(F32) + hb_ref[...].astype(F32), ghg_ref[...]),
        _group_norm(gf_ref[...].astype(F32) + gb_ref[...].astype(F32), ggl_ref[...]),
    )
    zs = (za_ref, zb_ref, zc_ref, zd_ref)
    acc = None
    for br in range(N_BRANCH):
        gate = _sigmoid(_dot(h, wmg_ref[br]) + bmg_ref[br])
        z = zs[br][...].astype(F32)
        yz = (ys[br] * (z * _sigmoid(z))).astype(BF16)
        part = gate * _dot(yz, wbr_ref[br])
        acc = part if acc is None else acc + part
    out = _dot(acc.astype(BF16), wout_ref[...])
    o_ref[...] = x_ref[...] + m_ref[2:3, :] * out


MERGE_LAT_BLK = 512


def _merge(x_part, y_a, y_b, h, p3, scans, w_br, w_merge, b_merge, w_out, g_hg, g_gla, modsel, part, n_lat):
    B, _, D = h.shape
    tm = TOK_BLK if part else MERGE_LAT_BLK
    n_blk = 1 if part else n_lat // tm
    blk0 = n_lat // tm if part else 0
    own = lambda w: pl.BlockSpec((None, tm, w), lambda b, i: (b, i, 0))
    stream = lambda w, col=0: pl.BlockSpec((None, tm, w), lambda b, i: (b, blk0 + i, col // w))
    once = pl.Buffered(1)
    const = lambda shape: pl.BlockSpec(shape, lambda b, i: (0,) * len(shape), pipeline_mode=once)
    ghg = jnp.tile(g_hg, HG_HEADS).reshape(1, BRANCH_W)
    ggl = jnp.tile(g_gla, GLA_HEADS).reshape(1, BRANCH_W)
    bm = b_merge.reshape(N_BRANCH, 1, D)
    return pl.pallas_call(
        _merge_kernel,
        out_shape=jax.ShapeDtypeStruct((B, n_blk * tm, D), F32),
        grid=(B, n_blk),
        in_specs=[own(D), stream(D)] + [stream(BRANCH_W, COL_Z + br * BRANCH_W) for br in range(N_BRANCH)]
        + [own(BRANCH_W)] * 2 + [stream(BRANCH_W)] * 4 + [
            const(w_br.shape), const(w_merge.shape), const(bm.shape), const(w_out.shape),
            const(ghg.shape), const(ggl.shape),
            pl.BlockSpec((None, None, 3, D), lambda b, i: (b, part, 0, 0)),
        ],
        out_specs=own(D),
        compiler_params=_cparams(("parallel", "parallel"), vmem_mb=56),
        name="branch_merge_ctx" if part else "branch_merge",
    )(x_part, h, p3, p3, p3, p3, y_a, y_b, *scans, w_br, w_merge, bm, w_out, ghg, ggl, modsel)


def kernel(x, c, ctx, c_ctx, ada_w, ada_b, norm_g, w_in, mla_w_uq, mla_w_ukv, mla_g_cq, mla_g_ckv, mla_g_q,
           mla_g_k, nat_rpb, nat_g_q, nat_g_k, hg_lb_logits, hg_g_o, gla_w2, gla_b2, gla_g_o, w_br, w_merge,
           b_merge, w_out):
    B, n_lat, D = x.shape
    n_ctx = ctx.shape[1]
    depth = ada_w.shape[0]
    Nt = n_lat + n_ctx
    n_lat_blk = n_lat // TOK_BLK
    assert n_ctx == TOK_BLK and n_lat % NAT_PREP_BLK == 0 and Nt % MM_ROW_BLK == 0
    assert Nt % MLA_TK == 0 and n_lat % MLA_TQ == 0

    cc = jnp.zeros((8, D), F32).at[:B].set(c).at[B].set(c_ctx)
    ada = _ada_vectors(cc, ada_w.astype(BF16), ada_b).reshape(depth, 8, 3, D)
    sm = jax.nn.softmax(hg_lb_logits.astype(F32), axis=0)
    lower = jnp.maximum(jnp.cumsum(sm, axis=0) - sm[0], 0.0)
    tabs = _rope_tables(n_lat, n_ctx)

    x_lat, x_ctx = x, ctx
    for l in range(depth):
        need_ctx = l < depth - 1
        modsel = jnp.stack([ada[l, :B], jnp.broadcast_to(ada[l, B], (B, 3, D))], axis=1)
        x_tail = jnp.concatenate([x_lat[:, n_lat - (MM_ROW_BLK - n_ctx):n_lat], x_ctx], axis=1)
        p3, h = _norm_project(x_lat, x_tail, norm_g[l], modsel, _pack_w_in(w_in[l]), n_lat, Nt)

        mw = _mla_weights(mla_w_uq[l], mla_w_ukv[l], mla_g_cq[l], mla_g_ckv[l], mla_g_q[l], mla_g_k[l])
        y_a = _mla_attention(*_mla_prep(p3, mw, tabs), n_lat, need_ctx)
        y_b = _nat_attention(p3, nat_rpb[l], nat_g_q[l], nat_g_k[l], n_lat, need_ctx)

        lb = lower[l]
        lbvec = jnp.stack([jnp.log(lb), jnp.log1p(-lb), 1.0 - lb], axis=1)
        w2pad = jnp.zeros((2, LANES, GLA_HEADS * GLA_DK), F32)
        for d in range(2):
            w2pad = w2pad.at[d, GLA_LOWRANK * d:GLA_LOWRANK * (d + 1)].set(gla_w2[l, d])
        scans = _linear_scans(p3, lbvec, w2pad.astype(BF16), gla_b2[l].reshape(2, 1, -1), n_lat)

        wts = (w_br[l].astype(BF16), w_merge[l].astype(BF16), b_merge[l], w_out[l].astype(BF16), hg_g_o[l], gla_g_o[l])
        parts = [_merge(x_part, y_a[part], y_b[part], h, p3, scans, *wts, modsel, part, n_lat)
                 for part, x_part in enumerate((x_lat, x_ctx)) if part == 0 or need_ctx]
        x_lat, x_ctx = parts[0], (parts[1] if need_ctx else None)
    return x_lat
```

```python
import functools

import jax
import jax.numpy as jnp
import numpy as np
from jax import lax
from jax.experimental import pallas as pl
from jax.experimental.pallas import tpu as pltpu

F32 = jnp.float32
BF16 = jnp.bfloat16

GRID_W = 64
N_BRANCH = 4
BRANCH_W = 512
MLA_HEADS = 8
MLA_NOPE = 64
MLA_ROPE = 32
MLA_V = 64
MLA_QK = MLA_NOPE + MLA_ROPE
Q_LORA = 256
KV_LORA = 128
ROPE_BASE = 10000.0
NAT_HEADS = 8
NAT_DH = 64
NAT_KH = 8
NAT_KW = 16
HG_HEADS = 4
HG_DK = 128
GLA_HEADS = 4
GLA_DK = 64
GLA_LOWRANK = 16
GLA_NORMALIZER = 16.0
EPS = 1e-6

W_MLA = Q_LORA + KV_LORA + MLA_ROPE
W_NAT = 3 * BRANCH_W
W_HG = 4 * BRANCH_W
W_GLA = 2 * GLA_HEADS * GLA_DK + BRANCH_W + 2 * GLA_LOWRANK

LANES = 128
HEAD_PAD = 128
TOK_BLK = 256

COL_HG = 0
COL_Z = COL_HG + W_HG
COL_NAT = COL_Z + N_BRANCH * BRANCH_W
COL_GLA = COL_NAT + W_NAT
COL_MLA = COL_GLA + 1024
COL_GLR = COL_MLA + 512
IN_W_PAD = COL_GLR + LANES
MM_COL_BLK = IN_W_PAD // 3
MM_ROW_BLK = 768

SUB = 16
CHUNK = 64
EXP_CAP = 80.0
NEG_BIG = -1e30
LOG2E = 1.4426950408889634


def _cparams(sem, vmem_mb=48):
    return pltpu.CompilerParams(dimension_semantics=sem, vmem_limit_bytes=vmem_mb * 1024 * 1024)


def _dot(a, b):
    return jnp.dot(a, b, preferred_element_type=F32)


def _dot_nt(a, b):
    return lax.dot_general(a, b, (((1,), (1,)), ((), ())), preferred_element_type=F32)


def _sigmoid(x):
    return 0.5 * jnp.tanh(0.5 * x) + 0.5


def _ada_kernel(c_ref, w_ref, b_ref, o_ref):
    c = c_ref[...]
    a = (c * jax.nn.sigmoid(c)).astype(BF16)
    o_ref[...] = _dot(a, w_ref[...]) + b_ref[...]


def _ada_vectors(cc, ada_w, ada_b):
    L, D, D3 = ada_w.shape
    tn = 1024
    return pl.pallas_call(
        _ada_kernel,
        out_shape=jax.ShapeDtypeStruct((L, cc.shape[0], D3), F32),
        grid=(L, D3 // tn),
        in_specs=[
            pl.BlockSpec((cc.shape[0], D), lambda l, j: (0, 0)),
            pl.BlockSpec((None, D, tn), lambda l, j: (l, 0, j)),
            pl.BlockSpec((None, 1, tn), lambda l, j: (l, 0, j)),
        ],
        out_specs=pl.BlockSpec((None, cc.shape[0], tn), lambda l, j: (l, 0, j)),
        compiler_params=_cparams(("parallel", "parallel")),
        name="ada_vectors",
    )(cc, ada_w, ada_b.reshape(L, 1, D3))


def _norm_proj_kernel(x_ref, tail_ref, g_ref, m_ref, w_ref, p_ref, h_ref, hs_ref, *, n_lat):
    first_col = pl.program_id(2) == 0
    last_row_blk = pl.program_id(1) == pl.num_programs(1) - 1

    def modulated(x, scale, shift):
        y = x * lax.rsqrt(jnp.mean(x * x, axis=-1, keepdims=True) + EPS) * g_ref[...]
        h = (y * (1.0 + scale) + shift).astype(BF16)
        hs_ref[...] = h
        h_ref[...] = h

    @pl.when(first_col & jnp.logical_not(last_row_blk))
    def _():
        modulated(x_ref[...], m_ref[0, 1:2, :], m_ref[0, 0:1, :])

    @pl.when(first_col & last_row_blk)
    def _():
        rows = tail_ref.shape[0]
        row = lax.broadcasted_iota(jnp.int32, (rows, 1), 0) + pl.program_id(1) * rows
        is_ctx = row >= n_lat
        modulated(tail_ref[...], jnp.where(is_ctx, m_ref[1, 1:2, :], m_ref[0, 1:2, :]),
                  jnp.where(is_ctx, m_ref[1, 0:1, :], m_ref[0, 0:1, :]))

    p_ref[...] = _dot(hs_ref[...], w_ref[...]).astype(p_ref.dtype)


def _norm_project(x_main, x_tail, g, modsel, w, n_lat, Nt):
    B, _, D = x_main.shape
    N = w.shape[1]
    last_main = Nt // MM_ROW_BLK - 2
    return pl.pallas_call(
        functools.partial(_norm_proj_kernel, n_lat=n_lat),
        out_shape=(jax.ShapeDtypeStruct((B, Nt, N), BF16), jax.ShapeDtypeStruct((B, Nt, D), BF16)),
        grid=(B, Nt // MM_ROW_BLK, N // MM_COL_BLK),
        in_specs=[
            pl.BlockSpec((None, MM_ROW_BLK, D), lambda b, i, j: (b, jnp.minimum(i, last_main), 0)),
            pl.BlockSpec((None, MM_ROW_BLK, D), lambda b, i, j: (b, 0, 0)),
            pl.BlockSpec((1, D), lambda b, i, j: (0, 0)),
            pl.BlockSpec((None, 2, 3, D), lambda b, i, j: (b, 0, 0, 0)),
            pl.BlockSpec((D, MM_COL_BLK), lambda b, i, j: (0, j)),
        ],
        out_specs=(
            pl.BlockSpec((None, MM_ROW_BLK, MM_COL_BLK), lambda b, i, j: (b, i, j)),
            pl.BlockSpec((None, MM_ROW_BLK, D), lambda b, i, j: (b, i, 0)),
        ),
        scratch_shapes=[pltpu.VMEM((MM_ROW_BLK, D), BF16)],
        compiler_params=_cparams(("parallel", "parallel", "arbitrary")),
        name="norm_project",
    )(x_main, x_tail, g.reshape(1, D), modsel, w)


def _pack_w_in(w_in_l):
    o_nat = W_MLA
    o_hg = o_nat + W_NAT
    o_gla = o_hg + W_HG
    o_z = o_gla + W_GLA
    D = w_in_l.shape[0]
    zeros = lambda n: jnp.zeros((D, n), w_in_l.dtype)
    cols = [
        w_in_l[:, o_hg:o_hg + W_HG],
        w_in_l[:, o_z:o_z + N_BRANCH * BRANCH_W],
        w_in_l[:, o_nat:o_nat + W_NAT],
        w_in_l[:, o_gla:o_gla + 1024],
        w_in_l[:, :W_MLA], zeros(512 - W_MLA),
        w_in_l[:, o_gla + 1024:o_gla + W_GLA], zeros(LANES - 2 * GLA_LOWRANK),
    ]
    return jnp.concatenate(cols, axis=1).astype(BF16)


MLA_TQ = 1024
MLA_TK = 768


def _mla_prep_kernel(p_ref, wq_ref, wk_ref, wv_ref, gcq_ref, gckv_ref, gq_ref, gk_ref, vone_ref,
                     c_ref, s_ref, qc_ref, kc_ref, v_ref, *, n_lat_blk):
    ctx_w = jnp.where(pl.program_id(1) == n_lat_blk, 1.0, 0.0)
    hw = MLA_HEADS * HEAD_PAD
    p = p_ref[...].astype(F32)
    cq = p[:, :Q_LORA]
    r = lax.rsqrt(jnp.mean(cq * cq, axis=-1, keepdims=True) + EPS)
    cqn = (cq * r * gcq_ref[...]).astype(BF16)
    qraw = _dot(cqn, wq_ref[...])
    kv = p[:, Q_LORA:]
    ckv = kv[:, :KV_LORA]
    r2 = lax.rsqrt(jnp.mean(ckv * ckv, axis=-1, keepdims=True) + EPS)
    lane = lax.broadcasted_iota(jnp.int32, kv.shape, 1)
    lhs = (kv * jnp.where(lane < KV_LORA, r2, 1.0) * gckv_ref[...]).astype(BF16)
    kraw = _dot(lhs, wk_ref[...])
    vraw = _dot(lhs, wv_ref[...]) + vone_ref[...]
    cs = c_ref[...]
    sn = s_ref[...]
    inv = 1.0 / MLA_QK
    for hd in range(MLA_HEADS):
        sl = slice(HEAD_PAD * hd, HEAD_PAD * (hd + 1))
        sw = slice(hw + HEAD_PAD * hd, hw + HEAD_PAD * (hd + 1))
        qh = qraw[:, sl]
        rq = lax.rsqrt(jnp.sum(qh * qh, axis=-1, keepdims=True) * inv + EPS)
        qn = qh * rq * gq_ref[0:1, :]
        q_rot = qn * cs + qraw[:, sw] * rq * gq_ref[1:2, :] * sn
        qc_ref[hd] = jnp.concatenate([q_rot, qn], axis=1).astype(qc_ref.dtype)
        kh = kraw[:, sl]
        rk = lax.rsqrt(jnp.sum(kh * kh, axis=-1, keepdims=True) * inv + EPS)
        kn = kh * rk * gk_ref[0:1, :]
        k_rot = kn * cs + kraw[:, sw] * rk * gk_ref[1:2, :] * sn
        kc_ref[hd] = jnp.concatenate([k_rot * (1.0 - ctx_w), kn * ctx_w], axis=1).astype(kc_ref.dtype)
        v_ref[hd] = vraw[:, sl].astype(v_ref.dtype)


def _mla_weights(w_uq, w_ukv, g_cq, g_ckv, g_q, g_k):
    H = MLA_HEADS
    half = MLA_ROPE // 2
    lo, mid, hi = MLA_NOPE, MLA_NOPE + half, MLA_QK

    def swapped(t):
        z = jnp.zeros_like(t)
        return z.at[..., lo:mid].set(-t[..., mid:hi]).at[..., mid:hi].set(t[..., lo:mid])

    def both(t):
        return jnp.concatenate([t, swapped(t)], axis=1).reshape(t.shape[0], 2 * H * HEAD_PAD).astype(BF16)

    wq = jnp.pad(w_uq.reshape(Q_LORA, H, MLA_QK), ((0, 0), (0, 0), (0, HEAD_PAD - MLA_QK)))
    wkv = w_ukv.reshape(KV_LORA, H, MLA_NOPE + MLA_V)
    wk = jnp.pad(wkv[:, :, :MLA_NOPE], ((0, 0), (0, 0), (0, HEAD_PAD - MLA_NOPE)))
    place = jnp.zeros((MLA_ROPE, H, HEAD_PAD), F32)
    place = place.at[jnp.arange(MLA_ROPE), :, MLA_NOPE + jnp.arange(MLA_ROPE)].set(1.0)
    wk = jnp.concatenate([wk, place, jnp.zeros((256 - KV_LORA - MLA_ROPE, H, HEAD_PAD), F32)], axis=0)
    wv = jnp.pad(wkv[:, :, MLA_NOPE:], ((0, 256 - KV_LORA), (0, 0), (0, HEAD_PAD - MLA_V)))
    wv = wv.reshape(256, H * HEAD_PAD).astype(BF16)
    vone = jnp.zeros((H, HEAD_PAD), F32).at[:, MLA_V].set(1.0).reshape(1, H * HEAD_PAD)
    gckv = jnp.concatenate([g_ckv, jnp.ones((256 - KV_LORA,), F32)]).reshape(1, 256)
    gq = jnp.pad(g_q * (MLA_QK ** -0.5 * LOG2E), (0, HEAD_PAD - MLA_QK))
    gk = jnp.pad(g_k, (0, HEAD_PAD - MLA_QK))
    unsign = jnp.where(jnp.arange(HEAD_PAD) < mid, -1.0, 1.0)
    gq2 = jnp.stack([gq, swapped(gq) * unsign], axis=0)
    gk2 = jnp.stack([gk, swapped(gk) * unsign], axis=0)
    return both(wq), both(wk), wv, g_cq.reshape(1, Q_LORA), gckv, gq2, gk2, vone


def _rope_tables(n_lat, n_ctx):
    quarter = MLA_ROPE // 4
    inv_freq = ROPE_BASE ** (-jnp.arange(quarter, dtype=F32) / quarter)
    t = jnp.arange(n_lat, dtype=jnp.int32)
    row = (t // GRID_W).astype(F32)
    col = (t % GRID_W).astype(F32)
    ang = jnp.concatenate([row[:, None] * inv_freq, col[:, None] * inv_freq], axis=-1)
    cos, sin = jnp.cos(ang), jnp.sin(ang)
    ones = jnp.ones((n_lat, MLA_NOPE), F32)
    tail = jnp.ones((n_lat, HEAD_PAD - MLA_QK), F32)
    zl = jnp.zeros((n_lat, MLA_NOPE), F32)
    zt = jnp.zeros((n_lat, HEAD_PAD - MLA_QK), F32)
    c = jnp.concatenate([ones, cos, cos, tail], axis=1)
    s = jnp.concatenate([zl, sin, sin, zt], axis=1)
    c = jnp.concatenate([c, jnp.ones((n_ctx, HEAD_PAD), F32)], axis=0)
    s = jnp.concatenate([s, jnp.zeros((n_ctx, HEAD_PAD), F32)], axis=0)
    return c, s


def _mla_prep(p3, mw, tabs):
    B, Nt, _ = p3.shape
    H = MLA_HEADS
    wq, wk, wv, gcq, gckv, gq, gk, vone = mw
    n_lat_blk = Nt // TOK_BLK - 1
    const = lambda shape: pl.BlockSpec(shape, lambda b, i: (0,) * len(shape))
    tab = pl.BlockSpec((TOK_BLK, HEAD_PAD), lambda b, i: (i, 0))
    out = lambda w: pl.BlockSpec((None, H, TOK_BLK, w), lambda b, i: (b, 0, i, 0))
    shp = lambda w: jax.ShapeDtypeStruct((B, H, Nt, w), BF16)
    return pl.pallas_call(
        functools.partial(_mla_prep_kernel, n_lat_blk=n_lat_blk),
        out_shape=(shp(2 * HEAD_PAD), shp(2 * HEAD_PAD), shp(HEAD_PAD)),
        grid=(B, Nt // TOK_BLK),
        in_specs=[
            pl.BlockSpec((None, TOK_BLK, 512), lambda b, i: (b, i, COL_MLA // 512)),
            const(wq.shape), const(wk.shape), const(wv.shape), const(gcq.shape), const(gckv.shape),
            const(gq.shape), const(gk.shape), const(vone.shape), tab, tab,
        ],
        out_specs=(out(2 * HEAD_PAD), out(2 * HEAD_PAD), out(HEAD_PAD)),
        compiler_params=_cparams(("parallel", "parallel")),
        name="mla_prep",
    )(p3, wq, wk, wv, gcq, gckv, gq, gk, vone, *tabs)


def _softmax_step(q, kc, vc, m, acc):
    s = _dot_nt(q, kc)
    m_new = jnp.maximum(m, jnp.max(s, axis=-1, keepdims=True))
    p = jnp.exp2((s - m_new).astype(BF16))
    acc = jnp.exp2(m - m_new) * acc + _dot(p, vc)
    return m_new, acc


def _mla_attn_kernel(qc_ref, kc_ref, v_ref, o_ref, *, n_chunks):
    tq = qc_ref.shape[1]

    def body(c, carry):
        off = pl.multiple_of(c * MLA_TK, MLA_TK)
        return tuple(
            _softmax_step(qc_ref[hh], kc_ref[hh, pl.ds(off, MLA_TK), :], v_ref[hh, pl.ds(off, MLA_TK), :], *carry[hh])
            for hh in range(2))

    m0 = jnp.full((tq, 1), NEG_BIG, F32)
    a0 = jnp.zeros((tq, HEAD_PAD), F32)
    carry = lax.fori_loop(0, n_chunks, body, ((m0, a0), (m0, a0)), unroll=True)
    outs = [acc[:, :MLA_V] / acc[:, MLA_V:MLA_V + 1] for _, acc in carry]
    o_ref[...] = jnp.concatenate(outs, axis=-1).astype(o_ref.dtype)


def _mla_ctx_kernel(qc_ref, kc_ref, v_ref, o_ref):
    outs = []
    for hh in range(2):
        s = _dot_nt(qc_ref[hh], kc_ref[hh])
        p = jnp.exp2(s - jnp.max(s, axis=-1, keepdims=True))
        acc = _dot(p.astype(BF16), v_ref[hh])
        outs.append(acc[:, :MLA_V] / acc[:, MLA_V:MLA_V + 1])
    o_ref[...] = jnp.concatenate(outs, axis=-1).astype(o_ref.dtype)


def _mla_attention(qc, kc, v, n_lat, need_ctx):
    B, H, Nt, _ = kc.shape
    n_ctx = Nt - n_lat
    o_lat = pl.pallas_call(
        functools.partial(_mla_attn_kernel, n_chunks=Nt // MLA_TK),
        out_shape=jax.ShapeDtypeStruct((B, n_lat, BRANCH_W), BF16),
        grid=(B, H // 2, n_lat // MLA_TQ),
        in_specs=[
            pl.BlockSpec((None, 2, MLA_TQ, 2 * HEAD_PAD), lambda b, g, i: (b, g, i, 0)),
            pl.BlockSpec((None, 2, Nt, 2 * HEAD_PAD), lambda b, g, i: (b, g, 0, 0)),
            pl.BlockSpec((None, 2, Nt, HEAD_PAD), lambda b, g, i: (b, g, 0, 0)),
        ],
        out_specs=pl.BlockSpec((None, MLA_TQ, 2 * MLA_V), lambda b, g, i: (b, i, g)),
        compiler_params=_cparams(("parallel", "parallel", "arbitrary")),
        name="mla_attention",
    )(qc, kc, v)
    if not need_ctx:
        return o_lat, None
    cblk = n_lat // n_ctx
    cspec = lambda w: pl.BlockSpec((None, 2, n_ctx, w), lambda b, g: (b, g, cblk, 0))
    o_ctx = pl.pallas_call(
        _mla_ctx_kernel,
        out_shape=jax.ShapeDtypeStruct((B, n_ctx, BRANCH_W), BF16),
        grid=(B, H // 2),
        in_specs=[cspec(2 * HEAD_PAD), cspec(2 * HEAD_PAD), cspec(HEAD_PAD)],
        out_specs=pl.BlockSpec((None, n_ctx, 2 * MLA_V), lambda b, g: (b, 0, g)),
        compiler_params=_cparams(("parallel", "parallel")),
        name="mla_attention_ctx",
    )(qc, kc, v)
    return o_lat, o_ctx


NAT_ROWS_BLK = 8
NAT_TQ = NAT_ROWS_BLK * GRID_W
NAT_WIN_ROWS = 16
NAT_WIN = NAT_WIN_ROWS * GRID_W
NAT_PREP_BLK = 1024


def _pair_norm(x, g, half0):
    x2 = x * x
    s0 = jnp.sum(jnp.where(half0, x2, 0.0), axis=-1, keepdims=True)
    s1 = jnp.sum(jnp.where(half0, 0.0, x2), axis=-1, keepdims=True)
    ms = jnp.where(half0, s0, s1) * (1.0 / NAT_DH)
    return x * lax.rsqrt(ms + EPS) * g


def _nat_kernel(q_ref, kl_ref, vl_ref, kc_ref, vc_ref, bias_ref, gq_ref, gk_ref, o_ref, kn_ref, vb_ref,
                *, n_lat, rows_total):
    rb = pl.program_id(2)
    half0 = lax.broadcasted_iota(jnp.int32, (1, LANES), 1) < NAT_DH

    @pl.when(rb == 0)
    def _():
        def body(c, carry):
            off = pl.multiple_of(c * NAT_PREP_BLK, NAT_PREP_BLK)
            kn_ref[pl.ds(off, NAT_PREP_BLK), :] = _pair_norm(
                kl_ref[pl.ds(off, NAT_PREP_BLK), :].astype(F32), gk_ref[...], half0).astype(BF16)
            vb_ref[pl.ds(off, NAT_PREP_BLK), :LANES] = vl_ref[pl.ds(off, NAT_PREP_BLK), :].astype(BF16)
            vb_ref[pl.ds(off, NAT_PREP_BLK), LANES:] = jnp.ones((NAT_PREP_BLK, LANES), BF16)
            return carry

        lax.fori_loop(0, n_lat // NAT_PREP_BLK, body, 0)
        kn_ref[n_lat:, :] = _pair_norm(kc_ref[...].astype(F32), gk_ref[...], half0).astype(BF16)
        vb_ref[n_lat:, :LANES] = vc_ref[...].astype(BF16)
        vb_ref[n_lat:, LANES:] = jnp.ones((vc_ref.shape[0], LANES), BF16)

    q = _pair_norm(q_ref[...].astype(F32), gq_ref[...], half0)
    ws = jnp.clip(rb * NAT_ROWS_BLK - NAT_KH // 2, 0, rows_total - NAT_WIN_ROWS)
    off = pl.multiple_of(ws * GRID_W, GRID_W)
    kw = kn_ref[pl.ds(off, NAT_WIN), :]
    vw = vb_ref[pl.ds(off, NAT_WIN), :]
    kc = kn_ref[n_lat:, :]
    vc = vb_ref[n_lat:, :]
    qms = [jnp.where(half0 if hh == 0 else jnp.logical_not(half0), q, 0.0).astype(BF16) for hh in range(2)]
    sws = [_dot_nt(qm, kw) + bias_ref[hh].astype(F32) for hh, qm in enumerate(qms)]
    scs = [_dot_nt(qm, kc) for qm in qms]
    ms = [jnp.maximum(jnp.max(sw, axis=-1, keepdims=True), jnp.max(sc, axis=-1, keepdims=True))
          for sw, sc in zip(sws, scs)]
    accs = [_dot(jnp.exp2((sw - m).astype(BF16)), vw) + _dot(jnp.exp2((sc - m).astype(BF16)), vc)
            for sw, sc, m in zip(sws, scs, ms)]
    outs = [acc[:, :LANES] / acc[:, LANES:] for acc in accs]
    o_ref[...] = jnp.where(half0, outs[0], outs[1]).astype(o_ref.dtype)


def _nat_ctx_kernel(q_ref, kc_ref, vc_ref, gq_ref, gk_ref, o_ref):
    half0 = lax.broadcasted_iota(jnp.int32, (1, LANES), 1) < NAT_DH
    q = _pair_norm(q_ref[...].astype(F32), gq_ref[...], half0)
    kc = _pair_norm(kc_ref[...].astype(F32), gk_ref[...], half0).astype(BF16)
    vc = vc_ref[...].astype(BF16)
    outs = []
    for hh in range(2):
        sel = half0 if hh == 0 else jnp.logical_not(half0)
        s = _dot_nt(jnp.where(sel, q, 0.0).astype(BF16), kc)
        p = jnp.exp2(s - jnp.max(s, axis=-1, keepdims=True))
        outs.append(_dot(p.astype(BF16), vc) / jnp.sum(p, axis=-1, keepdims=True))
    o_ref[...] = jnp.where(half0, outs[0], outs[1]).astype(o_ref.dtype)


def _nat_bias_tables(rpb, rows_total):
    rbs = np.array([0, min(NAT_ROWS_BLK, rows_total - NAT_ROWS_BLK), rows_total - NAT_ROWS_BLK])
    ws = np.clip(rbs - NAT_KH // 2, 0, rows_total - NAT_WIN_ROWS)
    qrow = rbs[:, None] + np.arange(NAT_ROWS_BLK)[None, :]
    r0 = np.clip(qrow - NAT_KH // 2, 0, rows_total - NAT_KH)
    krow = ws[:, None] + np.arange(NAT_WIN_ROWS)[None, :]
    row_ok = (krow[:, None, :] >= r0[:, :, None]) & (krow[:, None, :] < r0[:, :, None] + NAT_KH)
    dr = np.clip(krow[:, None, :] - qrow[:, :, None] + NAT_KH - 1, 0, 2 * NAT_KH - 2)
    col = np.arange(GRID_W)
    c0 = np.clip(col - NAT_KW // 2, 0, GRID_W - NAT_KW)
    col_ok = (col[None, :] >= c0[:, None]) & (col[None, :] < c0[:, None] + NAT_KW)
    dc = np.clip(col[None, :] - col[:, None], -(NAT_KW - 1), NAT_KW - 1) + NAT_KW - 1
    oh_c = np.eye(2 * NAT_KW - 1, dtype=np.float32)[dc]
    by_col = jnp.einsum('hrc,qkc->hrqk', rpb.astype(F32), oh_c, precision=lax.Precision.HIGHEST)
    by_col = jnp.where(jnp.asarray(col_ok), by_col * LOG2E, NEG_BIG).astype(BF16)
    masked = jnp.full(by_col[:, 0].shape, NEG_BIG, BF16)
    variants = []
    for v in range(3):
        rows = [jnp.concatenate([by_col[:, dr[v, i, j]] if row_ok[v, i, j] else masked
                                 for j in range(NAT_WIN_ROWS)], axis=-1) for i in range(NAT_ROWS_BLK)]
        variants.append(jnp.concatenate(rows, axis=-2))
    return jnp.stack(variants).reshape(3, NAT_HEADS // 2, 2, NAT_TQ, NAT_WIN)


def _nat_attention(p3, rpb, g_q, g_k, n_lat, need_ctx):
    B, Nt, _ = p3.shape
    n_ctx = Nt - n_lat
    rows_total = n_lat // GRID_W
    n_rb = rows_total // NAT_ROWS_BLK
    bias = _nat_bias_tables(rpb, rows_total)
    gq = jnp.tile(g_q * (NAT_DH ** -0.5 * LOG2E), 2).reshape(1, LANES)
    gk = jnp.tile(g_k, 2).reshape(1, LANES)
    cq, ck, cv = ((COL_NAT + j * BRANCH_W) // LANES for j in range(3))
    cblk = n_lat // n_ctx

    def variant(rb):
        return jnp.where(rb == 0, 0, jnp.where(rb == n_rb - 1, 2, 1))

    vec = pl.BlockSpec((1, LANES), lambda b, g, rb: (0, 0))
    o_lat = pl.pallas_call(
        functools.partial(_nat_kernel, n_lat=n_lat, rows_total=rows_total),
        out_shape=jax.ShapeDtypeStruct((B, n_lat, BRANCH_W), BF16),
        grid=(B, NAT_HEADS // 2, n_rb),
        in_specs=[
            pl.BlockSpec((None, NAT_TQ, LANES), lambda b, g, rb: (b, rb, cq + g)),
            pl.BlockSpec((None, n_lat, LANES), lambda b, g, rb: (b, 0, ck + g)),
            pl.BlockSpec((None, n_lat, LANES), lambda b, g, rb: (b, 0, cv + g)),
            pl.BlockSpec((None, n_ctx, LANES), lambda b, g, rb: (b, cblk, ck + g)),
            pl.BlockSpec((None, n_ctx, LANES), lambda b, g, rb: (b, cblk, cv + g)),
            pl.BlockSpec((None, None, 2, NAT_TQ, NAT_WIN), lambda b, g, rb: (variant(rb), g, 0, 0, 0)),
            vec, vec,
        ],
        out_specs=pl.BlockSpec((None, NAT_TQ, LANES), lambda b, g, rb: (b, rb, g)),
        scratch_shapes=[pltpu.VMEM((Nt, LANES), BF16), pltpu.VMEM((Nt, 2 * LANES), BF16)],
        compiler_params=_cparams(("parallel", "parallel", "arbitrary")),
        name="nat_attention",
    )(p3, p3, p3, p3, p3, bias, gq, gk)
    if not need_ctx:
        return o_lat, None
    vec2 = pl.BlockSpec((1, LANES), lambda b, g: (0, 0))
    o_ctx = pl.pallas_call(
        _nat_ctx_kernel,
        out_shape=jax.ShapeDtypeStruct((B, n_ctx, BRANCH_W), BF16),
        grid=(B, NAT_HEADS // 2),
        in_specs=[
            pl.BlockSpec((None, n_ctx, LANES), lambda b, g: (b, cblk, cq + g)),
            pl.BlockSpec((None, n_ctx, LANES), lambda b, g: (b, cblk, ck + g)),
            pl.BlockSpec((None, n_ctx, LANES), lambda b, g: (b, cblk, cv + g)),
            vec2, vec2,
        ],
        out_specs=pl.BlockSpec((None, n_ctx, LANES), lambda b, g: (b, 0, g)),
        compiler_params=_cparams(("parallel", "parallel")),
        name="nat_attention_ctx",
    )(p3, p3, p3, gq, gk)
    return o_lat, o_ctx


def _log_sigmoid(x):
    return jnp.minimum(x, 0.0) - jnp.log(1.0 + jnp.exp(-jnp.abs(x)))


def _run_block(operands, s_ref, o_ref, rev):
    qs, k, v, a, heads = operands()
    T, W = qs.shape
    n_ch, nsub, nsb = T // CHUNK, CHUNK // SUB, T // SUB
    row = lax.broadcasted_iota(jnp.int32, (T, T), 0)
    col = lax.broadcasted_iota(jnp.int32, (T, T), 1)
    same_chunk = (row // CHUNK) == (col // CHUNK)
    tri = jnp.where(same_chunk & ((row <= col) if rev else (row >= col)), 1.0, 0.0).astype(BF16)
    r64 = lax.broadcasted_iota(jnp.int32, (CHUNK, CHUNK), 0)
    c64 = lax.broadcasted_iota(jnp.int32, (CHUNK, CHUNK), 1)
    causal = (r64 <= c64) if rev else (r64 >= c64)
    hi = a.astype(BF16)
    mid = (a - hi.astype(F32)).astype(BF16)
    bc = _dot(tri, hi) + _dot(tri, mid)
    bex = bc - a
    first = [SUB * s + (SUB - 1 if rev else 0) for s in range(nsb)]
    last = [SUB * s + (0 if rev else SUB - 1) for s in range(nsb)]
    r_start = [bex[r:r + 1, :] for r in first]
    r_end = [bc[r:r + 1, :] for r in last]
    bend = [r_end[nsub * c + (0 if rev else nsub - 1)] for c in range(n_ch)]

    def spread(rows):
        return jnp.concatenate([jnp.broadcast_to(r, (SUB, W)) for r in rows], axis=0)

    rs_full = spread(r_start)
    qt = qs * jnp.exp(bc - rs_full)
    q_in = qt * spread([jnp.exp(r) for r in r_start])
    ke = k * jnp.exp(spread(r_end) - bc)
    k_out = ke * spread([jnp.exp(bend[s // nsub] - r_end[s]) for s in range(nsb)])
    kd = k * jnp.exp(jnp.minimum(rs_full - bc, EXP_CAP))
    zeros = jnp.zeros((SUB, W), F32)

    def keys_for(c, i):
        rows = []
        for j in range(nsub):
            sj, si = nsub * c + j, nsub * c + i
            blk = slice(SUB * sj, SUB * (sj + 1))
            if j == i:
                rows.append(kd[blk])
            elif (j > i) if rev else (j < i):
                rows.append(ke[blk] if abs(i - j) == 1 else ke[blk] * jnp.exp(r_start[si] - r_end[sj]))
            else:
                rows.append(zeros)
        return jnp.concatenate(rows, axis=0).astype(BF16)

    kts = [[keys_for(c, i) for i in range(nsub)] for c in range(n_ch)]
    vb = v.astype(BF16)
    yield
    intra, incr, q_state = [], [], []
    for hd, (ks, kmask, vs) in enumerate(heads):
        ksl = slice(ks, ks + LANES)
        vsl = slice(vs, vs + LANES)

        def msk(t, kmask=kmask):
            return t if kmask is None else jnp.where(kmask, t, 0.0)

        qt_h = msk(qt[:, ksl]).astype(BF16)
        ko_h = msk(k_out[:, ksl]).astype(BF16)
        q_state.append(msk(q_in[:, ksl]).astype(BF16))
        intra_h, incr_h = [], []
        for c in range(n_ch):
            cs = slice(CHUNK * c, CHUNK * (c + 1))
            blocks = [_dot_nt(qt_h[SUB * (nsub * c + i):SUB * (nsub * c + i + 1), :], kts[c][i][:, ksl])
                      for i in range(nsub)]
            sc = jnp.where(causal, jnp.concatenate(blocks, axis=0), 0.0).astype(BF16)
            intra_h.append(_dot(sc, vb[cs, vsl]))
            incr_h.append(_dot(v[cs, vsl].T.astype(BF16), ko_h[cs]))
        intra.append(intra_h)
        incr.append(incr_h)
    yield
    states = [s_ref[hd] for hd in range(len(heads))]
    for c in (range(n_ch - 1, -1, -1) if rev else range(n_ch)):
        cs = slice(CHUNK * c, CHUNK * (c + 1))
        dec = jnp.exp(bend[c])
        outs = []
        for hd, (ks, _, _) in enumerate(heads):
            outs.append(intra[hd][c] + _dot_nt(q_state[hd][cs], states[hd].astype(BF16)))
            states[hd] = states[hd] * dec[:, ks:ks + LANES] + incr[hd][c]
        o_ref[cs, :] = jnp.concatenate(outs, axis=-1).astype(o_ref.dtype)
        yield
    for hd, st in enumerate(states):
        s_ref[hd] = st


def _hgrn2_operands(q_ref, f_ref, v_ref, lb):
    q = q_ref[...].astype(F32)
    f = f_ref[...].astype(F32)
    loglb, log1mlb, oneml = lb[0:1, :], lb[1:2, :], lb[2:3, :]
    u = log1mlb + _log_sigmoid(f)
    a = jnp.maximum(u, loglb) + jnp.log(1.0 + jnp.exp(-jnp.abs(u - loglb)))
    k = oneml * _sigmoid(-f)
    qs = q * _sigmoid(q) * (HG_DK ** -0.5)
    heads = [(LANES * h, None, LANES * h) for h in range(HG_HEADS)]
    return qs, k, v_ref[...].astype(F32), a, heads


def _gla_operands(q_ref, k_ref, v_ref, r_ref, w2, b2):
    g = _dot(r_ref[...].astype(BF16), w2) + b2
    a = _log_sigmoid(g) * (1.0 / GLA_NORMALIZER)
    qs = q_ref[...].astype(F32) * (GLA_DK ** -0.5)
    half0 = lax.broadcasted_iota(jnp.int32, (1, LANES), 1) < GLA_DK
    heads = [(LANES * (h // 2), half0 if h % 2 == 0 else jnp.logical_not(half0), LANES * h)
             for h in range(GLA_HEADS)]
    return qs, k_ref[...].astype(F32), v_ref[...].astype(F32), a, heads


def _scans_kernel(hq_f, hf_f, hv_f, hq_b, hf_b, hv_b, lb_ref, gq_f, gk_f, gv_f, gr_f, gq_b, gk_b, gv_b, gr_b,
                  w2_ref, b2_ref, oh_f, oh_b, og_f, og_b, s_ref):
    @pl.when(pl.program_id(1) == 0)
    def _():
        s_ref[...] = jnp.zeros_like(s_ref)

    streams = [
        _run_block(functools.partial(_hgrn2_operands, hq_f, hf_f, hv_f, lb_ref[0]), s_ref.at[0], oh_f, False),
        _run_block(functools.partial(_hgrn2_operands, hq_b, hf_b, hv_b, lb_ref[1]), s_ref.at[1], oh_b, True),
        _run_block(functools.partial(_gla_operands, gq_f, gk_f, gv_f, gr_f, w2_ref[0], b2_ref[0]),
                   s_ref.at[2], og_f, False),
        _run_block(functools.partial(_gla_operands, gq_b, gk_b, gv_b, gr_b, w2_ref[1], b2_ref[1]),
                   s_ref.at[3], og_b, True),
    ]
    done = object()
    while streams:
        streams = [g for g in streams if next(g, done) is not done]


def _linear_scans(p3, lbvec, w2pad, b2, n_lat):
    B, Nt, _ = p3.shape
    nb = Nt // TOK_BLK
    n_lat_blk = n_lat // TOK_BLK
    fwd = lambda s: jnp.where(s == 0, n_lat_blk, s - 1)
    bwd = lambda s: jnp.where(s == 0, n_lat_blk, n_lat_blk - s)
    wqk = GLA_HEADS * GLA_DK
    spec = lambda blk, w, col: pl.BlockSpec((None, TOK_BLK, w), lambda b, s: (b, blk(s), col // w))
    hg = lambda blk, f_col: [spec(blk, BRANCH_W, COL_HG), spec(blk, BRANCH_W, f_col),
                             spec(blk, BRANCH_W, COL_HG + 3 * BRANCH_W)]
    gla = lambda blk: [spec(blk, wqk, COL_GLA), spec(blk, wqk, COL_GLA + wqk), spec(blk, BRANCH_W, COL_GLA + 2 * wqk),
                       spec(blk, LANES, COL_GLR)]
    const = lambda shape: pl.BlockSpec(shape, lambda b, s: (0,) * len(shape))
    out = lambda blk: pl.BlockSpec((None, TOK_BLK, BRANCH_W), lambda b, s: (b, blk(s), 0))
    shp = jax.ShapeDtypeStruct((B, Nt, BRANCH_W), BF16)
    return pl.pallas_call(
        _scans_kernel,
        out_shape=(shp, shp, shp, shp),
        grid=(B, nb),
        in_specs=(hg(fwd, COL_HG + BRANCH_W) + hg(bwd, COL_HG + 2 * BRANCH_W) + [const(lbvec.shape)]
                  + gla(fwd) + gla(bwd) + [const(w2pad.shape), const(b2.shape)]),
        out_specs=(out(fwd), out(bwd), out(fwd), out(bwd)),
        scratch_shapes=[pltpu.VMEM((4, HG_HEADS, LANES, LANES), F32)],
        compiler_params=_cparams(("parallel", "arbitrary")),
        name="linear_scans",
    )(*([p3] * 6), lbvec, *([p3] * 8), w2pad, b2)


def _group_norm(o, g):
    parts = []
    for hd in range(o.shape[1] // LANES):
        oh = o[:, LANES * hd:LANES * (hd + 1)]
        parts.append(oh * lax.rsqrt(jnp.mean(oh * oh, axis=-1, keepdims=True) + EPS))
    return jnp.concatenate(parts, axis=-1) * g


def _merge_kernel(x_ref, h_ref, za_ref, zb_ref, zc_ref, zd_ref, ya_ref, yb_ref, hf_ref, hb_ref, gf_ref,
                  gb_ref, wbr_ref, wmg_ref, bmg_ref, wout_ref, ghg_ref, ggl_ref, m_ref, o_ref):
    h = h_ref[...]
    ys = (
        ya_ref[...].astype(F32),
        yb_ref[...].astype(F32),
        _group_norm(hf_ref[...].astype(F32) + hb_ref[...].astype(F32), ghg_ref[...]),
        _group_norm(gf_ref[...].astype(F32) + gb_ref[...].astype(F32), ggl_ref[...]),
    )
    zs = (za_ref, zb_ref, zc_ref, zd_ref)
    acc = None
    for br in range(N_BRANCH):
        gate = _sigmoid(_dot(h, wmg_ref[br]) + bmg_ref[br])
        z = zs[br][...].astype(F32)
        yz = (ys[br] * (z * _sigmoid(z))).astype(BF16)
        part = gate * _dot(yz, wbr_ref[br])
        acc = part if acc is None else acc + part
    out = _dot(acc.astype(BF16), wout_ref[...])
    o_ref[...] = x_ref[...] + m_ref[2:3, :] * out


MERGE_LAT_BLK = 512


def _merge(x_part, y_a, y_b, h, p3, scans, w_br, w_merge, b_merge, w_out, g_hg, g_gla, modsel, part, n_lat):
    B, _, D = h.shape
    tm = TOK_BLK if part else MERGE_LAT_BLK
    n_blk = 1 if part else n_lat // tm
    blk0 = n_lat // tm if part else 0
    own = lambda w: pl.BlockSpec((None, tm, w), lambda b, i: (b, i, 0))
    stream = lambda w, col=0: pl.BlockSpec((None, tm, w), lambda b, i: (b, blk0 + i, col // w))
    once = pl.Buffered(1)
    const = lambda shape: pl.BlockSpec(shape, lambda b, i: (0,) * len(shape), pipeline_mode=once)
    ghg = jnp.tile(g_hg, HG_HEADS).reshape(1, BRANCH_W)
    ggl = jnp.tile(g_gla, GLA_HEADS).reshape(1, BRANCH_W)
    bm = b_merge.reshape(N_BRANCH, 1, D)
    return pl.pallas_call(
        _merge_kernel,
        out_shape=jax.ShapeDtypeStruct((B, n_blk * tm, D), F32),
        grid=(B, n_blk),
        in_specs=[own(D), stream(D)] + [stream(BRANCH_W, COL_Z + br * BRANCH_W) for br in range(N_BRANCH)]
        + [own(BRANCH_W)] * 2 + [stream(BRANCH_W)] * 4 + [
            const(w_br.shape), const(w_merge.shape), const(bm.shape), const(w_out.shape),
            const(ghg.shape), const(ggl.shape),
            pl.BlockSpec((None, None, 3, D), lambda b, i: (b, part, 0, 0)),
        ],
        out_specs=own(D),
        compiler_params=_cparams(("parallel", "parallel"), vmem_mb=56),
        name="branch_merge_ctx" if part else "branch_merge",
    )(x_part, h, p3, p3, p3, p3, y_a, y_b, *scans, w_br, w_merge, bm, w_out, ghg, ggl, modsel)


def kernel(x, c, ctx, c_ctx, ada_w, ada_b, norm_g, w_in, mla_w_uq, mla_w_ukv, mla_g_cq, mla_g_ckv, mla_g_q,
           mla_g_k, nat_rpb, nat_g_q, nat_g_k, hg_lb_logits, hg_g_o, gla_w2, gla_b2, gla_g_o, w_br, w_merge,
           b_merge, w_out):
    B, n_lat, D = x.shape
    n_ctx = ctx.shape[1]
    depth = ada_w.shape[0]
    Nt = n_lat + n_ctx
    n_lat_blk = n_lat // TOK_BLK
    assert n_ctx == TOK_BLK and n_lat % NAT_PREP_BLK == 0 and Nt % MM_ROW_BLK == 0
    assert Nt % MLA_TK == 0 and n_lat % MLA_TQ == 0

    cc = jnp.zeros((8, D), F32).at[:B].set(c).at[B].set(c_ctx)
    ada = _ada_vectors(cc, ada_w.astype(BF16), ada_b).reshape(depth, 8, 3, D)
    sm = jax.nn.softmax(hg_lb_logits.astype(F32), axis=0)
    lower = jnp.maximum(jnp.cumsum(sm, axis=0) - sm[0], 0.0)
    tabs = _rope_tables(n_lat, n_ctx)

    x_lat, x_ctx = x, ctx
    for l in range(depth):
        need_ctx = l < depth - 1
        modsel = jnp.stack([ada[l, :B], jnp.broadcast_to(ada[l, B], (B, 3, D))], axis=1)
        x_tail = jnp.concatenate([x_lat[:, n_lat - (MM_ROW_BLK - n_ctx):n_lat], x_ctx], axis=1)
        p3, h = _norm_project(x_lat, x_tail, norm_g[l], modsel, _pack_w_in(w_in[l]), n_lat, Nt)

        mw = _mla_weights(mla_w_uq[l], mla_w_ukv[l], mla_g_cq[l], mla_g_ckv[l], mla_g_q[l], mla_g_k[l])
        y_a = _mla_attention(*_mla_prep(p3, mw, tabs), n_lat, need_ctx)
        y_b = _nat_attention(p3, nat_rpb[l], nat_g_q[l], nat_g_k[l], n_lat, need_ctx)

        lb = lower[l]
        lbvec = jnp.stack([jnp.log(lb), jnp.log1p(-lb), 1.0 - lb], axis=1)
        w2pad = jnp.zeros((2, LANES, GLA_HEADS * GLA_DK), F32)
        for d in range(2):
            w2pad = w2pad.at[d, GLA_LOWRANK * d:GLA_LOWRANK * (d + 1)].set(gla_w2[l, d])
        scans = _linear_scans(p3, lbvec, w2pad.astype(BF16), gla_b2[l].reshape(2, 1, -1), n_lat)

        wts = (w_br[l].astype(BF16), w_merge[l].astype(BF16), b_merge[l], w_out[l].astype(BF16), hg_g_o[l], gla_g_o[l])
        parts = [_merge(x_part, y_a[part], y_b[part], h, p3, scans, *wts, modsel, part, n_lat)
                 for part, x_part in enumerate((x_lat, x_ctx)) if part == 0 or need_ctx]
        x_lat, x_ctx = parts[0], (parts[1] if need_ctx else None)
    return x_lat
```

```python
import functools

import jax
import jax.numpy as jnp
import numpy as np
from jax import lax
from jax.experimental import pallas as pl
from jax.experimental.pallas import tpu as pltpu

F32 = jnp.float32
BF16 = jnp.bfloat16

GRID_W = 64
N_BRANCH = 4
BRANCH_W = 512
MLA_HEADS = 8
MLA_NOPE = 64
MLA_ROPE = 32
MLA_V = 64
MLA_QK = MLA_NOPE + MLA_ROPE
Q_LORA = 256
KV_LORA = 128
ROPE_BASE = 10000.0
NAT_HEADS = 8
NAT_DH = 64
NAT_KH = 8
NAT_KW = 16
HG_HEADS = 4
HG_DK = 128
GLA_HEADS = 4
GLA_DK = 64
GLA_LOWRANK = 16
GLA_NORMALIZER = 16.0
EPS = 1e-6

W_MLA = Q_LORA + KV_LORA + MLA_ROPE
W_NAT = 3 * BRANCH_W
W_HG = 4 * BRANCH_W
W_GLA = 2 * GLA_HEADS * GLA_DK + BRANCH_W + 2 * GLA_LOWRANK

LANES = 128
HEAD_PAD = 128
TOK_BLK = 256

COL_HG = 0
COL_Z = COL_HG + W_HG
COL_NAT = COL_Z + N_BRANCH * BRANCH_W
COL_GLA = COL_NAT + W_NAT
COL_MLA = COL_GLA + 1024
COL_GLR = COL_MLA + 512
IN_W_PAD = COL_GLR + LANES
MM_COL_BLK = IN_W_PAD // 3
MM_ROW_BLK = 768

SUB = 16
CHUNK = 64
EXP_CAP = 80.0
NEG_BIG = -1e30
LOG2E = 1.4426950408889634


def _cparams(sem, vmem_mb=48):
    return pltpu.CompilerParams(dimension_semantics=sem, vmem_limit_bytes=vmem_mb * 1024 * 1024)


def _dot(a, b):
    return jnp.dot(a, b, preferred_element_type=F32)


def _dot_nt(a, b):
    return lax.dot_general(a, b, (((1,), (1,)), ((), ())), preferred_element_type=F32)


def _sigmoid(x):
    return 0.5 * jnp.tanh(0.5 * x) + 0.5


def _ada_kernel(c_ref, w_ref, b_ref, o_ref):
    c = c_ref[...]
    a = (c * jax.nn.sigmoid(c)).astype(BF16)
    o_ref[...] = _dot(a, w_ref[...]) + b_ref[...]


def _ada_vectors(cc, ada_w, ada_b):
    L, D, D3 = ada_w.shape
    tn = 1024
    return pl.pallas_call(
        _ada_kernel,
        out_shape=jax.ShapeDtypeStruct((L, cc.shape[0], D3), F32),
        grid=(L, D3 // tn),
        in_specs=[
            pl.BlockSpec((cc.shape[0], D), lambda l, j: (0, 0)),
            pl.BlockSpec((None, D, tn), lambda l, j: (l, 0, j)),
            pl.BlockSpec((None, 1, tn), lambda l, j: (l, 0, j)),
        ],
        out_specs=pl.BlockSpec((None, cc.shape[0], tn), lambda l, j: (l, 0, j)),
        compiler_params=_cparams(("parallel", "parallel")),
        name="ada_vectors",
    )(cc, ada_w, ada_b.reshape(L, 1, D3))


def _norm_proj_kernel(x_ref, tail_ref, g_ref, m_ref, w_ref, p_ref, h_ref, hs_ref, *, n_lat):
    first_col = pl.program_id(2) == 0
    last_row_blk = pl.program_id(1) == pl.num_programs(1) - 1

    def modulated(x, scale, shift):
        y = x * lax.rsqrt(jnp.mean(x * x, axis=-1, keepdims=True) + EPS) * g_ref[...]
        h = (y * (1.0 + scale) + shift).astype(BF16)
        hs_ref[...] = h
        h_ref[...] = h

    @pl.when(first_col & jnp.logical_not(last_row_blk))
    def _():
        modulated(x_ref[...], m_ref[0, 1:2, :], m_ref[0, 0:1, :])

    @pl.when(first_col & last_row_blk)
    def _():
        rows = tail_ref.shape[0]
        row = lax.broadcasted_iota(jnp.int32, (rows, 1), 0) + pl.program_id(1) * rows
        is_ctx = row >= n_lat
        modulated(tail_ref[...], jnp.where(is_ctx, m_ref[1, 1:2, :], m_ref[0, 1:2, :]),
                  jnp.where(is_ctx, m_ref[1, 0:1, :], m_ref[0, 0:1, :]))

    p_ref[...] = _dot(hs_ref[...], w_ref[...]).astype(p_ref.dtype)


def _norm_project(x_main, x_tail, g, modsel, w, n_lat, Nt):
    B, _, D = x_main.shape
    N = w.shape[1]
    last_main = Nt // MM_ROW_BLK - 2
    return pl.pallas_call(
        functools.partial(_norm_proj_kernel, n_lat=n_lat),
        out_shape=(jax.ShapeDtypeStruct((B, Nt, N), BF16), jax.ShapeDtypeStruct((B, Nt, D), BF16)),
        grid=(B, Nt // MM_ROW_BLK, N // MM_COL_BLK),
        in_specs=[
            pl.BlockSpec((None, MM_ROW_BLK, D), lambda b, i, j: (b, jnp.minimum(i, last_main), 0)),
            pl.BlockSpec((None, MM_ROW_BLK, D), lambda b, i, j: (b, 0, 0)),
            pl.BlockSpec((1, D), lambda b, i, j: (0, 0)),
            pl.BlockSpec((None, 2, 3, D), lambda b, i, j: (b, 0, 0, 0)),
            pl.BlockSpec((D, MM_COL_BLK), lambda b, i, j: (0, j)),
        ],
        out_specs=(
            pl.BlockSpec((None, MM_ROW_BLK, MM_COL_BLK), lambda b, i, j: (b, i, j)),
            pl.BlockSpec((None, MM_ROW_BLK, D), lambda b, i, j: (b, i, 0)),
        ),
        scratch_shapes=[pltpu.VMEM((MM_ROW_BLK, D), BF16)],
        compiler_params=_cparams(("parallel", "parallel", "arbitrary")),
        name="norm_project",
    )(x_main, x_tail, g.reshape(1, D), modsel, w)


def _pack_w_in(w_in_l):
    o_nat = W_MLA
    o_hg = o_nat + W_NAT
    o_gla = o_hg + W_HG
    o_z = o_gla + W_GLA
    D = w_in_l.shape[0]
    zeros = lambda n: jnp.zeros((D, n), w_in_l.dtype)
    cols = [
        w_in_l[:, o_hg:o_hg + W_HG],
        w_in_l[:, o_z:o_z + N_BRANCH * BRANCH_W],
        w_in_l[:, o_nat:o_nat + W_NAT],
        w_in_l[:, o_gla:o_gla + 1024],
        w_in_l[:, :W_MLA], zeros(512 - W_MLA),
        w_in_l[:, o_gla + 1024:o_gla + W_GLA], zeros(LANES - 2 * GLA_LOWRANK),
    ]
    return jnp.concatenate(cols, axis=1).astype(BF16)


MLA_TQ = 1024
MLA_TK = 768


def _mla_prep_kernel(p_ref, wq_ref, wk_ref, wv_ref, gcq_ref, gckv_ref, gq_ref, gk_ref, vone_ref,
                     c_ref, s_ref, qc_ref, kc_ref, v_ref, *, n_lat_blk):
    ctx_w = jnp.where(pl.program_id(1) == n_lat_blk, 1.0, 0.0)
    hw = MLA_HEADS * HEAD_PAD
    p = p_ref[...].astype(F32)
    cq = p[:, :Q_LORA]
    r = lax.rsqrt(jnp.mean(cq * cq, axis=-1, keepdims=True) + EPS)
    cqn = (cq * r * gcq_ref[...]).astype(BF16)
    qraw = _dot(cqn, wq_ref[...])
    kv = p[:, Q_LORA:]
    ckv = kv[:, :KV_LORA]
    r2 = lax.rsqrt(jnp.mean(ckv * ckv, axis=-1, keepdims=True) + EPS)
    lane = lax.broadcasted_iota(jnp.int32, kv.shape, 1)
    lhs = (kv * jnp.where(lane < KV_LORA, r2, 1.0) * gckv_ref[...]).astype(BF16)
    kraw = _dot(lhs, wk_ref[...])
    vraw = _dot(lhs, wv_ref[...]) + vone_ref[...]
    cs = c_ref[...]
    sn = s_ref[...]
    inv = 1.0 / MLA_QK
    for hd in range(MLA_HEADS):
        sl = slice(HEAD_PAD * hd, HEAD_PAD * (hd + 1))
        sw = slice(hw + HEAD_PAD * hd, hw + HEAD_PAD * (hd + 1))
        qh = qraw[:, sl]
        rq = lax.rsqrt(jnp.sum(qh * qh, axis=-1, keepdims=True) * inv + EPS)
        qn = qh * rq * gq_ref[0:1, :]
        q_rot = qn * cs + qraw[:, sw] * rq * gq_ref[1:2, :] * sn
        qc_ref[hd] = jnp.concatenate([q_rot, qn], axis=1).astype(qc_ref.dtype)
        kh = kraw[:, sl]
        rk = lax.rsqrt(jnp.sum(kh * kh, axis=-1, keepdims=True) * inv + EPS)
        kn = kh * rk * gk_ref[0:1, :]
        k_rot = kn * cs + kraw[:, sw] * rk * gk_ref[1:2, :] * sn
        kc_ref[hd] = jnp.concatenate([k_rot * (1.0 - ctx_w), kn * ctx_w], axis=1).astype(kc_ref.dtype)
        v_ref[hd] = vraw[:, sl].astype(v_ref.dtype)


def _mla_weights(w_uq, w_ukv, g_cq, g_ckv, g_q, g_k):
    H = MLA_HEADS
    half = MLA_ROPE // 2
    lo, mid, hi = MLA_NOPE, MLA_NOPE + half, MLA_QK

    def swapped(t):
        z = jnp.zeros_like(t)
        return z.at[..., lo:mid].set(-t[..., mid:hi]).at[..., mid:hi].set(t[..., lo:mid])

    def both(t):
        return jnp.concatenate([t, swapped(t)], axis=1).reshape(t.shape[0], 2 * H * HEAD_PAD).astype(BF16)

    wq = jnp.pad(w_uq.reshape(Q_LORA, H, MLA_QK), ((0, 0), (0, 0), (0, HEAD_PAD - MLA_QK)))
    wkv = w_ukv.reshape(KV_LORA, H, MLA_NOPE + MLA_V)
    wk = jnp.pad(wkv[:, :, :MLA_NOPE], ((0, 0), (0, 0), (0, HEAD_PAD - MLA_NOPE)))
    place = jnp.zeros((MLA_ROPE, H, HEAD_PAD), F32)
    place = place.at[jnp.arange(MLA_ROPE), :, MLA_NOPE + jnp.arange(MLA_ROPE)].set(1.0)
    wk = jnp.concatenate([wk, place, jnp.zeros((256 - KV_LORA - MLA_ROPE, H, HEAD_PAD), F32)], axis=0)
    wv = jnp.pad(wkv[:, :, MLA_NOPE:], ((0, 256 - KV_LORA), (0, 0), (0, HEAD_PAD - MLA_V)))
    wv = wv.reshape(256, H * HEAD_PAD).astype(BF16)
    vone = jnp.zeros((H, HEAD_PAD), F32).at[:, MLA_V].set(1.0).reshape(1, H * HEAD_PAD)
    gckv = jnp.concatenate([g_ckv, jnp.ones((256 - KV_LORA,), F32)]).reshape(1, 256)
    gq = jnp.pad(g_q * (MLA_QK ** -0.5 * LOG2E), (0, HEAD_PAD - MLA_QK))
    gk = jnp.pad(g_k, (0, HEAD_PAD - MLA_QK))
    unsign = jnp.where(jnp.arange(HEAD_PAD) < mid, -1.0, 1.0)
    gq2 = jnp.stack([gq, swapped(gq) * unsign], axis=0)
    gk2 = jnp.stack([gk, swapped(gk) * unsign], axis=0)
    return both(wq), both(wk), wv, g_cq.reshape(1, Q_LORA), gckv, gq2, gk2, vone


def _rope_tables(n_lat, n_ctx):
    quarter = MLA_ROPE // 4
    inv_freq = ROPE_BASE ** (-jnp.arange(quarter, dtype=F32) / quarter)
    t = jnp.arange(n_lat, dtype=jnp.int32)
    row = (t // GRID_W).astype(F32)
    col = (t % GRID_W).astype(F32)
    ang = jnp.concatenate([row[:, None] * inv_freq, col[:, None] * inv_freq], axis=-1)
    cos, sin = jnp.cos(ang), jnp.sin(ang)
    ones = jnp.ones((n_lat, MLA_NOPE), F32)
    tail = jnp.ones((n_lat, HEAD_PAD - MLA_QK), F32)
    zl = jnp.zeros((n_lat, MLA_NOPE), F32)
    zt = jnp.zeros((n_lat, HEAD_PAD - MLA_QK), F32)
    c = jnp.concatenate([ones, cos, cos, tail], axis=1)
    s = jnp.concatenate([zl, sin, sin, zt], axis=1)
    c = jnp.concatenate([c, jnp.ones((n_ctx, HEAD_PAD), F32)], axis=0)
    s = jnp.concatenate([s, jnp.zeros((n_ctx, HEAD_PAD), F32)], axis=0)
    return c, s


def _mla_prep(p3, mw, tabs):
    B, Nt, _ = p3.shape
    H = MLA_HEADS
    wq, wk, wv, gcq, gckv, gq, gk, vone = mw
    n_lat_blk = Nt // TOK_BLK - 1
    const = lambda shape: pl.BlockSpec(shape, lambda b, i: (0,) * len(shape))
    tab = pl.BlockSpec((TOK_BLK, HEAD_PAD), lambda b, i: (i, 0))
    out = lambda w: pl.BlockSpec((None, H, TOK_BLK, w), lambda b, i: (b, 0, i, 0))
    shp = lambda w: jax.ShapeDtypeStruct((B, H, Nt, w), BF16)
    return pl.pallas_call(
        functools.partial(_mla_prep_kernel, n_lat_blk=n_lat_blk),
        out_shape=(shp(2 * HEAD_PAD), shp(2 * HEAD_PAD), shp(HEAD_PAD)),
        grid=(B, Nt // TOK_BLK),
        in_specs=[
            pl.BlockSpec((None, TOK_BLK, 512), lambda b, i: (b, i, COL_MLA // 512)),
            const(wq.shape), const(wk.shape), const(wv.shape), const(gcq.shape), const(gckv.shape),
            const(gq.shape), const(gk.shape), const(vone.shape), tab, tab,
        ],
        out_specs=(out(2 * HEAD_PAD), out(2 * HEAD_PAD), out(HEAD_PAD)),
        compiler_params=_cparams(("parallel", "parallel")),
        name="mla_prep",
    )(p3, wq, wk, wv, gcq, gckv, gq, gk, vone, *tabs)


def _softmax_step(q, kc, vc, m, acc):
    s = _dot_nt(q, kc)
    m_new = jnp.maximum(m, jnp.max(s, axis=-1, keepdims=True))
    p = jnp.exp2((s - m_new).astype(BF16))
    acc = jnp.exp2(m - m_new) * acc + _dot(p, vc)
    return m_new, acc


def _mla_attn_kernel(qc_ref, kc_ref, v_ref, o_ref, *, n_chunks):
    tq = qc_ref.shape[1]

    def body(c, carry):
        off = pl.multiple_of(c * MLA_TK, MLA_TK)
        return tuple(
            _softmax_step(qc_ref[hh], kc_ref[hh, pl.ds(off, MLA_TK), :], v_ref[hh, pl.ds(off, MLA_TK), :], *carry[hh])
            for hh in range(2))

    m0 = jnp.full((tq, 1), NEG_BIG, F32)
    a0 = jnp.zeros((tq, HEAD_PAD), F32)
    carry = lax.fori_loop(0, n_chunks, body, ((m0, a0), (m0, a0)), unroll=True)
    outs = [acc[:, :MLA_V] / acc[:, MLA_V:MLA_V + 1] for _, acc in carry]
    o_ref[...] = jnp.concatenate(outs, axis=-1).astype(o_ref.dtype)


def _mla_ctx_kernel(qc_ref, kc_ref, v_ref, o_ref):
    outs = []
    for hh in range(2):
        s = _dot_nt(qc_ref[hh], kc_ref[hh])
        p = jnp.exp2(s - jnp.max(s, axis=-1, keepdims=True))
        acc = _dot(p.astype(BF16), v_ref[hh])
        outs.append(acc[:, :MLA_V] / acc[:, MLA_V:MLA_V + 1])
    o_ref[...] = jnp.concatenate(outs, axis=-1).astype(o_ref.dtype)


def _mla_attention(qc, kc, v, n_lat, need_ctx):
    B, H, Nt, _ = kc.shape
    n_ctx = Nt - n_lat
    o_lat = pl.pallas_call(
        functools.partial(_mla_attn_kernel, n_chunks=Nt // MLA_TK),
        out_shape=jax.ShapeDtypeStruct((B, n_lat, BRANCH_W), BF16),
        grid=(B, H // 2, n_lat // MLA_TQ),
        in_specs=[
            pl.BlockSpec((None, 2, MLA_TQ, 2 * HEAD_PAD), lambda b, g, i: (b, g, i, 0)),
            pl.BlockSpec((None, 2, Nt, 2 * HEAD_PAD), lambda b, g, i: (b, g, 0, 0)),
            pl.BlockSpec((None, 2, Nt, HEAD_PAD), lambda b, g, i: (b, g, 0, 0)),
        ],
        out_specs=pl.BlockSpec((None, MLA_TQ, 2 * MLA_V), lambda b, g, i: (b, i, g)),
        compiler_params=_cparams(("parallel", "parallel", "arbitrary")),
        name="mla_attention",
    )(qc, kc, v)
    if not need_ctx:
        return o_lat, None
    cblk = n_lat // n_ctx
    cspec = lambda w: pl.BlockSpec((None, 2, n_ctx, w), lambda b, g: (b, g, cblk, 0))
    o_ctx = pl.pallas_call(
        _mla_ctx_kernel,
        out_shape=jax.ShapeDtypeStruct((B, n_ctx, BRANCH_W), BF16),
        grid=(B, H // 2),
        in_specs=[cspec(2 * HEAD_PAD), cspec(2 * HEAD_PAD), cspec(HEAD_PAD)],
        out_specs=pl.BlockSpec((None, n_ctx, 2 * MLA_V), lambda b, g: (b, 0, g)),
        compiler_params=_cparams(("parallel", "parallel")),
        name="mla_attention_ctx",
    )(qc, kc, v)
    return o_lat, o_ctx


NAT_ROWS_BLK = 8
NAT_TQ = NAT_ROWS_BLK * GRID_W
NAT_WIN_ROWS = 16
NAT_WIN = NAT_WIN_ROWS * GRID_W
NAT_PREP_BLK = 1024


def _pair_norm(x, g, half0):
    x2 = x * x
    s0 = jnp.sum(jnp.where(half0, x2, 0.0), axis=-1, keepdims=True)
    s1 = jnp.sum(jnp.where(half0, 0.0, x2), axis=-1, keepdims=True)
    ms = jnp.where(half0, s0, s1) * (1.0 / NAT_DH)
    return x * lax.rsqrt(ms + EPS) * g


def _nat_kernel(q_ref, kl_ref, vl_ref, kc_ref, vc_ref, bias_ref, gq_ref, gk_ref, o_ref, kn_ref, vb_ref,
                *, n_lat, rows_total):
    rb = pl.program_id(2)
    half0 = lax.broadcasted_iota(jnp.int32, (1, LANES), 1) < NAT_DH

    @pl.when(rb == 0)
    def _():
        def body(c, carry):
            off = pl.multiple_of(c * NAT_PREP_BLK, NAT_PREP_BLK)
            kn_ref[pl.ds(off, NAT_PREP_BLK), :] = _pair_norm(
                kl_ref[pl.ds(off, NAT_PREP_BLK), :].astype(F32), gk_ref[...], half0).astype(BF16)
            vb_ref[pl.ds(off, NAT_PREP_BLK), :LANES] = vl_ref[pl.ds(off, NAT_PREP_BLK), :].astype(BF16)
            vb_ref[pl.ds(off, NAT_PREP_BLK), LANES:] = jnp.ones((NAT_PREP_BLK, LANES), BF16)
            return carry

        lax.fori_loop(0, n_lat // NAT_PREP_BLK, body, 0)
        kn_ref[n_lat:, :] = _pair_norm(kc_ref[...].astype(F32), gk_ref[...], half0).astype(BF16)
        vb_ref[n_lat:, :LANES] = vc_ref[...].astype(BF16)
        vb_ref[n_lat:, LANES:] = jnp.ones((vc_ref.shape[0], LANES), BF16)

    q = _pair_norm(q_ref[...].astype(F32), gq_ref[...], half0)
    ws = jnp.clip(rb * NAT_ROWS_BLK - NAT_KH // 2, 0, rows_total - NAT_WIN_ROWS)
    off = pl.multiple_of(ws * GRID_W, GRID_W)
    kw = kn_ref[pl.ds(off, NAT_WIN), :]
    vw = vb_ref[pl.ds(off, NAT_WIN), :]
    kc = kn_ref[n_lat:, :]
    vc = vb_ref[n_lat:, :]
    qms = [jnp.where(half0 if hh == 0 else jnp.logical_not(half0), q, 0.0).astype(BF16) for hh in range(2)]
    sws = [_dot_nt(qm, kw) + bias_ref[hh].astype(F32) for hh, qm in enumerate(qms)]
    scs = [_dot_nt(qm, kc) for qm in qms]
    ms = [jnp.maximum(jnp.max(sw, axis=-1, keepdims=True), jnp.max(sc, axis=-1, keepdims=True))
          for sw, sc in zip(sws, scs)]
    accs = [_dot(jnp.exp2((sw - m).astype(BF16)), vw) + _dot(jnp.exp2((sc - m).astype(BF16)), vc)
            for sw, sc, m in zip(sws, scs, ms)]
    outs = [acc[:, :LANES] / acc[:, LANES:] for acc in accs]
    o_ref[...] = jnp.where(half0, outs[0], outs[1]).astype(o_ref.dtype)


def _nat_ctx_kernel(q_ref, kc_ref, vc_ref, gq_ref, gk_ref, o_ref):
    half0 = lax.broadcasted_iota(jnp.int32, (1, LANES), 1) < NAT_DH
    q = _pair_norm(q_ref[...].astype(F32), gq_ref[...], half0)
    kc = _pair_norm(kc_ref[...].astype(F32), gk_ref[...], half0).astype(BF16)
    vc = vc_ref[...].astype(BF16)
    outs = []
    for hh in range(2):
        sel = half0 if hh == 0 else jnp.logical_not(half0)
        s = _dot_nt(jnp.where(sel, q, 0.0).astype(BF16), kc)
        p = jnp.exp2(s - jnp.max(s, axis=-1, keepdims=True))
        outs.append(_dot(p.astype(BF16), vc) / jnp.sum(p, axis=-1, keepdims=True))
    o_ref[...] = jnp.where(half0, outs[0], outs[1]).astype(o_ref.dtype)


def _nat_bias_tables(rpb, rows_total):
    rbs = np.array([0, min(NAT_ROWS_BLK, rows_total - NAT_ROWS_BLK), rows_total - NAT_ROWS_BLK])
    ws = np.clip(rbs - NAT_KH // 2, 0, rows_total - NAT_WIN_ROWS)
    qrow = rbs[:, None] + np.arange(NAT_ROWS_BLK)[None, :]
    r0 = np.clip(qrow - NAT_KH // 2, 0, rows_total - NAT_KH)
    krow = ws[:, None] + np.arange(NAT_WIN_ROWS)[None, :]
    row_ok = (krow[:, None, :] >= r0[:, :, None]) & (krow[:, None, :] < r0[:, :, None] + NAT_KH)
    dr = np.clip(krow[:, None, :] - qrow[:, :, None] + NAT_KH - 1, 0, 2 * NAT_KH - 2)
    col = np.arange(GRID_W)
    c0 = np.clip(col - NAT_KW // 2, 0, GRID_W - NAT_KW)
    col_ok = (col[None, :] >= c0[:, None]) & (col[None, :] < c0[:, None] + NAT_KW)
    dc = np.clip(col[None, :] - col[:, None], -(NAT_KW - 1), NAT_KW - 1) + NAT_KW - 1
    oh_c = np.eye(2 * NAT_KW - 1, dtype=np.float32)[dc]
    by_col = jnp.einsum('hrc,qkc->hrqk', rpb.astype(F32), oh_c, precision=lax.Precision.HIGHEST)
    by_col = jnp.where(jnp.asarray(col_ok), by_col * LOG2E, NEG_BIG).astype(BF16)
    masked = jnp.full(by_col[:, 0].shape, NEG_BIG, BF16)
    variants = []
    for v in range(3):
        rows = [jnp.concatenate([by_col[:, dr[v, i, j]] if row_ok[v, i, j] else masked
                                 for j in range(NAT_WIN_ROWS)], axis=-1) for i in range(NAT_ROWS_BLK)]
        variants.append(jnp.concatenate(rows, axis=-2))
    return jnp.stack(variants).reshape(3, NAT_HEADS // 2, 2, NAT_TQ, NAT_WIN)


def _nat_attention(p3, rpb, g_q, g_k, n_lat, need_ctx):
    B, Nt, _ = p3.shape
    n_ctx = Nt - n_lat
    rows_total = n_lat // GRID_W
    n_rb = rows_total // NAT_ROWS_BLK
    bias = _nat_bias_tables(rpb, rows_total)
    gq = jnp.tile(g_q * (NAT_DH ** -0.5 * LOG2E), 2).reshape(1, LANES)
    gk = jnp.tile(g_k, 2).reshape(1, LANES)
    cq, ck, cv = ((COL_NAT + j * BRANCH_W) // LANES for j in range(3))
    cblk = n_lat // n_ctx

    def variant(rb):
        return jnp.where(rb == 0, 0, jnp.where(rb == n_rb - 1, 2, 1))

    vec = pl.BlockSpec((1, LANES), lambda b, g, rb: (0, 0))
    o_lat = pl.pallas_call(
        functools.partial(_nat_kernel, n_lat=n_lat, rows_total=rows_total),
        out_shape=jax.ShapeDtypeStruct((B, n_lat, BRANCH_W), BF16),
        grid=(B, NAT_HEADS // 2, n_rb),
        in_specs=[
            pl.BlockSpec((None, NAT_TQ, LANES), lambda b, g, rb: (b, rb, cq + g)),
            pl.BlockSpec((None, n_lat, LANES), lambda b, g, rb: (b, 0, ck + g)),
            pl.BlockSpec((None, n_lat, LANES), lambda b, g, rb: (b, 0, cv + g)),
            pl.BlockSpec((None, n_ctx, LANES), lambda b, g, rb: (b, cblk, ck + g)),
            pl.BlockSpec((None, n_ctx, LANES), lambda b, g, rb: (b, cblk, cv + g)),
            pl.BlockSpec((None, None, 2, NAT_TQ, NAT_WIN), lambda b, g, rb: (variant(rb), g, 0, 0, 0)),
            vec, vec,
        ],
        out_specs=pl.BlockSpec((None, NAT_TQ, LANES), lambda b, g, rb: (b, rb, g)),
        scratch_shapes=[pltpu.VMEM((Nt, LANES), BF16), pltpu.VMEM((Nt, 2 * LANES), BF16)],
        compiler_params=_cparams(("parallel", "parallel", "arbitrary")),
        name="nat_attention",
    )(p3, p3, p3, p3, p3, bias, gq, gk)
    if not need_ctx:
        return o_lat, None
    vec2 = pl.BlockSpec((1, LANES), lambda b, g: (0, 0))
    o_ctx = pl.pallas_call(
        _nat_ctx_kernel,
        out_shape=jax.ShapeDtypeStruct((B, n_ctx, BRANCH_W), BF16),
        grid=(B, NAT_HEADS // 2),
        in_specs=[
            pl.BlockSpec((None, n_ctx, LANES), lambda b, g: (b, cblk, cq + g)),
            pl.BlockSpec((None, n_ctx, LANES), lambda b, g: (b, cblk, ck + g)),
            pl.BlockSpec((None, n_ctx, LANES), lambda b, g: (b, cblk, cv + g)),
            vec2, vec2,
        ],
        out_specs=pl.BlockSpec((None, n_ctx, LANES), lambda b, g: (b, 0, g)),
        compiler_params=_cparams(("parallel", "parallel")),
        name="nat_attention_ctx",
    )(p3, p3, p3, gq, gk)
    return o_lat, o_ctx


def _log_sigmoid(x):
    return jnp.minimum(x, 0.0) - jnp.log(1.0 + jnp.exp(-jnp.abs(x)))


def _run_block(operands, s_ref, o_ref, rev):
    n_ch, nsub = o_ref.shape[0] // CHUNK, CHUNK // SUB
    r64 = lax.broadcasted_iota(jnp.int32, (CHUNK, CHUNK), 0)
    c64 = lax.broadcasted_iota(jnp.int32, (CHUNK, CHUNK), 1)
    causal = (r64 <= c64) if rev else (r64 >= c64)
    tri = jnp.where(causal, 1.0, 0.0).astype(BF16)
    first = [SUB * i + (SUB - 1 if rev else 0) for i in range(nsub)]
    last = [SUB * i + (0 if rev else SUB - 1) for i in range(nsub)]
    loaded = []
    for c in range(n_ch):
        qs, k, v, a, heads = operands(slice(CHUNK * c, CHUNK * (c + 1)))
        hi = a.astype(BF16)
        mid = (a - hi.astype(F32)).astype(BF16)
        loaded.append((qs, k, v, a, _dot(tri, hi) + _dot(tri, mid)))
        yield
    staged = []
    for c in range(n_ch):
        qs, k, v, a, bc = loaded[c]
        W = qs.shape[1]
        bex = bc - a
        r_start = [bex[r:r + 1, :] for r in first]
        r_end = [bc[r:r + 1, :] for r in last]
        bend = r_end[0] if rev else r_end[nsub - 1]

        def spread(rows):
            return jnp.concatenate([jnp.broadcast_to(r, (SUB, W)) for r in rows], axis=0)

        rs_full = spread(r_start)
        qt = qs * jnp.exp(bc - rs_full)
        q_in = qt * spread([jnp.exp(r) for r in r_start])
        ke = k * jnp.exp(spread(r_end) - bc)
        k_out = ke * spread([jnp.exp(bend - r) for r in r_end])
        kd = k * jnp.exp(jnp.minimum(rs_full - bc, EXP_CAP))
        zeros = jnp.zeros((SUB, W), F32)
        kts = []
        for i in range(nsub):
            rows = []
            for j in range(nsub):
                blk = slice(SUB * j, SUB * (j + 1))
                if j == i:
                    rows.append(kd[blk])
                elif (j > i) if rev else (j < i):
                    rows.append(ke[blk] if abs(i - j) == 1 else ke[blk] * jnp.exp(r_start[i] - r_end[j]))
                else:
                    rows.append(zeros)
            kts.append(jnp.concatenate(rows, axis=0).astype(BF16))
        vb = v.astype(BF16)
        intra, incr, q_state = [], [], []
        for ks, kmask, vs in heads:
            ksl = slice(ks, ks + LANES)
            vsl = slice(vs, vs + LANES)

            def msk(t, kmask=kmask):
                return t if kmask is None else jnp.where(kmask, t, 0.0)

            qt_h = msk(qt[:, ksl]).astype(BF16)
            blocks = [_dot_nt(qt_h[SUB * i:SUB * (i + 1), :], kts[i][:, ksl]) for i in range(nsub)]
            sc = jnp.where(causal, jnp.concatenate(blocks, axis=0), 0.0).astype(BF16)
            intra.append(_dot(sc, vb[:, vsl]))
            incr.append(_dot(v[:, vsl].T.astype(BF16), msk(k_out[:, ksl]).astype(BF16)))
            q_state.append(msk(q_in[:, ksl]).astype(BF16))
        staged.append((intra, incr, q_state, jnp.exp(bend)))
        yield
    states = [s_ref[hd] for hd in range(len(heads))]
    for c in (range(n_ch - 1, -1, -1) if rev else range(n_ch)):
        intra, incr, q_state, dec = staged[c]
        outs = []
        for hd, (ks, _, _) in enumerate(heads):
            outs.append(intra[hd] + _dot_nt(q_state[hd], states[hd].astype(BF16)))
            states[hd] = states[hd] * dec[:, ks:ks + LANES] + incr[hd]
        o_ref[CHUNK * c:CHUNK * (c + 1), :] = jnp.concatenate(outs, axis=-1).astype(o_ref.dtype)
        yield
    for hd, st in enumerate(states):
        s_ref[hd] = st


def _hgrn2_operands(q_ref, f_ref, v_ref, lb, rows):
    q = q_ref[rows, :].astype(F32)
    f = f_ref[rows, :].astype(F32)
    loglb, log1mlb, oneml = lb[0:1, :], lb[1:2, :], lb[2:3, :]
    u = log1mlb + _log_sigmoid(f)
    a = jnp.maximum(u, loglb) + jnp.log(1.0 + jnp.exp(-jnp.abs(u - loglb)))
    k = oneml * _sigmoid(-f)
    qs = q * _sigmoid(q) * (HG_DK ** -0.5)
    heads = [(LANES * h, None, LANES * h) for h in range(HG_HEADS)]
    return qs, k, v_ref[rows, :].astype(F32), a, heads


def _gla_operands(q_ref, k_ref, v_ref, r_ref, w2, b2, rows):
    g = _dot(r_ref[rows, :].astype(BF16), w2) + b2
    a = _log_sigmoid(g) * (1.0 / GLA_NORMALIZER)
    qs = q_ref[rows, :].astype(F32) * (GLA_DK ** -0.5)
    half0 = lax.broadcasted_iota(jnp.int32, (1, LANES), 1) < GLA_DK
    heads = [(LANES * (h // 2), half0 if h % 2 == 0 else jnp.logical_not(half0), LANES * h)
             for h in range(GLA_HEADS)]
    return qs, k_ref[rows, :].astype(F32), v_ref[rows, :].astype(F32), a, heads


def _scans_kernel(hq_f, hf_f, hv_f, hq_b, hf_b, hv_b, lb_ref, gq_f, gk_f, gv_f, gr_f, gq_b, gk_b, gv_b, gr_b,
                  w2_ref, b2_ref, oh_f, oh_b, og_f, og_b, s_ref):
    @pl.when(pl.program_id(1) == 0)
    def _():
        s_ref[...] = jnp.zeros_like(s_ref)

    streams = [
        _run_block(functools.partial(_hgrn2_operands, hq_f, hf_f, hv_f, lb_ref[0]), s_ref.at[0], oh_f, False),
        _run_block(functools.partial(_hgrn2_operands, hq_b, hf_b, hv_b, lb_ref[1]), s_ref.at[1], oh_b, True),
        _run_block(functools.partial(_gla_operands, gq_f, gk_f, gv_f, gr_f, w2_ref[0], b2_ref[0]),
                   s_ref.at[2], og_f, False),
        _run_block(functools.partial(_gla_operands, gq_b, gk_b, gv_b, gr_b, w2_ref[1], b2_ref[1]),
                   s_ref.at[3], og_b, True),
    ]
    done = object()
    while streams:
        streams = [g for g in streams if next(g, done) is not done]


def _linear_scans(p3, lbvec, w2pad, b2, n_lat):
    B, Nt, _ = p3.shape
    nb = Nt // TOK_BLK
    n_lat_blk = n_lat // TOK_BLK
    fwd = lambda s: jnp.where(s == 0, n_lat_blk, s - 1)
    bwd = lambda s: jnp.where(s == 0, n_lat_blk, n_lat_blk - s)
    wqk = GLA_HEADS * GLA_DK
    spec = lambda blk, w, col: pl.BlockSpec((None, TOK_BLK, w), lambda b, s: (b, blk(s), col // w))
    hg = lambda blk, f_col: [spec(blk, BRANCH_W, COL_HG), spec(blk, BRANCH_W, f_col),
                             spec(blk, BRANCH_W, COL_HG + 3 * BRANCH_W)]
    gla = lambda blk: [spec(blk, wqk, COL_GLA), spec(blk, wqk, COL_GLA + wqk), spec(blk, BRANCH_W, COL_GLA + 2 * wqk),
                       spec(blk, LANES, COL_GLR)]
    const = lambda shape: pl.BlockSpec(shape, lambda b, s: (0,) * len(shape))
    out = lambda blk: pl.BlockSpec((None, TOK_BLK, BRANCH_W), lambda b, s: (b, blk(s), 0))
    shp = jax.ShapeDtypeStruct((B, Nt, BRANCH_W), BF16)
    return pl.pallas_call(
        _scans_kernel,
        out_shape=(shp, shp, shp, shp),
        grid=(B, nb),
        in_specs=(hg(fwd, COL_HG + BRANCH_W) + hg(bwd, COL_HG + 2 * BRANCH_W) + [const(lbvec.shape)]
                  + gla(fwd) + gla(bwd) + [const(w2pad.shape), const(b2.shape)]),
        out_specs=(out(fwd), out(bwd), out(fwd), out(bwd)),
        scratch_shapes=[pltpu.VMEM((4, HG_HEADS, LANES, LANES), F32)],
        compiler_params=_cparams(("parallel", "arbitrary")),
        name="linear_scans",
    )(*([p3] * 6), lbvec, *([p3] * 8), w2pad, b2)


def _group_norm(o, g):
    parts = []
    for hd in range(o.shape[1] // LANES):
        oh = o[:, LANES * hd:LANES * (hd + 1)]
        parts.append(oh * lax.rsqrt(jnp.mean(oh * oh, axis=-1, keepdims=True) + EPS))
    return jnp.concatenate(parts, axis=-1) * g


def _merge_kernel(x_ref, h_ref, za_ref, zb_ref, zc_ref, zd_ref, ya_ref, yb_ref, hf_ref, hb_ref, gf_ref,
                  gb_ref, wbr_ref, wmg_ref, bmg_ref, wout_ref, ghg_ref, ggl_ref, m_ref, o_ref):
    h = h_ref[...]
    ys = (
        ya_ref[...].astype(F32),
        yb_ref[...].astype(F32),
        _group_norm(hf_ref[...].astype(F32) + hb_ref[...].astype(F32), ghg_ref[...]),
        _group_norm(gf_ref[...].astype(F32) + gb_ref[...].astype(F32), ggl_ref[...]),
    )
    zs = (za_ref, zb_ref, zc_ref, zd_ref)
    acc = None
    for br in range(N_BRANCH):
        gate = _sigmoid(_dot(h, wmg_ref[br]) + bmg_ref[br])
        z = zs[br][...].astype(F32)
        yz = (ys[br] * (z * _sigmoid(z))).astype(BF16)
        part = gate * _dot(yz, wbr_ref[br])
        acc = part if acc is None else acc + part
    out = _dot(acc.astype(BF16), wout_ref[...])
    o_ref[...] = x_ref[...] + m_ref[2:3, :] * out


MERGE_LAT_BLK = 512


def _merge(x_part, y_a, y_b, h, p3, scans, w_br, w_merge, b_merge, w_out, g_hg, g_gla, modsel, part, n_lat):
    B, _, D = h.shape
    tm = TOK_BLK if part else MERGE_LAT_BLK
    n_blk = 1 if part else n_lat // tm
    blk0 = n_lat // tm if part else 0
    own = lambda w: pl.BlockSpec((None, tm, w), lambda b, i: (b, i, 0))
    stream = lambda w, col=0: pl.BlockSpec((None, tm, w), lambda b, i: (b, blk0 + i, col // w))
    once = pl.Buffered(1)
    const = lambda shape: pl.BlockSpec(shape, lambda b, i: (0,) * len(shape), pipeline_mode=once)
    ghg = jnp.tile(g_hg, HG_HEADS).reshape(1, BRANCH_W)
    ggl = jnp.tile(g_gla, GLA_HEADS).reshape(1, BRANCH_W)
    bm = b_merge.reshape(N_BRANCH, 1, D)
    return pl.pallas_call(
        _merge_kernel,
        out_shape=jax.ShapeDtypeStruct((B, n_blk * tm, D), F32),
        grid=(B, n_blk),
        in_specs=[own(D), stream(D)] + [stream(BRANCH_W, COL_Z + br * BRANCH_W) for br in range(N_BRANCH)]
        + [own(BRANCH_W)] * 2 + [stream(BRANCH_W)] * 4 + [
            const(w_br.shape), const(w_merge.shape), const(bm.shape), const(w_out.shape),
            const(ghg.shape), const(ggl.shape),
            pl.BlockSpec((None, None, 3, D), lambda b, i: (b, part, 0, 0)),
        ],
        out_specs=own(D),
        compiler_params=_cparams(("parallel", "parallel"), vmem_mb=56),
        name="branch_merge_ctx" if part else "branch_merge",
    )(x_part, h, p3, p3, p3, p3, y_a, y_b, *scans, w_br, w_merge, bm, w_out, ghg, ggl, modsel)


def kernel(x, c, ctx, c_ctx, ada_w, ada_b, norm_g, w_in, mla_w_uq, mla_w_ukv, mla_g_cq, mla_g_ckv, mla_g_q,
           mla_g_k, nat_rpb, nat_g_q, nat_g_k, hg_lb_logits, hg_g_o, gla_w2, gla_b2, gla_g_o, w_br, w_merge,
           b_merge, w_out):
    B, n_lat, D = x.shape
    n_ctx = ctx.shape[1]
    depth = ada_w.shape[0]
    Nt = n_lat + n_ctx
    n_lat_blk = n_lat // TOK_BLK
    assert n_ctx == TOK_BLK and n_lat % NAT_PREP_BLK == 0 and Nt % MM_ROW_BLK == 0
    assert Nt % MLA_TK == 0 and n_lat % MLA_TQ == 0

    cc = jnp.zeros((8, D), F32).at[:B].set(c).at[B].set(c_ctx)
    ada = _ada_vectors(cc, ada_w.astype(BF16), ada_b).reshape(depth, 8, 3, D)
    sm = jax.nn.softmax(hg_lb_logits.astype(F32), axis=0)
    lower = jnp.maximum(jnp.cumsum(sm, axis=0) - sm[0], 0.0)
    tabs = _rope_tables(n_lat, n_ctx)

    x_lat, x_ctx = x, ctx
    for l in range(depth):
        need_ctx = l < depth - 1
        modsel = jnp.stack([ada[l, :B], jnp.broadcast_to(ada[l, B], (B, 3, D))], axis=1)
        x_tail = jnp.concatenate([x_lat[:, n_lat - (MM_ROW_BLK - n_ctx):n_lat], x_ctx], axis=1)
        p3, h = _norm_project(x_lat, x_tail, norm_g[l], modsel, _pack_w_in(w_in[l]), n_lat, Nt)

        mw = _mla_weights(mla_w_uq[l], mla_w_ukv[l], mla_g_cq[l], mla_g_ckv[l], mla_g_q[l], mla_g_k[l])
        y_a = _mla_attention(*_mla_prep(p3, mw, tabs), n_lat, need_ctx)
        y_b = _nat_attention(p3, nat_rpb[l], nat_g_q[l], nat_g_k[l], n_lat, need_ctx)

        lb = lower[l]
        lbvec = jnp.stack([jnp.log(lb), jnp.log1p(-lb), 1.0 - lb], axis=1)
        w2pad = jnp.zeros((2, LANES, GLA_HEADS * GLA_DK), F32)
        for d in range(2):
            w2pad = w2pad.at[d, GLA_LOWRANK * d:GLA_LOWRANK * (d + 1)].set(gla_w2[l, d])
        scans = _linear_scans(p3, lbvec, w2pad.astype(BF16), gla_b2[l].reshape(2, 1, -1), n_lat)

        wts = (w_br[l].astype(BF16), w_merge[l].astype(BF16), b_merge[l], w_out[l].astype(BF16), hg_g_o[l], gla_g_o[l])
        parts = [_merge(x_part, y_a[part], y_b[part], h, p3, scans, *wts, modsel, part, n_lat)
                 for part, x_part in enumerate((x_lat, x_ctx)) if part == 0 or need_ctx]
        x_lat, x_ctx = parts[0], (parts[1] if need_ctx else None)
    return x_lat
```

```python
import functools

import jax
import jax.numpy as jnp
import numpy as np
from jax import lax
from jax.experimental import pallas as pl
from jax.experimental.pallas import tpu as pltpu

F32 = jnp.float32
BF16 = jnp.bfloat16

GRID_W = 64
N_BRANCH = 4
BRANCH_W = 512
MLA_HEADS = 8
MLA_NOPE = 64
MLA_ROPE = 32
MLA_V = 64
MLA_QK = MLA_NOPE + MLA_ROPE
Q_LORA = 256
KV_LORA = 128
ROPE_BASE = 10000.0
NAT_HEADS = 8
NAT_DH = 64
NAT_KH = 8
NAT_KW = 16
HG_HEADS = 4
HG_DK = 128
GLA_HEADS = 4
GLA_DK = 64
GLA_LOWRANK = 16
GLA_NORMALIZER = 16.0
EPS = 1e-6

W_MLA = Q_LORA + KV_LORA + MLA_ROPE
W_NAT = 3 * BRANCH_W
W_HG = 4 * BRANCH_W
W_GLA = 2 * GLA_HEADS * GLA_DK + BRANCH_W + 2 * GLA_LOWRANK

LANES = 128
HEAD_PAD = 128
TOK_BLK = 256

COL_HG = 0
COL_Z = COL_HG + W_HG
COL_NAT = COL_Z + N_BRANCH * BRANCH_W
COL_GLA = COL_NAT + W_NAT
COL_MLA = COL_GLA + 1024
COL_GLR = COL_MLA + 512
IN_W_PAD = COL_GLR + LANES
MM_COL_BLK = IN_W_PAD // 3
MM_ROW_BLK = 768

SUB = 16
CHUNK = 64
EXP_CAP = 80.0
NEG_BIG = -1e30
LOG2E = 1.4426950408889634


def _cparams(sem, vmem_mb=48):
    return pltpu.CompilerParams(dimension_semantics=sem, vmem_limit_bytes=vmem_mb * 1024 * 1024)


def _dot(a, b):
    return jnp.dot(a, b, preferred_element_type=F32)


def _dot_nt(a, b):
    return lax.dot_general(a, b, (((1,), (1,)), ((), ())), preferred_element_type=F32)


def _sigmoid(x):
    return 0.5 * jnp.tanh(0.5 * x) + 0.5


def _ada_kernel(c_ref, w_ref, b_ref, o_ref):
    c = c_ref[...]
    a = (c * jax.nn.sigmoid(c)).astype(BF16)
    o_ref[...] = _dot(a, w_ref[...]) + b_ref[...]


def _ada_vectors(cc, ada_w, ada_b):
    L, D, D3 = ada_w.shape
    tn = 1024
    return pl.pallas_call(
        _ada_kernel,
        out_shape=jax.ShapeDtypeStruct((L, cc.shape[0], D3), F32),
        grid=(L, D3 // tn),
        in_specs=[
            pl.BlockSpec((cc.shape[0], D), lambda l, j: (0, 0)),
            pl.BlockSpec((None, D, tn), lambda l, j: (l, 0, j)),
            pl.BlockSpec((None, 1, tn), lambda l, j: (l, 0, j)),
        ],
        out_specs=pl.BlockSpec((None, cc.shape[0], tn), lambda l, j: (l, 0, j)),
        compiler_params=_cparams(("parallel", "parallel")),
        name="ada_vectors",
    )(cc, ada_w, ada_b.reshape(L, 1, D3))


def _norm_proj_kernel(x_ref, tail_ref, g_ref, m_ref, w_ref, p_ref, h_ref, hs_ref, *, n_lat):
    first_col = pl.program_id(2) == 0
    last_row_blk = pl.program_id(1) == pl.num_programs(1) - 1

    def modulated(x, scale, shift):
        y = x * lax.rsqrt(jnp.mean(x * x, axis=-1, keepdims=True) + EPS) * g_ref[...]
        h = (y * (1.0 + scale) + shift).astype(BF16)
        hs_ref[...] = h
        h_ref[...] = h

    @pl.when(first_col & jnp.logical_not(last_row_blk))
    def _():
        modulated(x_ref[...], m_ref[0, 1:2, :], m_ref[0, 0:1, :])

    @pl.when(first_col & last_row_blk)
    def _():
        rows = tail_ref.shape[0]
        row = lax.broadcasted_iota(jnp.int32, (rows, 1), 0) + pl.program_id(1) * rows
        is_ctx = row >= n_lat
        modulated(tail_ref[...], jnp.where(is_ctx, m_ref[1, 1:2, :], m_ref[0, 1:2, :]),
                  jnp.where(is_ctx, m_ref[1, 0:1, :], m_ref[0, 0:1, :]))

    p_ref[...] = _dot(hs_ref[...], w_ref[...]).astype(p_ref.dtype)


def _norm_project(x_main, x_tail, g, modsel, w, n_lat, Nt):
    B, _, D = x_main.shape
    N = w.shape[1]
    last_main = Nt // MM_ROW_BLK - 2
    return pl.pallas_call(
        functools.partial(_norm_proj_kernel, n_lat=n_lat),
        out_shape=(jax.ShapeDtypeStruct((B, Nt, N), BF16), jax.ShapeDtypeStruct((B, Nt, D), BF16)),
        grid=(B, Nt // MM_ROW_BLK, N // MM_COL_BLK),
        in_specs=[
            pl.BlockSpec((None, MM_ROW_BLK, D), lambda b, i, j: (b, jnp.minimum(i, last_main), 0)),
            pl.BlockSpec((None, MM_ROW_BLK, D), lambda b, i, j: (b, 0, 0)),
            pl.BlockSpec((1, D), lambda b, i, j: (0, 0)),
            pl.BlockSpec((None, 2, 3, D), lambda b, i, j: (b, 0, 0, 0)),
            pl.BlockSpec((D, MM_COL_BLK), lambda b, i, j: (0, j)),
        ],
        out_specs=(
            pl.BlockSpec((None, MM_ROW_BLK, MM_COL_BLK), lambda b, i, j: (b, i, j)),
            pl.BlockSpec((None, MM_ROW_BLK, D), lambda b, i, j: (b, i, 0)),
        ),
        scratch_shapes=[pltpu.VMEM((MM_ROW_BLK, D), BF16)],
        compiler_params=_cparams(("parallel", "parallel", "arbitrary")),
        name="norm_project",
    )(x_main, x_tail, g.reshape(1, D), modsel, w)


def _pack_w_in(w_in_l):
    o_nat = W_MLA
    o_hg = o_nat + W_NAT
    o_gla = o_hg + W_HG
    o_z = o_gla + W_GLA
    D = w_in_l.shape[0]
    zeros = lambda n: jnp.zeros((D, n), w_in_l.dtype)
    cols = [
        w_in_l[:, o_hg:o_hg + W_HG],
        w_in_l[:, o_z:o_z + N_BRANCH * BRANCH_W],
        w_in_l[:, o_nat:o_nat + W_NAT],
        w_in_l[:, o_gla:o_gla + 1024],
        w_in_l[:, :W_MLA], zeros(512 - W_MLA),
        w_in_l[:, o_gla + 1024:o_gla + W_GLA], zeros(LANES - 2 * GLA_LOWRANK),
    ]
    return jnp.concatenate(cols, axis=1).astype(BF16)


MLA_TQ = 1024
MLA_TK = 768


def _mla_prep_kernel(p_ref, wq_ref, wk_ref, wv_ref, gcq_ref, gckv_ref, gq_ref, gk_ref, vone_ref,
                     c_ref, s_ref, qc_ref, kc_ref, v_ref, *, n_lat_blk):
    ctx_w = jnp.where(pl.program_id(1) == n_lat_blk, 1.0, 0.0)
    hw = MLA_HEADS * HEAD_PAD
    p = p_ref[...].astype(F32)
    cq = p[:, :Q_LORA]
    r = lax.rsqrt(jnp.mean(cq * cq, axis=-1, keepdims=True) + EPS)
    cqn = (cq * r * gcq_ref[...]).astype(BF16)
    qraw = _dot(cqn, wq_ref[...])
    kv = p[:, Q_LORA:]
    ckv = kv[:, :KV_LORA]
    r2 = lax.rsqrt(jnp.mean(ckv * ckv, axis=-1, keepdims=True) + EPS)
    lane = lax.broadcasted_iota(jnp.int32, kv.shape, 1)
    lhs = (kv * jnp.where(lane < KV_LORA, r2, 1.0) * gckv_ref[...]).astype(BF16)
    kraw = _dot(lhs, wk_ref[...])
    vraw = _dot(lhs, wv_ref[...]) + vone_ref[...]
    cs = c_ref[...]
    sn = s_ref[...]
    inv = 1.0 / MLA_QK
    for hd in range(MLA_HEADS):
        sl = slice(HEAD_PAD * hd, HEAD_PAD * (hd + 1))
        sw = slice(hw + HEAD_PAD * hd, hw + HEAD_PAD * (hd + 1))
        qh = qraw[:, sl]
        rq = lax.rsqrt(jnp.sum(qh * qh, axis=-1, keepdims=True) * inv + EPS)
        qn = qh * rq * gq_ref[0:1, :]
        q_rot = qn * cs + qraw[:, sw] * rq * gq_ref[1:2, :] * sn
        qc_ref[hd] = jnp.concatenate([q_rot, qn], axis=1).astype(qc_ref.dtype)
        kh = kraw[:, sl]
        rk = lax.rsqrt(jnp.sum(kh * kh, axis=-1, keepdims=True) * inv + EPS)
        kn = kh * rk * gk_ref[0:1, :]
        k_rot = kn * cs + kraw[:, sw] * rk * gk_ref[1:2, :] * sn
        kc_ref[hd] = jnp.concatenate([k_rot * (1.0 - ctx_w), kn * ctx_w], axis=1).astype(kc_ref.dtype)
        v_ref[hd] = vraw[:, sl].astype(v_ref.dtype)


def _mla_weights(w_uq, w_ukv, g_cq, g_ckv, g_q, g_k):
    H = MLA_HEADS
    half = MLA_ROPE // 2
    lo, mid, hi = MLA_NOPE, MLA_NOPE + half, MLA_QK

    def swapped(t):
        z = jnp.zeros_like(t)
        return z.at[..., lo:mid].set(-t[..., mid:hi]).at[..., mid:hi].set(t[..., lo:mid])

    def both(t):
        return jnp.concatenate([t, swapped(t)], axis=1).reshape(t.shape[0], 2 * H * HEAD_PAD).astype(BF16)

    wq = jnp.pad(w_uq.reshape(Q_LORA, H, MLA_QK), ((0, 0), (0, 0), (0, HEAD_PAD - MLA_QK)))
    wkv = w_ukv.reshape(KV_LORA, H, MLA_NOPE + MLA_V)
    wk = jnp.pad(wkv[:, :, :MLA_NOPE], ((0, 0), (0, 0), (0, HEAD_PAD - MLA_NOPE)))
    place = jnp.zeros((MLA_ROPE, H, HEAD_PAD), F32)
    place = place.at[jnp.arange(MLA_ROPE), :, MLA_NOPE + jnp.arange(MLA_ROPE)].set(1.0)
    wk = jnp.concatenate([wk, place, jnp.zeros((256 - KV_LORA - MLA_ROPE, H, HEAD_PAD), F32)], axis=0)
    wv = jnp.pad(wkv[:, :, MLA_NOPE:], ((0, 256 - KV_LORA), (0, 0), (0, HEAD_PAD - MLA_V)))
    wv = wv.reshape(256, H * HEAD_PAD).astype(BF16)
    vone = jnp.zeros((H, HEAD_PAD), F32).at[:, MLA_V].set(1.0).reshape(1, H * HEAD_PAD)
    gckv = jnp.concatenate([g_ckv, jnp.ones((256 - KV_LORA,), F32)]).reshape(1, 256)
    gq = jnp.pad(g_q * (MLA_QK ** -0.5 * LOG2E), (0, HEAD_PAD - MLA_QK))
    gk = jnp.pad(g_k, (0, HEAD_PAD - MLA_QK))
    unsign = jnp.where(jnp.arange(HEAD_PAD) < mid, -1.0, 1.0)
    gq2 = jnp.stack([gq, swapped(gq) * unsign], axis=0)
    gk2 = jnp.stack([gk, swapped(gk) * unsign], axis=0)
    return both(wq), both(wk), wv, g_cq.reshape(1, Q_LORA), gckv, gq2, gk2, vone


def _rope_tables(n_lat, n_ctx):
    quarter = MLA_ROPE // 4
    inv_freq = ROPE_BASE ** (-jnp.arange(quarter, dtype=F32) / quarter)
    t = jnp.arange(n_lat, dtype=jnp.int32)
    row = (t // GRID_W).astype(F32)
    col = (t % GRID_W).astype(F32)
    ang = jnp.concatenate([row[:, None] * inv_freq, col[:, None] * inv_freq], axis=-1)
    cos, sin = jnp.cos(ang), jnp.sin(ang)
    ones = jnp.ones((n_lat, MLA_NOPE), F32)
    tail = jnp.ones((n_lat, HEAD_PAD - MLA_QK), F32)
    zl = jnp.zeros((n_lat, MLA_NOPE), F32)
    zt = jnp.zeros((n_lat, HEAD_PAD - MLA_QK), F32)
    c = jnp.concatenate([ones, cos, cos, tail], axis=1)
    s = jnp.concatenate([zl, sin, sin, zt], axis=1)
    c = jnp.concatenate([c, jnp.ones((n_ctx, HEAD_PAD), F32)], axis=0)
    s = jnp.concatenate([s, jnp.zeros((n_ctx, HEAD_PAD), F32)], axis=0)
    return c, s


def _mla_prep(p3, mw, tabs):
    B, Nt, _ = p3.shape
    H = MLA_HEADS
    wq, wk, wv, gcq, gckv, gq, gk, vone = mw
    n_lat_blk = Nt // TOK_BLK - 1
    const = lambda shape: pl.BlockSpec(shape, lambda b, i: (0,) * len(shape))
    tab = pl.BlockSpec((TOK_BLK, HEAD_PAD), lambda b, i: (i, 0))
    out = lambda w: pl.BlockSpec((None, H, TOK_BLK, w), lambda b, i: (b, 0, i, 0))
    shp = lambda w: jax.ShapeDtypeStruct((B, H, Nt, w), BF16)
    return pl.pallas_call(
        functools.partial(_mla_prep_kernel, n_lat_blk=n_lat_blk),
        out_shape=(shp(2 * HEAD_PAD), shp(2 * HEAD_PAD), shp(HEAD_PAD)),
        grid=(B, Nt // TOK_BLK),
        in_specs=[
            pl.BlockSpec((None, TOK_BLK, 512), lambda b, i: (b, i, COL_MLA // 512)),
            const(wq.shape), const(wk.shape), const(wv.shape), const(gcq.shape), const(gckv.shape),
            const(gq.shape), const(gk.shape), const(vone.shape), tab, tab,
        ],
        out_specs=(out(2 * HEAD_PAD), out(2 * HEAD_PAD), out(HEAD_PAD)),
        compiler_params=_cparams(("parallel", "parallel")),
        name="mla_prep",
    )(p3, wq, wk, wv, gcq, gckv, gq, gk, vone, *tabs)


def _softmax_step(q, kc, vc, m, acc):
    s = _dot_nt(q, kc)
    m_new = jnp.maximum(m, jnp.max(s, axis=-1, keepdims=True))
    p = jnp.exp2((s - m_new).astype(BF16))
    acc = jnp.exp2(m - m_new) * acc + _dot(p, vc)
    return m_new, acc


def _mla_ctx_kernel(qc_ref, kc_ref, v_ref, o_ref):
    outs = []
    for hh in range(2):
        s = _dot_nt(qc_ref[hh], kc_ref[hh])
        p = jnp.exp2(s - jnp.max(s, axis=-1, keepdims=True))
        acc = _dot(p.astype(BF16), v_ref[hh])
        outs.append(acc[:, :MLA_V] / acc[:, MLA_V:MLA_V + 1])
    o_ref[...] = jnp.concatenate(outs, axis=-1).astype(o_ref.dtype)


def _mla_attention_ctx(qc, kc, v, n_lat):
    B, H, Nt, _ = kc.shape
    n_ctx = Nt - n_lat
    cblk = n_lat // n_ctx
    cspec = lambda w: pl.BlockSpec((None, 2, n_ctx, w), lambda b, g: (b, g, cblk, 0))
    return pl.pallas_call(
        _mla_ctx_kernel,
        out_shape=jax.ShapeDtypeStruct((B, n_ctx, BRANCH_W), BF16),
        grid=(B, H // 2),
        in_specs=[cspec(2 * HEAD_PAD), cspec(2 * HEAD_PAD), cspec(HEAD_PAD)],
        out_specs=pl.BlockSpec((None, n_ctx, 2 * MLA_V), lambda b, g: (b, 0, g)),
        compiler_params=_cparams(("parallel", "parallel")),
        name="mla_attention_ctx",
    )(qc, kc, v)


NAT_ROWS_BLK = 8
NAT_TQ = NAT_ROWS_BLK * GRID_W
NAT_WIN_ROWS = 16
NAT_WIN = NAT_WIN_ROWS * GRID_W
NAT_PREP_BLK = 1024


def _pair_norm(x, g, half0):
    x2 = x * x
    s0 = jnp.sum(jnp.where(half0, x2, 0.0), axis=-1, keepdims=True)
    s1 = jnp.sum(jnp.where(half0, 0.0, x2), axis=-1, keepdims=True)
    ms = jnp.where(half0, s0, s1) * (1.0 / NAT_DH)
    return x * lax.rsqrt(ms + EPS) * g


def _nat_kernel(q_ref, kl_ref, vl_ref, kc_ref, vc_ref, bias_ref, gq_ref, gk_ref, o_ref, kn_ref, vb_ref,
                *, n_lat, rows_total):
    rb = pl.program_id(2)
    half0 = lax.broadcasted_iota(jnp.int32, (1, LANES), 1) < NAT_DH

    @pl.when(rb == 0)
    def _():
        def body(c, carry):
            off = pl.multiple_of(c * NAT_PREP_BLK, NAT_PREP_BLK)
            kn_ref[pl.ds(off, NAT_PREP_BLK), :] = _pair_norm(
                kl_ref[pl.ds(off, NAT_PREP_BLK), :].astype(F32), gk_ref[...], half0).astype(BF16)
            vb_ref[pl.ds(off, NAT_PREP_BLK), :LANES] = vl_ref[pl.ds(off, NAT_PREP_BLK), :].astype(BF16)
            vb_ref[pl.ds(off, NAT_PREP_BLK), LANES:] = jnp.ones((NAT_PREP_BLK, LANES), BF16)
            return carry

        lax.fori_loop(0, n_lat // NAT_PREP_BLK, body, 0)
        kn_ref[n_lat:, :] = _pair_norm(kc_ref[...].astype(F32), gk_ref[...], half0).astype(BF16)
        vb_ref[n_lat:, :LANES] = vc_ref[...].astype(BF16)
        vb_ref[n_lat:, LANES:] = jnp.ones((vc_ref.shape[0], LANES), BF16)

    q = _pair_norm(q_ref[...].astype(F32), gq_ref[...], half0)
    ws = jnp.clip(rb * NAT_ROWS_BLK - NAT_KH // 2, 0, rows_total - NAT_WIN_ROWS)
    off = pl.multiple_of(ws * GRID_W, GRID_W)
    kw = kn_ref[pl.ds(off, NAT_WIN), :]
    vw = vb_ref[pl.ds(off, NAT_WIN), :]
    kc = kn_ref[n_lat:, :]
    vc = vb_ref[n_lat:, :]
    qms = [jnp.where(half0 if hh == 0 else jnp.logical_not(half0), q, 0.0).astype(BF16) for hh in range(2)]
    sws = [_dot_nt(qm, kw) + bias_ref[hh].astype(F32) for hh, qm in enumerate(qms)]
    scs = [_dot_nt(qm, kc) for qm in qms]
    ms = [jnp.maximum(jnp.max(sw, axis=-1, keepdims=True), jnp.max(sc, axis=-1, keepdims=True))
          for sw, sc in zip(sws, scs)]
    accs = [_dot(jnp.exp2((sw - m).astype(BF16)), vw) + _dot(jnp.exp2((sc - m).astype(BF16)), vc)
            for sw, sc, m in zip(sws, scs, ms)]
    outs = [acc[:, :LANES] / acc[:, LANES:] for acc in accs]
    o_ref[...] = jnp.where(half0, outs[0], outs[1]).astype(o_ref.dtype)


def _nat_ctx_kernel(q_ref, kc_ref, vc_ref, gq_ref, gk_ref, o_ref):
    half0 = lax.broadcasted_iota(jnp.int32, (1, LANES), 1) < NAT_DH
    q = _pair_norm(q_ref[...].astype(F32), gq_ref[...], half0)
    kc = _pair_norm(kc_ref[...].astype(F32), gk_ref[...], half0).astype(BF16)
    vc = vc_ref[...].astype(BF16)
    outs = []
    for hh in range(2):
        sel = half0 if hh == 0 else jnp.logical_not(half0)
        s = _dot_nt(jnp.where(sel, q, 0.0).astype(BF16), kc)
        p = jnp.exp2(s - jnp.max(s, axis=-1, keepdims=True))
        outs.append(_dot(p.astype(BF16), vc) / jnp.sum(p, axis=-1, keepdims=True))
    o_ref[...] = jnp.where(half0, outs[0], outs[1]).astype(o_ref.dtype)


def _nat_bias_tables(rpb, rows_total):
    rbs = np.array([0, min(NAT_ROWS_BLK, rows_total - NAT_ROWS_BLK), rows_total - NAT_ROWS_BLK])
    ws = np.clip(rbs - NAT_KH // 2, 0, rows_total - NAT_WIN_ROWS)
    qrow = rbs[:, None] + np.arange(NAT_ROWS_BLK)[None, :]
    r0 = np.clip(qrow - NAT_KH // 2, 0, rows_total - NAT_KH)
    krow = ws[:, None] + np.arange(NAT_WIN_ROWS)[None, :]
    row_ok = (krow[:, None, :] >= r0[:, :, None]) & (krow[:, None, :] < r0[:, :, None] + NAT_KH)
    dr = np.clip(krow[:, None, :] - qrow[:, :, None] + NAT_KH - 1, 0, 2 * NAT_KH - 2)
    col = np.arange(GRID_W)
    c0 = np.clip(col - NAT_KW // 2, 0, GRID_W - NAT_KW)
    col_ok = (col[None, :] >= c0[:, None]) & (col[None, :] < c0[:, None] + NAT_KW)
    dc = np.clip(col[None, :] - col[:, None], -(NAT_KW - 1), NAT_KW - 1) + NAT_KW - 1
    oh_c = np.eye(2 * NAT_KW - 1, dtype=np.float32)[dc]
    by_col = jnp.einsum('hrc,qkc->hrqk', rpb.astype(F32), oh_c, precision=lax.Precision.HIGHEST)
    by_col = jnp.where(jnp.asarray(col_ok), by_col * LOG2E, NEG_BIG).astype(BF16)
    masked = jnp.full(by_col[:, 0].shape, NEG_BIG, BF16)
    variants = []
    for v in range(3):
        rows = [jnp.concatenate([by_col[:, dr[v, i, j]] if row_ok[v, i, j] else masked
                                 for j in range(NAT_WIN_ROWS)], axis=-1) for i in range(NAT_ROWS_BLK)]
        variants.append(jnp.concatenate(rows, axis=-2))
    return jnp.stack(variants).reshape(3, NAT_HEADS // 2, 2, NAT_TQ, NAT_WIN)


def _nat_attention(p3, rpb, g_q, g_k, n_lat, need_ctx):
    B, Nt, _ = p3.shape
    n_ctx = Nt - n_lat
    rows_total = n_lat // GRID_W
    n_rb = rows_total // NAT_ROWS_BLK
    bias = _nat_bias_tables(rpb, rows_total)
    gq = jnp.tile(g_q * (NAT_DH ** -0.5 * LOG2E), 2).reshape(1, LANES)
    gk = jnp.tile(g_k, 2).reshape(1, LANES)
    cq, ck, cv = ((COL_NAT + j * BRANCH_W) // LANES for j in range(3))
    cblk = n_lat // n_ctx

    def variant(rb):
        return jnp.where(rb == 0, 0, jnp.where(rb == n_rb - 1, 2, 1))

    vec = pl.BlockSpec((1, LANES), lambda b, g, rb: (0, 0))
    o_lat = pl.pallas_call(
        functools.partial(_nat_kernel, n_lat=n_lat, rows_total=rows_total),
        out_shape=jax.ShapeDtypeStruct((B, n_lat, BRANCH_W), BF16),
        grid=(B, NAT_HEADS // 2, n_rb),
        in_specs=[
            pl.BlockSpec((None, NAT_TQ, LANES), lambda b, g, rb: (b, rb, cq + g)),
            pl.BlockSpec((None, n_lat, LANES), lambda b, g, rb: (b, 0, ck + g)),
            pl.BlockSpec((None, n_lat, LANES), lambda b, g, rb: (b, 0, cv + g)),
            pl.BlockSpec((None, n_ctx, LANES), lambda b, g, rb: (b, cblk, ck + g)),
            pl.BlockSpec((None, n_ctx, LANES), lambda b, g, rb: (b, cblk, cv + g)),
            pl.BlockSpec((None, None, 2, NAT_TQ, NAT_WIN), lambda b, g, rb: (variant(rb), g, 0, 0, 0)),
            vec, vec,
        ],
        out_specs=pl.BlockSpec((None, NAT_TQ, LANES), lambda b, g, rb: (b, rb, g)),
        scratch_shapes=[pltpu.VMEM((Nt, LANES), BF16), pltpu.VMEM((Nt, 2 * LANES), BF16)],
        compiler_params=_cparams(("parallel", "parallel", "arbitrary")),
        name="nat_attention",
    )(p3, p3, p3, p3, p3, bias, gq, gk)
    if not need_ctx:
        return o_lat, None
    vec2 = pl.BlockSpec((1, LANES), lambda b, g: (0, 0))
    o_ctx = pl.pallas_call(
        _nat_ctx_kernel,
        out_shape=jax.ShapeDtypeStruct((B, n_ctx, BRANCH_W), BF16),
        grid=(B, NAT_HEADS // 2),
        in_specs=[
            pl.BlockSpec((None, n_ctx, LANES), lambda b, g: (b, cblk, cq + g)),
            pl.BlockSpec((None, n_ctx, LANES), lambda b, g: (b, cblk, ck + g)),
            pl.BlockSpec((None, n_ctx, LANES), lambda b, g: (b, cblk, cv + g)),
            vec2, vec2,
        ],
        out_specs=pl.BlockSpec((None, n_ctx, LANES), lambda b, g: (b, 0, g)),
        compiler_params=_cparams(("parallel", "parallel")),
        name="nat_attention_ctx",
    )(p3, p3, p3, gq, gk)
    return o_lat, o_ctx


def _log_sigmoid(x):
    return jnp.minimum(x, 0.0) - jnp.log(1.0 + jnp.exp(-jnp.abs(x)))


def _run_block(operands, s_ref, o_ref, rev):
    n_ch, nsub = o_ref.shape[0] // CHUNK, CHUNK // SUB
    r64 = lax.broadcasted_iota(jnp.int32, (CHUNK, CHUNK), 0)
    c64 = lax.broadcasted_iota(jnp.int32, (CHUNK, CHUNK), 1)
    causal = (r64 <= c64) if rev else (r64 >= c64)
    tri = jnp.where(causal, 1.0, 0.0).astype(BF16)
    first = [SUB * i + (SUB - 1 if rev else 0) for i in range(nsub)]
    last = [SUB * i + (0 if rev else SUB - 1) for i in range(nsub)]
    loaded = []
    for c in range(n_ch):
        qs, k, v, a, heads = operands(slice(CHUNK * c, CHUNK * (c + 1)))
        hi = a.astype(BF16)
        mid = (a - hi.astype(F32)).astype(BF16)
        loaded.append((qs, k, v, a, _dot(tri, hi) + _dot(tri, mid)))
        yield
    staged = []
    for c in range(n_ch):
        qs, k, v, a, bc = loaded[c]
        W = qs.shape[1]
        bex = bc - a
        r_start = [bex[r:r + 1, :] for r in first]
        r_end = [bc[r:r + 1, :] for r in last]
        bend = r_end[0] if rev else r_end[nsub - 1]

        def spread(rows):
            return jnp.concatenate([jnp.broadcast_to(r, (SUB, W)) for r in rows], axis=0)

        rs_full = spread(r_start)
        qt = qs * jnp.exp(bc - rs_full)
        q_in = qt * spread([jnp.exp(r) for r in r_start])
        ke = k * jnp.exp(spread(r_end) - bc)
        k_out = ke * spread([jnp.exp(bend - r) for r in r_end])
        kd = k * jnp.exp(jnp.minimum(rs_full - bc, EXP_CAP))
        zeros = jnp.zeros((SUB, W), F32)
        kts = []
        for i in range(nsub):
            rows = []
            for j in range(nsub):
                blk = slice(SUB * j, SUB * (j + 1))
                if j == i:
                    rows.append(kd[blk])
                elif (j > i) if rev else (j < i):
                    rows.append(ke[blk] if abs(i - j) == 1 else ke[blk] * jnp.exp(r_start[i] - r_end[j]))
                else:
                    rows.append(zeros)
            kts.append(jnp.concatenate(rows, axis=0).astype(BF16))
        vb = v.astype(BF16)
        intra, incr, q_state = [], [], []
        for ks, kmask, vs in heads:
            ksl = slice(ks, ks + LANES)
            vsl = slice(vs, vs + LANES)

            def msk(t, kmask=kmask):
                return t if kmask is None else jnp.where(kmask, t, 0.0)

            qt_h = msk(qt[:, ksl]).astype(BF16)
            blocks = [_dot_nt(qt_h[SUB * i:SUB * (i + 1), :], kts[i][:, ksl]) for i in range(nsub)]
            sc = jnp.where(causal, jnp.concatenate(blocks, axis=0), 0.0).astype(BF16)
            intra.append(_dot(sc, vb[:, vsl]))
            incr.append(_dot(v[:, vsl].T.astype(BF16), msk(k_out[:, ksl]).astype(BF16)))
            q_state.append(msk(q_in[:, ksl]).astype(BF16))
        staged.append((intra, incr, q_state, jnp.exp(bend)))
        yield
    states = [s_ref[hd] for hd in range(len(heads))]
    for c in (range(n_ch - 1, -1, -1) if rev else range(n_ch)):
        intra, incr, q_state, dec = staged[c]
        outs = []
        for hd, (ks, _, _) in enumerate(heads):
            outs.append(intra[hd] + _dot_nt(q_state[hd], states[hd].astype(BF16)))
            states[hd] = states[hd] * dec[:, ks:ks + LANES] + incr[hd]
        o_ref[CHUNK * c:CHUNK * (c + 1), :] = jnp.concatenate(outs, axis=-1).astype(o_ref.dtype)
        yield
    for hd, st in enumerate(states):
        s_ref[hd] = st


def _hgrn2_operands(q_ref, f_ref, v_ref, lb, rows):
    q = q_ref[rows, :].astype(F32)
    f = f_ref[rows, :].astype(F32)
    loglb, log1mlb, oneml = lb[0:1, :], lb[1:2, :], lb[2:3, :]
    u = log1mlb + _log_sigmoid(f)
    a = jnp.maximum(u, loglb) + jnp.log(1.0 + jnp.exp(-jnp.abs(u - loglb)))
    k = oneml * _sigmoid(-f)
    qs = q * _sigmoid(q) * (HG_DK ** -0.5)
    heads = [(LANES * h, None, LANES * h) for h in range(HG_HEADS)]
    return qs, k, v_ref[rows, :].astype(F32), a, heads


def _gla_operands(q_ref, k_ref, v_ref, r_ref, w2, b2, rows):
    g = _dot(r_ref[rows, :].astype(BF16), w2) + b2
    a = _log_sigmoid(g) * (1.0 / GLA_NORMALIZER)
    qs = q_ref[rows, :].astype(F32) * (GLA_DK ** -0.5)
    half0 = lax.broadcasted_iota(jnp.int32, (1, LANES), 1) < GLA_DK
    heads = [(LANES * (h // 2), half0 if h % 2 == 0 else jnp.logical_not(half0), LANES * h)
             for h in range(GLA_HEADS)]
    return qs, k_ref[rows, :].astype(F32), v_ref[rows, :].astype(F32), a, heads


def _scan_streams(hg_refs, lb_ref, gla_refs, w2_ref, b2_ref, out_refs, s_ref):
    hq_f, hf_f, hv_f, hq_b, hf_b, hv_b = hg_refs
    gq_f, gk_f, gv_f, gr_f, gq_b, gk_b, gv_b, gr_b = gla_refs
    oh_f, oh_b, og_f, og_b = out_refs
    return [
        _run_block(functools.partial(_hgrn2_operands, hq_f, hf_f, hv_f, lb_ref[0]), s_ref.at[0], oh_f, False),
        _run_block(functools.partial(_hgrn2_operands, hq_b, hf_b, hv_b, lb_ref[1]), s_ref.at[1], oh_b, True),
        _run_block(functools.partial(_gla_operands, gq_f, gk_f, gv_f, gr_f, w2_ref[0], b2_ref[0]),
                   s_ref.at[2], og_f, False),
        _run_block(functools.partial(_gla_operands, gq_b, gk_b, gv_b, gr_b, w2_ref[1], b2_ref[1]),
                   s_ref.at[3], og_b, True),
    ]


def _trace_round_robin(stages):
    done = object()
    while stages:
        stages = [g for g in stages if next(g, done) is not done]


def _scans_ctx_kernel(*refs):
    hg_refs, lb_ref, gla_refs, (w2_ref, b2_ref) = refs[:6], refs[6], refs[7:15], refs[15:17]
    out_refs, st_ref, s_ref = refs[17:21], refs[21], refs[22]
    s_ref[...] = jnp.zeros_like(s_ref)
    _trace_round_robin(_scan_streams(hg_refs, lb_ref, gla_refs, w2_ref, b2_ref, out_refs, s_ref))
    st_ref[...] = s_ref[...]


def _mla_steps(qc_ref, kc_ref, v_ref, o_ref, n_chunks):
    tq = qc_ref.shape[1]
    carry = [(jnp.full((tq, 1), NEG_BIG, F32), jnp.zeros((tq, HEAD_PAD), F32)) for _ in range(2)]
    for c in range(n_chunks):
        rows = slice(MLA_TK * c, MLA_TK * (c + 1))
        carry = [_softmax_step(qc_ref[hh], kc_ref[hh, rows, :], v_ref[hh, rows, :], *carry[hh]) for hh in range(2)]
        yield
    outs = [acc[:, :MLA_V] / acc[:, MLA_V:MLA_V + 1] for _, acc in carry]
    o_ref[...] = jnp.concatenate(outs, axis=-1).astype(o_ref.dtype)


def _mla_scans_kernel(*refs, n_chunks):
    qc_ref, kc_ref, v_ref = refs[:3]
    hg_refs, lb_ref, gla_refs, (w2_ref, b2_ref), st0_ref = refs[3:9], refs[9], refs[10:18], refs[18:20], refs[20]
    o_ref, out_refs, s_ref = refs[21], refs[22:26], refs[26]

    @pl.when(pl.program_id(1) == 0)
    def _():
        s_ref[...] = st0_ref[...]

    _trace_round_robin([_mla_steps(qc_ref, kc_ref, v_ref, o_ref, n_chunks)]
                       + _scan_streams(hg_refs, lb_ref, gla_refs, w2_ref, b2_ref, out_refs, s_ref))


def _scan_specs(blk):
    fwd, bwd = blk
    wqk = GLA_HEADS * GLA_DK
    spec = lambda bm, w, col: pl.BlockSpec((None, TOK_BLK, w), lambda b, s: (b, bm(s), col // w))
    hg = lambda bm, f_col: [spec(bm, BRANCH_W, COL_HG), spec(bm, BRANCH_W, f_col),
                            spec(bm, BRANCH_W, COL_HG + 3 * BRANCH_W)]
    gla = lambda bm: [spec(bm, wqk, COL_GLA), spec(bm, wqk, COL_GLA + wqk), spec(bm, BRANCH_W, COL_GLA + 2 * wqk),
                      spec(bm, LANES, COL_GLR)]
    return hg(fwd, COL_HG + BRANCH_W) + hg(bwd, COL_HG + 2 * BRANCH_W), gla(fwd) + gla(bwd)


def _scans_ctx(p3, lbvec, w2pad, b2, n_lat):
    B = p3.shape[0]
    ctx_blk = n_lat // TOK_BLK
    hg, gla = _scan_specs((lambda s: ctx_blk, lambda s: ctx_blk))
    const = lambda shape: pl.BlockSpec(shape, lambda b, s: (0,) * len(shape))
    shp = jax.ShapeDtypeStruct((B, TOK_BLK, BRANCH_W), BF16)
    st_shape = (4, HG_HEADS, LANES, LANES)
    return pl.pallas_call(
        _scans_ctx_kernel,
        out_shape=(shp, shp, shp, shp, jax.ShapeDtypeStruct((B,) + st_shape, F32)),
        grid=(B, 1),
        in_specs=hg + [const(lbvec.shape)] + gla + [const(w2pad.shape), const(b2.shape)],
        out_specs=tuple([pl.BlockSpec((None, TOK_BLK, BRANCH_W), lambda b, s: (b, 0, 0))] * 4
                        + [pl.BlockSpec((None,) + st_shape, lambda b, s: (b, 0, 0, 0, 0))]),
        scratch_shapes=[pltpu.VMEM(st_shape, F32)],
        compiler_params=_cparams(("parallel", "arbitrary")),
        name="linear_scans_ctx",
    )(*([p3] * 6), lbvec, *([p3] * 8), w2pad, b2)


def _mla_and_scans(qc, kc, v, p3, lbvec, w2pad, b2, states0, n_lat):
    B, H, Nt, _ = kc.shape
    n_q = n_lat // MLA_TQ
    n_blk = n_lat // TOK_BLK
    assert (H // 2) * n_q == n_blk
    hg, gla = _scan_specs((lambda s: s, lambda s: n_blk - 1 - s))
    const = lambda shape: pl.BlockSpec(shape, lambda b, s: (0,) * len(shape))
    st_shape = (4, HG_HEADS, LANES, LANES)
    shp = jax.ShapeDtypeStruct((B, n_lat, BRANCH_W), BF16)
    scan_out = lambda bm: pl.BlockSpec((None, TOK_BLK, BRANCH_W), lambda b, s: (b, bm(s), 0))
    return pl.pallas_call(
        functools.partial(_mla_scans_kernel, n_chunks=Nt // MLA_TK),
        out_shape=(shp,) * 5,
        grid=(B, n_blk),
        in_specs=[
            pl.BlockSpec((None, 2, MLA_TQ, 2 * HEAD_PAD), lambda b, s: (b, s // n_q, s % n_q, 0)),
            pl.BlockSpec((None, 2, Nt, 2 * HEAD_PAD), lambda b, s: (b, s // n_q, 0, 0)),
            pl.BlockSpec((None, 2, Nt, HEAD_PAD), lambda b, s: (b, s // n_q, 0, 0)),
        ] + hg + [const(lbvec.shape)] + gla + [const(w2pad.shape), const(b2.shape),
                                                pl.BlockSpec((None,) + st_shape, lambda b, s: (b, 0, 0, 0, 0))],
        out_specs=(
            pl.BlockSpec((None, MLA_TQ, 2 * MLA_V), lambda b, s: (b, s % n_q, s // n_q)),
            scan_out(lambda s: s), scan_out(lambda s: n_blk - 1 - s),
            scan_out(lambda s: s), scan_out(lambda s: n_blk - 1 - s),
        ),
        scratch_shapes=[pltpu.VMEM(st_shape, F32)],
        compiler_params=_cparams(("parallel", "arbitrary"), vmem_mb=58),
        name="mla_attention_and_scans",
    )(qc, kc, v, *([p3] * 6), lbvec, *([p3] * 8), w2pad, b2, states0)


def _group_norm(o, g):
    parts = []
    for hd in range(o.shape[1] // LANES):
        oh = o[:, LANES * hd:LANES * (hd + 1)]
        parts.append(oh * lax.rsqrt(jnp.mean(oh * oh, axis=-1, keepdims=True) + EPS))
    return jnp.concatenate(parts, axis=-1) * g


def _merge_kernel(x_ref, h_ref, za_ref, zb_ref, zc_ref, zd_ref, ya_ref, yb_ref, hf_ref, hb_ref, gf_ref,
                  gb_ref, wbr_ref, wmg_ref, bmg_ref, wout_ref, ghg_ref, ggl_ref, m_ref, o_ref):
    h = h_ref[...]
    ys = (
        ya_ref[...].astype(F32),
        yb_ref[...].astype(F32),
        _group_norm(hf_ref[...].astype(F32) + hb_ref[...].astype(F32), ghg_ref[...]),
        _group_norm(gf_ref[...].astype(F32) + gb_ref[...].astype(F32), ggl_ref[...]),
    )
    zs = (za_ref, zb_ref, zc_ref, zd_ref)
    acc = None
    for br in range(N_BRANCH):
        gate = _sigmoid(_dot(h, wmg_ref[br]) + bmg_ref[br])
        z = zs[br][...].astype(F32)
        yz = (ys[br] * (z * _sigmoid(z))).astype(BF16)
        part = gate * _dot(yz, wbr_ref[br])
        acc = part if acc is None else acc + part
    out = _dot(acc.astype(BF16), wout_ref[...])
    o_ref[...] = x_ref[...] + m_ref[2:3, :] * out


MERGE_LAT_BLK = 512


def _merge(x_part, y_a, y_b, h, p3, scans, w_br, w_merge, b_merge, w_out, g_hg, g_gla, modsel, part, n_lat):
    B, _, D = h.shape
    tm = TOK_BLK if part else MERGE_LAT_BLK
    n_blk = 1 if part else n_lat // tm
    blk0 = n_lat // tm if part else 0
    own = lambda w: pl.BlockSpec((None, tm, w), lambda b, i: (b, i, 0))
    stream = lambda w, col=0: pl.BlockSpec((None, tm, w), lambda b, i: (b, blk0 + i, col // w))
    once = pl.Buffered(1)
    const = lambda shape: pl.BlockSpec(shape, lambda b, i: (0,) * len(shape), pipeline_mode=once)
    ghg = jnp.tile(g_hg, HG_HEADS).reshape(1, BRANCH_W)
    ggl = jnp.tile(g_gla, GLA_HEADS).reshape(1, BRANCH_W)
    bm = b_merge.reshape(N_BRANCH, 1, D)
    return pl.pallas_call(
        _merge_kernel,
        out_shape=jax.ShapeDtypeStruct((B, n_blk * tm, D), F32),
        grid=(B, n_blk),
        in_specs=[own(D), stream(D)] + [stream(BRANCH_W, COL_Z + br * BRANCH_W) for br in range(N_BRANCH)]
        + [own(BRANCH_W)] * 6 + [
            const(w_br.shape), const(w_merge.shape), const(bm.shape), const(w_out.shape),
            const(ghg.shape), const(ggl.shape),
            pl.BlockSpec((None, None, 3, D), lambda b, i: (b, part, 0, 0)),
        ],
        out_specs=own(D),
        compiler_params=_cparams(("parallel", "parallel"), vmem_mb=56),
        name="branch_merge_ctx" if part else "branch_merge",
    )(x_part, h, p3, p3, p3, p3, y_a, y_b, *scans, w_br, w_merge, bm, w_out, ghg, ggl, modsel)


def kernel(x, c, ctx, c_ctx, ada_w, ada_b, norm_g, w_in, mla_w_uq, mla_w_ukv, mla_g_cq, mla_g_ckv, mla_g_q,
           mla_g_k, nat_rpb, nat_g_q, nat_g_k, hg_lb_logits, hg_g_o, gla_w2, gla_b2, gla_g_o, w_br, w_merge,
           b_merge, w_out):
    B, n_lat, D = x.shape
    n_ctx = ctx.shape[1]
    depth = ada_w.shape[0]
    Nt = n_lat + n_ctx
    n_lat_blk = n_lat // TOK_BLK
    assert n_ctx == TOK_BLK and n_lat % NAT_PREP_BLK == 0 and Nt % MM_ROW_BLK == 0
    assert Nt % MLA_TK == 0 and n_lat % MLA_TQ == 0

    cc = jnp.zeros((8, D), F32).at[:B].set(c).at[B].set(c_ctx)
    ada = _ada_vectors(cc, ada_w.astype(BF16), ada_b).reshape(depth, 8, 3, D)
    sm = jax.nn.softmax(hg_lb_logits.astype(F32), axis=0)
    lower = jnp.maximum(jnp.cumsum(sm, axis=0) - sm[0], 0.0)
    tabs = _rope_tables(n_lat, n_ctx)

    x_lat, x_ctx = x, ctx
    for l in range(depth):
        need_ctx = l < depth - 1
        modsel = jnp.stack([ada[l, :B], jnp.broadcast_to(ada[l, B], (B, 3, D))], axis=1)
        x_tail = jnp.concatenate([x_lat[:, n_lat - (MM_ROW_BLK - n_ctx):n_lat], x_ctx], axis=1)
        p3, h = _norm_project(x_lat, x_tail, norm_g[l], modsel, _pack_w_in(w_in[l]), n_lat, Nt)

        mw = _mla_weights(mla_w_uq[l], mla_w_ukv[l], mla_g_cq[l], mla_g_ckv[l], mla_g_q[l], mla_g_k[l])
        qc, kc, vm = _mla_prep(p3, mw, tabs)
        y_b = _nat_attention(p3, nat_rpb[l], nat_g_q[l], nat_g_k[l], n_lat, need_ctx)

        lb = lower[l]
        lbvec = jnp.stack([jnp.log(lb), jnp.log1p(-lb), 1.0 - lb], axis=1)
        w2pad = jnp.zeros((2, LANES, GLA_HEADS * GLA_DK), F32)
        for d in range(2):
            w2pad = w2pad.at[d, GLA_LOWRANK * d:GLA_LOWRANK * (d + 1)].set(gla_w2[l, d])
        scan_args = (p3, lbvec, w2pad.astype(BF16), gla_b2[l].reshape(2, 1, -1))
        *scans_ctx, states0 = _scans_ctx(*scan_args, n_lat)
        y_a_lat, *scans_lat = _mla_and_scans(qc, kc, vm, *scan_args, states0, n_lat)
        y_a = (y_a_lat, _mla_attention_ctx(qc, kc, vm, n_lat) if need_ctx else None)
        scans = (scans_lat, scans_ctx)

        wts = (w_br[l].astype(BF16), w_merge[l].astype(BF16), b_merge[l], w_out[l].astype(BF16), hg_g_o[l], gla_g_o[l])
        parts = [_merge(x_part, y_a[part], y_b[part], h, p3, scans[part], *wts, modsel, part, n_lat)
                 for part, x_part in enumerate((x_lat, x_ctx)) if part == 0 or need_ctx]
        x_lat, x_ctx = parts[0], (parts[1] if need_ctx else None)
    return x_lat
```

```python
import functools

import jax
import jax.numpy as jnp
import numpy as np
from jax import lax
from jax.experimental import pallas as pl
from jax.experimental.pallas import tpu as pltpu

F32 = jnp.float32
BF16 = jnp.bfloat16

GRID_W = 64
N_BRANCH = 4
BRANCH_W = 512
MLA_HEADS = 8
MLA_NOPE = 64
MLA_ROPE = 32
MLA_V = 64
MLA_QK = MLA_NOPE + MLA_ROPE
Q_LORA = 256
KV_LORA = 128
ROPE_BASE = 10000.0
NAT_HEADS = 8
NAT_DH = 64
NAT_KH = 8
NAT_KW = 16
HG_HEADS = 4
HG_DK = 128
GLA_HEADS = 4
GLA_DK = 64
GLA_LOWRANK = 16
GLA_NORMALIZER = 16.0
EPS = 1e-6

W_MLA = Q_LORA + KV_LORA + MLA_ROPE
W_NAT = 3 * BRANCH_W
W_HG = 4 * BRANCH_W
W_GLA = 2 * GLA_HEADS * GLA_DK + BRANCH_W + 2 * GLA_LOWRANK

LANES = 128
HEAD_PAD = 128
TOK_BLK = 256

COL_HG = 0
COL_Z = COL_HG + W_HG
COL_NAT = COL_Z + N_BRANCH * BRANCH_W
COL_GLA = COL_NAT + W_NAT
COL_MLA = COL_GLA + 1024
COL_GLR = COL_MLA + 512
IN_W_PAD = COL_GLR + LANES
MM_COL_BLK = IN_W_PAD // 3
MM_ROW_BLK = 768

SUB = 16
CHUNK = 64
EXP_CAP = 80.0
NEG_BIG = -1e30
LOG2E = 1.4426950408889634


def _cparams(sem, vmem_mb=48):
    return pltpu.CompilerParams(dimension_semantics=sem, vmem_limit_bytes=vmem_mb * 1024 * 1024)


def _dot(a, b):
    return jnp.dot(a, b, preferred_element_type=F32)


def _dot_nt(a, b):
    return lax.dot_general(a, b, (((1,), (1,)), ((), ())), preferred_element_type=F32)


def _sigmoid(x):
    return 0.5 * jnp.tanh(0.5 * x) + 0.5


def _ada_kernel(c_ref, w_ref, b_ref, o_ref):
    c = c_ref[...]
    a = (c * jax.nn.sigmoid(c)).astype(BF16)
    o_ref[...] = _dot(a, w_ref[...]) + b_ref[...]


def _ada_vectors(cc, ada_w, ada_b):
    L, D, D3 = ada_w.shape
    tn = 1024
    return pl.pallas_call(
        _ada_kernel,
        out_shape=jax.ShapeDtypeStruct((L, cc.shape[0], D3), F32),
        grid=(L, D3 // tn),
        in_specs=[
            pl.BlockSpec((cc.shape[0], D), lambda l, j: (0, 0)),
            pl.BlockSpec((None, D, tn), lambda l, j: (l, 0, j)),
            pl.BlockSpec((None, 1, tn), lambda l, j: (l, 0, j)),
        ],
        out_specs=pl.BlockSpec((None, cc.shape[0], tn), lambda l, j: (l, 0, j)),
        compiler_params=_cparams(("parallel", "parallel")),
        name="ada_vectors",
    )(cc, ada_w, ada_b.reshape(L, 1, D3))


def _norm_proj_kernel(x_ref, tail_ref, g_ref, m_ref, w_ref, p_ref, h_ref, hs_ref, *, n_lat):
    first_col = pl.program_id(2) == 0
    last_row_blk = pl.program_id(1) == pl.num_programs(1) - 1

    def modulated(x, scale, shift):
        y = x * lax.rsqrt(jnp.mean(x * x, axis=-1, keepdims=True) + EPS) * g_ref[...]
        h = (y * (1.0 + scale) + shift).astype(BF16)
        hs_ref[...] = h
        h_ref[...] = h

    @pl.when(first_col & jnp.logical_not(last_row_blk))
    def _():
        modulated(x_ref[...], m_ref[0, 1:2, :], m_ref[0, 0:1, :])

    @pl.when(first_col & last_row_blk)
    def _():
        rows = tail_ref.shape[0]
        row = lax.broadcasted_iota(jnp.int32, (rows, 1), 0) + pl.program_id(1) * rows
        is_ctx = row >= n_lat
        modulated(tail_ref[...], jnp.where(is_ctx, m_ref[1, 1:2, :], m_ref[0, 1:2, :]),
                  jnp.where(is_ctx, m_ref[1, 0:1, :], m_ref[0, 0:1, :]))

    p_ref[...] = _dot(hs_ref[...], w_ref[...]).astype(p_ref.dtype)


def _norm_project(x_main, x_tail, g, modsel, w, n_lat, Nt):
    B, _, D = x_main.shape
    N = w.shape[1]
    last_main = Nt // MM_ROW_BLK - 2
    return pl.pallas_call(
        functools.partial(_norm_proj_kernel, n_lat=n_lat),
        out_shape=(jax.ShapeDtypeStruct((B, Nt, N), BF16), jax.ShapeDtypeStruct((B, Nt, D), BF16)),
        grid=(B, Nt // MM_ROW_BLK, N // MM_COL_BLK),
        in_specs=[
            pl.BlockSpec((None, MM_ROW_BLK, D), lambda b, i, j: (b, jnp.minimum(i, last_main), 0)),
            pl.BlockSpec((None, MM_ROW_BLK, D), lambda b, i, j: (b, 0, 0)),
            pl.BlockSpec((1, D), lambda b, i, j: (0, 0)),
            pl.BlockSpec((None, 2, 3, D), lambda b, i, j: (b, 0, 0, 0)),
            pl.BlockSpec((D, MM_COL_BLK), lambda b, i, j: (0, j)),
        ],
        out_specs=(
            pl.BlockSpec((None, MM_ROW_BLK, MM_COL_BLK), lambda b, i, j: (b, i, j)),
            pl.BlockSpec((None, MM_ROW_BLK, D), lambda b, i, j: (b, i, 0)),
        ),
        scratch_shapes=[pltpu.VMEM((MM_ROW_BLK, D), BF16)],
        compiler_params=_cparams(("parallel", "parallel", "arbitrary")),
        name="norm_project",
    )(x_main, x_tail, g.reshape(1, D), modsel, w)


def _pack_w_in(w_in_l):
    o_nat = W_MLA
    o_hg = o_nat + W_NAT
    o_gla = o_hg + W_HG
    o_z = o_gla + W_GLA
    D = w_in_l.shape[0]
    zeros = lambda n: jnp.zeros((D, n), w_in_l.dtype)
    cols = [
        w_in_l[:, o_hg:o_hg + W_HG],
        w_in_l[:, o_z:o_z + N_BRANCH * BRANCH_W],
        w_in_l[:, o_nat:o_nat + W_NAT],
        w_in_l[:, o_gla:o_gla + 1024],
        w_in_l[:, :W_MLA], zeros(512 - W_MLA),
        w_in_l[:, o_gla + 1024:o_gla + W_GLA], zeros(LANES - 2 * GLA_LOWRANK),
    ]
    return jnp.concatenate(cols, axis=1).astype(BF16)


MLA_TQ = 1024
MLA_TK = 768


def _mla_prep_kernel(p_ref, wq_ref, wk_ref, wv_ref, gcq_ref, gckv_ref, gq_ref, gk_ref, vone_ref,
                     c_ref, s_ref, qc_ref, kc_ref, v_ref, *, n_lat_blk):
    ctx_w = jnp.where(pl.program_id(1) == n_lat_blk, 1.0, 0.0)
    hw = MLA_HEADS * HEAD_PAD
    p = p_ref[...].astype(F32)
    cq = p[:, :Q_LORA]
    r = lax.rsqrt(jnp.mean(cq * cq, axis=-1, keepdims=True) + EPS)
    cqn = (cq * r * gcq_ref[...]).astype(BF16)
    qraw = _dot(cqn, wq_ref[...])
    kv = p[:, Q_LORA:]
    ckv = kv[:, :KV_LORA]
    r2 = lax.rsqrt(jnp.mean(ckv * ckv, axis=-1, keepdims=True) + EPS)
    lane = lax.broadcasted_iota(jnp.int32, kv.shape, 1)
    lhs = (kv * jnp.where(lane < KV_LORA, r2, 1.0) * gckv_ref[...]).astype(BF16)
    kraw = _dot(lhs, wk_ref[...])
    vraw = _dot(lhs, wv_ref[...]) + vone_ref[...]
    cs = c_ref[...]
    sn = s_ref[...]
    inv = 1.0 / MLA_QK
    for hd in range(MLA_HEADS):
        sl = slice(HEAD_PAD * hd, HEAD_PAD * (hd + 1))
        sw = slice(hw + HEAD_PAD * hd, hw + HEAD_PAD * (hd + 1))
        qh = qraw[:, sl]
        rq = lax.rsqrt(jnp.sum(qh * qh, axis=-1, keepdims=True) * inv + EPS)
        qn = qh * rq * gq_ref[0:1, :]
        q_rot = qn * cs + qraw[:, sw] * rq * gq_ref[1:2, :] * sn
        qc_ref[hd] = jnp.concatenate([q_rot, qn], axis=1).astype(qc_ref.dtype)
        kh = kraw[:, sl]
        rk = lax.rsqrt(jnp.sum(kh * kh, axis=-1, keepdims=True) * inv + EPS)
        kn = kh * rk * gk_ref[0:1, :]
        k_rot = kn * cs + kraw[:, sw] * rk * gk_ref[1:2, :] * sn
        kc_ref[hd] = jnp.concatenate([k_rot * (1.0 - ctx_w), kn * ctx_w], axis=1).astype(kc_ref.dtype)
        v_ref[hd] = vraw[:, sl].astype(v_ref.dtype)


def _mla_weights(w_uq, w_ukv, g_cq, g_ckv, g_q, g_k):
    H = MLA_HEADS
    half = MLA_ROPE // 2
    lo, mid, hi = MLA_NOPE, MLA_NOPE + half, MLA_QK

    def swapped(t):
        z = jnp.zeros_like(t)
        return z.at[..., lo:mid].set(-t[..., mid:hi]).at[..., mid:hi].set(t[..., lo:mid])

    def both(t):
        return jnp.concatenate([t, swapped(t)], axis=1).reshape(t.shape[0], 2 * H * HEAD_PAD).astype(BF16)

    wq = jnp.pad(w_uq.reshape(Q_LORA, H, MLA_QK), ((0, 0), (0, 0), (0, HEAD_PAD - MLA_QK)))
    wkv = w_ukv.reshape(KV_LORA, H, MLA_NOPE + MLA_V)
    wk = jnp.pad(wkv[:, :, :MLA_NOPE], ((0, 0), (0, 0), (0, HEAD_PAD - MLA_NOPE)))
    place = jnp.zeros((MLA_ROPE, H, HEAD_PAD), F32)
    place = place.at[jnp.arange(MLA_ROPE), :, MLA_NOPE + jnp.arange(MLA_ROPE)].set(1.0)
    wk = jnp.concatenate([wk, place, jnp.zeros((256 - KV_LORA - MLA_ROPE, H, HEAD_PAD), F32)], axis=0)
    wv = jnp.pad(wkv[:, :, MLA_NOPE:], ((0, 256 - KV_LORA), (0, 0), (0, HEAD_PAD - MLA_V)))
    wv = wv.reshape(256, H * HEAD_PAD).astype(BF16)
    vone = jnp.zeros((H, HEAD_PAD), F32).at[:, MLA_V].set(1.0).reshape(1, H * HEAD_PAD)
    gckv = jnp.concatenate([g_ckv, jnp.ones((256 - KV_LORA,), F32)]).reshape(1, 256)
    gq = jnp.pad(g_q * (MLA_QK ** -0.5 * LOG2E), (0, HEAD_PAD - MLA_QK))
    gk = jnp.pad(g_k, (0, HEAD_PAD - MLA_QK))
    unsign = jnp.where(jnp.arange(HEAD_PAD) < mid, -1.0, 1.0)
    gq2 = jnp.stack([gq, swapped(gq) * unsign], axis=0)
    gk2 = jnp.stack([gk, swapped(gk) * unsign], axis=0)
    return both(wq), both(wk), wv, g_cq.reshape(1, Q_LORA), gckv, gq2, gk2, vone


def _rope_tables(n_lat, n_ctx):
    quarter = MLA_ROPE // 4
    inv_freq = ROPE_BASE ** (-jnp.arange(quarter, dtype=F32) / quarter)
    t = jnp.arange(n_lat, dtype=jnp.int32)
    row = (t // GRID_W).astype(F32)
    col = (t % GRID_W).astype(F32)
    ang = jnp.concatenate([row[:, None] * inv_freq, col[:, None] * inv_freq], axis=-1)
    cos, sin = jnp.cos(ang), jnp.sin(ang)
    ones = jnp.ones((n_lat, MLA_NOPE), F32)
    tail = jnp.ones((n_lat, HEAD_PAD - MLA_QK), F32)
    zl = jnp.zeros((n_lat, MLA_NOPE), F32)
    zt = jnp.zeros((n_lat, HEAD_PAD - MLA_QK), F32)
    c = jnp.concatenate([ones, cos, cos, tail], axis=1)
    s = jnp.concatenate([zl, sin, sin, zt], axis=1)
    c = jnp.concatenate([c, jnp.ones((n_ctx, HEAD_PAD), F32)], axis=0)
    s = jnp.concatenate([s, jnp.zeros((n_ctx, HEAD_PAD), F32)], axis=0)
    return c, s


def _mla_prep(p3, mw, tabs):
    B, Nt, _ = p3.shape
    H = MLA_HEADS
    wq, wk, wv, gcq, gckv, gq, gk, vone = mw
    n_lat_blk = Nt // TOK_BLK - 1
    const = lambda shape: pl.BlockSpec(shape, lambda b, i: (0,) * len(shape))
    tab = pl.BlockSpec((TOK_BLK, HEAD_PAD), lambda b, i: (i, 0))
    out = lambda w: pl.BlockSpec((None, H, TOK_BLK, w), lambda b, i: (b, 0, i, 0))
    shp = lambda w: jax.ShapeDtypeStruct((B, H, Nt, w), BF16)
    return pl.pallas_call(
        functools.partial(_mla_prep_kernel, n_lat_blk=n_lat_blk),
        out_shape=(shp(2 * HEAD_PAD), shp(2 * HEAD_PAD), shp(HEAD_PAD)),
        grid=(B, Nt // TOK_BLK),
        in_specs=[
            pl.BlockSpec((None, TOK_BLK, 512), lambda b, i: (b, i, COL_MLA // 512)),
            const(wq.shape), const(wk.shape), const(wv.shape), const(gcq.shape), const(gckv.shape),
            const(gq.shape), const(gk.shape), const(vone.shape), tab, tab,
        ],
        out_specs=(out(2 * HEAD_PAD), out(2 * HEAD_PAD), out(HEAD_PAD)),
        compiler_params=_cparams(("parallel", "parallel")),
        name="mla_prep",
    )(p3, wq, wk, wv, gcq, gckv, gq, gk, vone, *tabs)


def _softmax_step(q, kc, vc, m, acc):
    s = _dot_nt(q, kc)
    m_new = jnp.maximum(m, jnp.max(s, axis=-1, keepdims=True))
    p = jnp.exp2((s - m_new).astype(BF16))
    acc = jnp.exp2(m - m_new) * acc + _dot(p, vc)
    return m_new, acc


def _mla_attn_kernel(qc_ref, kc_ref, v_ref, o_ref, *, n_chunks):
    tq = qc_ref.shape[1]

    def body(c, carry):
        off = pl.multiple_of(c * MLA_TK, MLA_TK)
        return tuple(
            _softmax_step(qc_ref[hh], kc_ref[hh, pl.ds(off, MLA_TK), :], v_ref[hh, pl.ds(off, MLA_TK), :], *carry[hh])
            for hh in range(2))

    m0 = jnp.full((tq, 1), NEG_BIG, F32)
    a0 = jnp.zeros((tq, HEAD_PAD), F32)
    carry = lax.fori_loop(0, n_chunks, body, ((m0, a0), (m0, a0)), unroll=True)
    outs = [acc[:, :MLA_V] / acc[:, MLA_V:MLA_V + 1] for _, acc in carry]
    o_ref[...] = jnp.concatenate(outs, axis=-1).astype(o_ref.dtype)


def _mla_ctx_kernel(qc_ref, kc_ref, v_ref, o_ref):
    outs = []
    for hh in range(2):
        s = _dot_nt(qc_ref[hh], kc_ref[hh])
        p = jnp.exp2(s - jnp.max(s, axis=-1, keepdims=True))
        acc = _dot(p.astype(BF16), v_ref[hh])
        outs.append(acc[:, :MLA_V] / acc[:, MLA_V:MLA_V + 1])
    o_ref[...] = jnp.concatenate(outs, axis=-1).astype(o_ref.dtype)


def _mla_attention(qc, kc, v, n_lat, need_ctx):
    B, H, Nt, _ = kc.shape
    n_ctx = Nt - n_lat
    o_lat = pl.pallas_call(
        functools.partial(_mla_attn_kernel, n_chunks=Nt // MLA_TK),
        out_shape=jax.ShapeDtypeStruct((B, n_lat, BRANCH_W), BF16),
        grid=(B, H // 2, n_lat // MLA_TQ),
        in_specs=[
            pl.BlockSpec((None, 2, MLA_TQ, 2 * HEAD_PAD), lambda b, g, i: (b, g, i, 0)),
            pl.BlockSpec((None, 2, Nt, 2 * HEAD_PAD), lambda b, g, i: (b, g, 0, 0)),
            pl.BlockSpec((None, 2, Nt, HEAD_PAD), lambda b, g, i: (b, g, 0, 0)),
        ],
        out_specs=pl.BlockSpec((None, MLA_TQ, 2 * MLA_V), lambda b, g, i: (b, i, g)),
        compiler_params=_cparams(("parallel", "parallel", "arbitrary")),
        name="mla_attention",
    )(qc, kc, v)
    if not need_ctx:
        return o_lat, None
    cblk = n_lat // n_ctx
    cspec = lambda w: pl.BlockSpec((None, 2, n_ctx, w), lambda b, g: (b, g, cblk, 0))
    o_ctx = pl.pallas_call(
        _mla_ctx_kernel,
        out_shape=jax.ShapeDtypeStruct((B, n_ctx, BRANCH_W), BF16),
        grid=(B, H // 2),
        in_specs=[cspec(2 * HEAD_PAD), cspec(2 * HEAD_PAD), cspec(HEAD_PAD)],
        out_specs=pl.BlockSpec((None, n_ctx, 2 * MLA_V), lambda b, g: (b, 0, g)),
        compiler_params=_cparams(("parallel", "parallel")),
        name="mla_attention_ctx",
    )(qc, kc, v)
    return o_lat, o_ctx


NAT_ROWS_BLK = 4
NAT_TQ = NAT_ROWS_BLK * GRID_W
NAT_WIN_ROWS = 12
NAT_WIN = NAT_WIN_ROWS * GRID_W
NAT_PREP_BLK = 1024


def _pair_norm(x, g, half0):
    x2 = x * x
    s0 = jnp.sum(jnp.where(half0, x2, 0.0), axis=-1, keepdims=True)
    s1 = jnp.sum(jnp.where(half0, 0.0, x2), axis=-1, keepdims=True)
    ms = jnp.where(half0, s0, s1) * (1.0 / NAT_DH)
    return x * lax.rsqrt(ms + EPS) * g


def _nat_kernel(q_ref, kl_ref, vl_ref, kc_ref, vc_ref, bias_ref, gq_ref, gk_ref, o_ref, kn_ref, vb_ref,
                *, n_lat, rows_total):
    rb = pl.program_id(2)
    half0 = lax.broadcasted_iota(jnp.int32, (1, LANES), 1) < NAT_DH

    @pl.when(rb == 0)
    def _():
        def body(c, carry):
            off = pl.multiple_of(c * NAT_PREP_BLK, NAT_PREP_BLK)
            kn_ref[pl.ds(off, NAT_PREP_BLK), :] = _pair_norm(
                kl_ref[pl.ds(off, NAT_PREP_BLK), :].astype(F32), gk_ref[...], half0).astype(BF16)
            vb_ref[pl.ds(off, NAT_PREP_BLK), :LANES] = vl_ref[pl.ds(off, NAT_PREP_BLK), :].astype(BF16)
            vb_ref[pl.ds(off, NAT_PREP_BLK), LANES:] = jnp.ones((NAT_PREP_BLK, LANES), BF16)
            return carry

        lax.fori_loop(0, n_lat // NAT_PREP_BLK, body, 0)
        kn_ref[n_lat:, :] = _pair_norm(kc_ref[...].astype(F32), gk_ref[...], half0).astype(BF16)
        vb_ref[n_lat:, :LANES] = vc_ref[...].astype(BF16)
        vb_ref[n_lat:, LANES:] = jnp.ones((vc_ref.shape[0], LANES), BF16)

    q = _pair_norm(q_ref[...].astype(F32), gq_ref[...], half0)
    ws = jnp.clip(rb * NAT_ROWS_BLK - NAT_KH // 2, 0, rows_total - NAT_WIN_ROWS)
    off = pl.multiple_of(ws * GRID_W, GRID_W)
    kw = kn_ref[pl.ds(off, NAT_WIN), :]
    vw = vb_ref[pl.ds(off, NAT_WIN), :]
    kc = kn_ref[n_lat:, :]
    vc = vb_ref[n_lat:, :]
    qms = [jnp.where(half0 if hh == 0 else jnp.logical_not(half0), q, 0.0).astype(BF16) for hh in range(2)]
    sws = [_dot_nt(qm, kw) + bias_ref[hh].astype(F32) for hh, qm in enumerate(qms)]
    scs = [_dot_nt(qm, kc) for qm in qms]
    ms = [jnp.maximum(jnp.max(sw, axis=-1, keepdims=True), jnp.max(sc, axis=-1, keepdims=True))
          for sw, sc in zip(sws, scs)]
    accs = [_dot(jnp.exp2((sw - m).astype(BF16)), vw) + _dot(jnp.exp2((sc - m).astype(BF16)), vc)
            for sw, sc, m in zip(sws, scs, ms)]
    outs = [acc[:, :LANES] / acc[:, LANES:] for acc in accs]
    o_ref[...] = jnp.where(half0, outs[0], outs[1]).astype(o_ref.dtype)


def _nat_ctx_kernel(q_ref, kc_ref, vc_ref, gq_ref, gk_ref, o_ref):
    half0 = lax.broadcasted_iota(jnp.int32, (1, LANES), 1) < NAT_DH
    q = _pair_norm(q_ref[...].astype(F32), gq_ref[...], half0)
    kc = _pair_norm(kc_ref[...].astype(F32), gk_ref[...], half0).astype(BF16)
    vc = vc_ref[...].astype(BF16)
    outs = []
    for hh in range(2):
        sel = half0 if hh == 0 else jnp.logical_not(half0)
        s = _dot_nt(jnp.where(sel, q, 0.0).astype(BF16), kc)
        p = jnp.exp2(s - jnp.max(s, axis=-1, keepdims=True))
        outs.append(_dot(p.astype(BF16), vc) / jnp.sum(p, axis=-1, keepdims=True))
    o_ref[...] = jnp.where(half0, outs[0], outs[1]).astype(o_ref.dtype)


def _nat_bias_tables(rpb, rows_total):
    rbs = np.array([0, min(NAT_ROWS_BLK, rows_total - NAT_ROWS_BLK), rows_total - NAT_ROWS_BLK])
    ws = np.clip(rbs - NAT_KH // 2, 0, rows_total - NAT_WIN_ROWS)
    qrow = rbs[:, None] + np.arange(NAT_ROWS_BLK)[None, :]
    r0 = np.clip(qrow - NAT_KH // 2, 0, rows_total - NAT_KH)
    krow = ws[:, None] + np.arange(NAT_WIN_ROWS)[None, :]
    row_ok = (krow[:, None, :] >= r0[:, :, None]) & (krow[:, None, :] < r0[:, :, None] + NAT_KH)
    dr = np.clip(krow[:, None, :] - qrow[:, :, None] + NAT_KH - 1, 0, 2 * NAT_KH - 2)
    col = np.arange(GRID_W)
    c0 = np.clip(col - NAT_KW // 2, 0, GRID_W - NAT_KW)
    col_ok = (col[None, :] >= c0[:, None]) & (col[None, :] < c0[:, None] + NAT_KW)
    dc = np.clip(col[None, :] - col[:, None], -(NAT_KW - 1), NAT_KW - 1) + NAT_KW - 1
    oh_c = np.eye(2 * NAT_KW - 1, dtype=np.float32)[dc]
    by_col = jnp.einsum('hrc,qkc->hrqk', rpb.astype(F32), oh_c, precision=lax.Precision.HIGHEST)
    by_col = jnp.where(jnp.asarray(col_ok), by_col * LOG2E, NEG_BIG).astype(BF16)
    masked = jnp.full(by_col[:, 0].shape, NEG_BIG, BF16)
    variants = []
    for v in range(3):
        rows = [jnp.concatenate([by_col[:, dr[v, i, j]] if row_ok[v, i, j] else masked
                                 for j in range(NAT_WIN_ROWS)], axis=-1) for i in range(NAT_ROWS_BLK)]
        variants.append(jnp.concatenate(rows, axis=-2))
    return jnp.stack(variants).reshape(3, NAT_HEADS // 2, 2, NAT_TQ, NAT_WIN)


def _nat_attention(p3, rpb, g_q, g_k, n_lat, need_ctx):
    B, Nt, _ = p3.shape
    n_ctx = Nt - n_lat
    rows_total = n_lat // GRID_W
    n_rb = rows_total // NAT_ROWS_BLK
    bias = _nat_bias_tables(rpb, rows_total)
    gq = jnp.tile(g_q * (NAT_DH ** -0.5 * LOG2E), 2).reshape(1, LANES)
    gk = jnp.tile(g_k, 2).reshape(1, LANES)
    cq, ck, cv = ((COL_NAT + j * BRANCH_W) // LANES for j in range(3))
    cblk = n_lat // n_ctx

    def variant(rb):
        return jnp.where(rb == 0, 0, jnp.where(rb == n_rb - 1, 2, 1))

    vec = pl.BlockSpec((1, LANES), lambda b, g, rb: (0, 0))
    o_lat = pl.pallas_call(
        functools.partial(_nat_kernel, n_lat=n_lat, rows_total=rows_total),
        out_shape=jax.ShapeDtypeStruct((B, n_lat, BRANCH_W), BF16),
        grid=(B, NAT_HEADS // 2, n_rb),
        in_specs=[
            pl.BlockSpec((None, NAT_TQ, LANES), lambda b, g, rb: (b, rb, cq + g)),
            pl.BlockSpec((None, n_lat, LANES), lambda b, g, rb: (b, 0, ck + g)),
            pl.BlockSpec((None, n_lat, LANES), lambda b, g, rb: (b, 0, cv + g)),
            pl.BlockSpec((None, n_ctx, LANES), lambda b, g, rb: (b, cblk, ck + g)),
            pl.BlockSpec((None, n_ctx, LANES), lambda b, g, rb: (b, cblk, cv + g)),
            pl.BlockSpec((None, None, 2, NAT_TQ, NAT_WIN), lambda b, g, rb: (variant(rb), g, 0, 0, 0)),
            vec, vec,
        ],
        out_specs=pl.BlockSpec((None, NAT_TQ, LANES), lambda b, g, rb: (b, rb, g)),
        scratch_shapes=[pltpu.VMEM((Nt, LANES), BF16), pltpu.VMEM((Nt, 2 * LANES), BF16)],
        compiler_params=_cparams(("parallel", "parallel", "arbitrary")),
        name="nat_attention",
    )(p3, p3, p3, p3, p3, bias, gq, gk)
    if not need_ctx:
        return o_lat, None
    vec2 = pl.BlockSpec((1, LANES), lambda b, g: (0, 0))
    o_ctx = pl.pallas_call(
        _nat_ctx_kernel,
        out_shape=jax.ShapeDtypeStruct((B, n_ctx, BRANCH_W), BF16),
        grid=(B, NAT_HEADS // 2),
        in_specs=[
            pl.BlockSpec((None, n_ctx, LANES), lambda b, g: (b, cblk, cq + g)),
            pl.BlockSpec((None, n_ctx, LANES), lambda b, g: (b, cblk, ck + g)),
            pl.BlockSpec((None, n_ctx, LANES), lambda b, g: (b, cblk, cv + g)),
            vec2, vec2,
        ],
        out_specs=pl.BlockSpec((None, n_ctx, LANES), lambda b, g: (b, 0, g)),
        compiler_params=_cparams(("parallel", "parallel")),
        name="nat_attention_ctx",
    )(p3, p3, p3, gq, gk)
    return o_lat, o_ctx


def _log_sigmoid(x):
    return jnp.minimum(x, 0.0) - jnp.log(1.0 + jnp.exp(-jnp.abs(x)))


def _run_block(operands, s_ref, o_ref, rev):
    n_ch, nsub = o_ref.shape[0] // CHUNK, CHUNK // SUB
    r64 = lax.broadcasted_iota(jnp.int32, (CHUNK, CHUNK), 0)
    c64 = lax.broadcasted_iota(jnp.int32, (CHUNK, CHUNK), 1)
    causal = (r64 <= c64) if rev else (r64 >= c64)
    tri = jnp.where(causal, 1.0, 0.0).astype(BF16)
    first = [SUB * i + (SUB - 1 if rev else 0) for i in range(nsub)]
    last = [SUB * i + (0 if rev else SUB - 1) for i in range(nsub)]
    loaded = []
    for c in range(n_ch):
        qs, k, v, a, heads = operands(slice(CHUNK * c, CHUNK * (c + 1)))
        hi = a.astype(BF16)
        mid = (a - hi.astype(F32)).astype(BF16)
        loaded.append((qs, k, v, a, _dot(tri, hi) + _dot(tri, mid)))
        yield
    staged = []
    for c in range(n_ch):
        qs, k, v, a, bc = loaded[c]
        W = qs.shape[1]
        bex = bc - a
        r_start = [bex[r:r + 1, :] for r in first]
        r_end = [bc[r:r + 1, :] for r in last]
        bend = r_end[0] if rev else r_end[nsub - 1]

        def spread(rows):
            return jnp.concatenate([jnp.broadcast_to(r, (SUB, W)) for r in rows], axis=0)

        rs_full = spread(r_start)
        qt = qs * jnp.exp(bc - rs_full)
        q_in = qt * spread([jnp.exp(r) for r in r_start])
        ke = k * jnp.exp(spread(r_end) - bc)
        k_out = ke * spread([jnp.exp(bend - r) for r in r_end])
        kd = k * jnp.exp(jnp.minimum(rs_full - bc, EXP_CAP))
        zeros = jnp.zeros((SUB, W), F32)
        kts = []
        for i in range(nsub):
            rows = []
            for j in range(nsub):
                blk = slice(SUB * j, SUB * (j + 1))
                if j == i:
                    rows.append(kd[blk])
                elif (j > i) if rev else (j < i):
                    rows.append(ke[blk] if abs(i - j) == 1 else ke[blk] * jnp.exp(r_start[i] - r_end[j]))
                else:
                    rows.append(zeros)
            kts.append(jnp.concatenate(rows, axis=0).astype(BF16))
        vb = v.astype(BF16)
        intra, incr, q_state = [], [], []
        for ks, kmask, vs in heads:
            ksl = slice(ks, ks + LANES)
            vsl = slice(vs, vs + LANES)

            def msk(t, kmask=kmask):
                return t if kmask is None else jnp.where(kmask, t, 0.0)

            qt_h = msk(qt[:, ksl]).astype(BF16)
            blocks = [_dot_nt(qt_h[SUB * i:SUB * (i + 1), :], kts[i][:, ksl]) for i in range(nsub)]
            sc = jnp.where(causal, jnp.concatenate(blocks, axis=0), 0.0).astype(BF16)
            intra.append(_dot(sc, vb[:, vsl]))
            incr.append(_dot(v[:, vsl].T.astype(BF16), msk(k_out[:, ksl]).astype(BF16)))
            q_state.append(msk(q_in[:, ksl]).astype(BF16))
        staged.append((intra, incr, q_state, jnp.exp(bend)))
        yield
    states = [s_ref[hd] for hd in range(len(heads))]
    for c in (range(n_ch - 1, -1, -1) if rev else range(n_ch)):
        intra, incr, q_state, dec = staged[c]
        outs = []
        for hd, (ks, _, _) in enumerate(heads):
            outs.append(intra[hd] + _dot_nt(q_state[hd], states[hd].astype(BF16)))
            states[hd] = states[hd] * dec[:, ks:ks + LANES] + incr[hd]
        o_ref[CHUNK * c:CHUNK * (c + 1), :] = jnp.concatenate(outs, axis=-1).astype(o_ref.dtype)
        yield
    for hd, st in enumerate(states):
        s_ref[hd] = st


def _hgrn2_operands(q_ref, f_ref, v_ref, lb, rows):
    q = q_ref[rows, :].astype(F32)
    f = f_ref[rows, :].astype(F32)
    loglb, log1mlb, oneml = lb[0:1, :], lb[1:2, :], lb[2:3, :]
    u = log1mlb + _log_sigmoid(f)
    a = jnp.maximum(u, loglb) + jnp.log(1.0 + jnp.exp(-jnp.abs(u - loglb)))
    k = oneml * _sigmoid(-f)
    qs = q * _sigmoid(q) * (HG_DK ** -0.5)
    heads = [(LANES * h, None, LANES * h) for h in range(HG_HEADS)]
    return qs, k, v_ref[rows, :].astype(F32), a, heads


def _gla_operands(q_ref, k_ref, v_ref, r_ref, w2, b2, rows):
    g = _dot(r_ref[rows, :].astype(BF16), w2) + b2
    a = _log_sigmoid(g) * (1.0 / GLA_NORMALIZER)
    qs = q_ref[rows, :].astype(F32) * (GLA_DK ** -0.5)
    half0 = lax.broadcasted_iota(jnp.int32, (1, LANES), 1) < GLA_DK
    heads = [(LANES * (h // 2), half0 if h % 2 == 0 else jnp.logical_not(half0), LANES * h)
             for h in range(GLA_HEADS)]
    return qs, k_ref[rows, :].astype(F32), v_ref[rows, :].astype(F32), a, heads


def _scans_kernel(hq_f, hf_f, hv_f, hq_b, hf_b, hv_b, lb_ref, gq_f, gk_f, gv_f, gr_f, gq_b, gk_b, gv_b, gr_b,
                  w2_ref, b2_ref, oh_f, oh_b, og_f, og_b, s_ref):
    @pl.when(pl.program_id(1) == 0)
    def _():
        s_ref[...] = jnp.zeros_like(s_ref)

    streams = [
        _run_block(functools.partial(_hgrn2_operands, hq_f, hf_f, hv_f, lb_ref[0]), s_ref.at[0], oh_f, False),
        _run_block(functools.partial(_hgrn2_operands, hq_b, hf_b, hv_b, lb_ref[1]), s_ref.at[1], oh_b, True),
        _run_block(functools.partial(_gla_operands, gq_f, gk_f, gv_f, gr_f, w2_ref[0], b2_ref[0]),
                   s_ref.at[2], og_f, False),
        _run_block(functools.partial(_gla_operands, gq_b, gk_b, gv_b, gr_b, w2_ref[1], b2_ref[1]),
                   s_ref.at[3], og_b, True),
    ]
    done = object()
    while streams:
        streams = [g for g in streams if next(g, done) is not done]


def _linear_scans(p3, lbvec, w2pad, b2, n_lat):
    B, Nt, _ = p3.shape
    nb = Nt // TOK_BLK
    n_lat_blk = n_lat // TOK_BLK
    fwd = lambda s: jnp.where(s == 0, n_lat_blk, s - 1)
    bwd = lambda s: jnp.where(s == 0, n_lat_blk, n_lat_blk - s)
    wqk = GLA_HEADS * GLA_DK
    spec = lambda blk, w, col: pl.BlockSpec((None, TOK_BLK, w), lambda b, s: (b, blk(s), col // w))
    hg = lambda blk, f_col: [spec(blk, BRANCH_W, COL_HG), spec(blk, BRANCH_W, f_col),
                             spec(blk, BRANCH_W, COL_HG + 3 * BRANCH_W)]
    gla = lambda blk: [spec(blk, wqk, COL_GLA), spec(blk, wqk, COL_GLA + wqk), spec(blk, BRANCH_W, COL_GLA + 2 * wqk),
                       spec(blk, LANES, COL_GLR)]
    const = lambda shape: pl.BlockSpec(shape, lambda b, s: (0,) * len(shape))
    out = lambda blk: pl.BlockSpec((None, TOK_BLK, BRANCH_W), lambda b, s: (b, blk(s), 0))
    shp = jax.ShapeDtypeStruct((B, Nt, BRANCH_W), BF16)
    return pl.pallas_call(
        _scans_kernel,
        out_shape=(shp, shp, shp, shp),
        grid=(B, nb),
        in_specs=(hg(fwd, COL_HG + BRANCH_W) + hg(bwd, COL_HG + 2 * BRANCH_W) + [const(lbvec.shape)]
                  + gla(fwd) + gla(bwd) + [const(w2pad.shape), const(b2.shape)]),
        out_specs=(out(fwd), out(bwd), out(fwd), out(bwd)),
        scratch_shapes=[pltpu.VMEM((4, HG_HEADS, LANES, LANES), F32)],
        compiler_params=_cparams(("parallel", "arbitrary")),
        name="linear_scans",
    )(*([p3] * 6), lbvec, *([p3] * 8), w2pad, b2)


def _group_norm(o, g):
    parts = []
    for hd in range(o.shape[1] // LANES):
        oh = o[:, LANES * hd:LANES * (hd + 1)]
        parts.append(oh * lax.rsqrt(jnp.mean(oh * oh, axis=-1, keepdims=True) + EPS))
    return jnp.concatenate(parts, axis=-1) * g


def _merge_kernel(x_ref, h_ref, za_ref, zb_ref, zc_ref, zd_ref, ya_ref, yb_ref, hf_ref, hb_ref, gf_ref,
                  gb_ref, wbr_ref, wmg_ref, bmg_ref, wout_ref, ghg_ref, ggl_ref, m_ref, o_ref):
    h = h_ref[...]
    ys = (
        ya_ref[...].astype(F32),
        yb_ref[...].astype(F32),
        _group_norm(hf_ref[...].astype(F32) + hb_ref[...].astype(F32), ghg_ref[...]),
        _group_norm(gf_ref[...].astype(F32) + gb_ref[...].astype(F32), ggl_ref[...]),
    )
    zs = (za_ref, zb_ref, zc_ref, zd_ref)
    acc = None
    for br in range(N_BRANCH):
        gate = _sigmoid(_dot(h, wmg_ref[br]) + bmg_ref[br])
        z = zs[br][...].astype(F32)
        yz = (ys[br] * (z * _sigmoid(z))).astype(BF16)
        part = gate * _dot(yz, wbr_ref[br])
        acc = part if acc is None else acc + part
    out = _dot(acc.astype(BF16), wout_ref[...])
    o_ref[...] = x_ref[...] + m_ref[2:3, :] * out


MERGE_LAT_BLK = 512


def _merge(x_part, y_a, y_b, h, p3, scans, w_br, w_merge, b_merge, w_out, g_hg, g_gla, modsel, part, n_lat):
    B, _, D = h.shape
    tm = TOK_BLK if part else MERGE_LAT_BLK
    n_blk = 1 if part else n_lat // tm
    blk0 = n_lat // tm if part else 0
    own = lambda w: pl.BlockSpec((None, tm, w), lambda b, i: (b, i, 0))
    stream = lambda w, col=0: pl.BlockSpec((None, tm, w), lambda b, i: (b, blk0 + i, col // w))
    once = pl.Buffered(1)
    const = lambda shape: pl.BlockSpec(shape, lambda b, i: (0,) * len(shape), pipeline_mode=once)
    ghg = jnp.tile(g_hg, HG_HEADS).reshape(1, BRANCH_W)
    ggl = jnp.tile(g_gla, GLA_HEADS).reshape(1, BRANCH_W)
    bm = b_merge.reshape(N_BRANCH, 1, D)
    return pl.pallas_call(
        _merge_kernel,
        out_shape=jax.ShapeDtypeStruct((B, n_blk * tm, D), F32),
        grid=(B, n_blk),
        in_specs=[own(D), stream(D)] + [stream(BRANCH_W, COL_Z + br * BRANCH_W) for br in range(N_BRANCH)]
        + [own(BRANCH_W)] * 2 + [stream(BRANCH_W)] * 4 + [
            const(w_br.shape), const(w_merge.shape), const(bm.shape), const(w_out.shape),
            const(ghg.shape), const(ggl.shape),
            pl.BlockSpec((None, None, 3, D), lambda b, i: (b, part, 0, 0)),
        ],
        out_specs=own(D),
        compiler_params=_cparams(("parallel", "parallel"), vmem_mb=56),
        name="branch_merge_ctx" if part else "branch_merge",
    )(x_part, h, p3, p3, p3, p3, y_a, y_b, *scans, w_br, w_merge, bm, w_out, ghg, ggl, modsel)


def kernel(x, c, ctx, c_ctx, ada_w, ada_b, norm_g, w_in, mla_w_uq, mla_w_ukv, mla_g_cq, mla_g_ckv, mla_g_q,
           mla_g_k, nat_rpb, nat_g_q, nat_g_k, hg_lb_logits, hg_g_o, gla_w2, gla_b2, gla_g_o, w_br, w_merge,
           b_merge, w_out):
    B, n_lat, D = x.shape
    n_ctx = ctx.shape[1]
    depth = ada_w.shape[0]
    Nt = n_lat + n_ctx
    n_lat_blk = n_lat // TOK_BLK
    assert n_ctx == TOK_BLK and n_lat % NAT_PREP_BLK == 0 and Nt % MM_ROW_BLK == 0
    assert Nt % MLA_TK == 0 and n_lat % MLA_TQ == 0

    cc = jnp.zeros((8, D), F32).at[:B].set(c).at[B].set(c_ctx)
    ada = _ada_vectors(cc, ada_w.astype(BF16), ada_b).reshape(depth, 8, 3, D)
    sm = jax.nn.softmax(hg_lb_logits.astype(F32), axis=0)
    lower = jnp.maximum(jnp.cumsum(sm, axis=0) - sm[0], 0.0)
    tabs = _rope_tables(n_lat, n_ctx)

    x_lat, x_ctx = x, ctx
    for l in range(depth):
        need_ctx = l < depth - 1
        modsel = jnp.stack([ada[l, :B], jnp.broadcast_to(ada[l, B], (B, 3, D))], axis=1)
        x_tail = jnp.concatenate([x_lat[:, n_lat - (MM_ROW_BLK - n_ctx):n_lat], x_ctx], axis=1)
        p3, h = _norm_project(x_lat, x_tail, norm_g[l], modsel, _pack_w_in(w_in[l]), n_lat, Nt)

        mw = _mla_weights(mla_w_uq[l], mla_w_ukv[l], mla_g_cq[l], mla_g_ckv[l], mla_g_q[l], mla_g_k[l])
        y_a = _mla_attention(*_mla_prep(p3, mw, tabs), n_lat, need_ctx)
        y_b = _nat_attention(p3, nat_rpb[l], nat_g_q[l], nat_g_k[l], n_lat, need_ctx)

        lb = lower[l]
        lbvec = jnp.stack([jnp.log(lb), jnp.log1p(-lb), 1.0 - lb], axis=1)
        w2pad = jnp.zeros((2, LANES, GLA_HEADS * GLA_DK), F32)
        for d in range(2):
            w2pad = w2pad.at[d, GLA_LOWRANK * d:GLA_LOWRANK * (d + 1)].set(gla_w2[l, d])
        scans = _linear_scans(p3, lbvec, w2pad.astype(BF16), gla_b2[l].reshape(2, 1, -1), n_lat)

        wts = (w_br[l].astype(BF16), w_merge[l].astype(BF16), b_merge[l], w_out[l].astype(BF16), hg_g_o[l], gla_g_o[l])
        parts = [_merge(x_part, y_a[part], y_b[part], h, p3, scans, *wts, modsel, part, n_lat)
                 for part, x_part in enumerate((x_lat, x_ctx)) if part == 0 or need_ctx]
        x_lat, x_ctx = parts[0], (parts[1] if need_ctx else None)
    return x_lat
```

```python
import functools

import jax
import jax.numpy as jnp
import numpy as np
from jax import lax
from jax.experimental import pallas as pl
from jax.experimental.pallas import tpu as pltpu

F32 = jnp.float32
BF16 = jnp.bfloat16

GRID_W = 64
N_BRANCH = 4
BRANCH_W = 512
MLA_HEADS = 8
MLA_NOPE = 64
MLA_ROPE = 32
MLA_V = 64
MLA_QK = MLA_NOPE + MLA_ROPE
Q_LORA = 256
KV_LORA = 128
ROPE_BASE = 10000.0
NAT_HEADS = 8
NAT_DH = 64
NAT_KH = 8
NAT_KW = 16
HG_HEADS = 4
HG_DK = 128
GLA_HEADS = 4
GLA_DK = 64
GLA_LOWRANK = 16
GLA_NORMALIZER = 16.0
EPS = 1e-6

W_MLA = Q_LORA + KV_LORA + MLA_ROPE
W_NAT = 3 * BRANCH_W
W_HG = 4 * BRANCH_W
W_GLA = 2 * GLA_HEADS * GLA_DK + BRANCH_W + 2 * GLA_LOWRANK

LANES = 128
HEAD_PAD = 128
TOK_BLK = 256

COL_HG = 0
COL_Z = COL_HG + W_HG
COL_NAT = COL_Z + N_BRANCH * BRANCH_W
COL_GLA = COL_NAT + W_NAT
COL_MLA = COL_GLA + 1024
COL_GLR = COL_MLA + 512
IN_W_PAD = COL_GLR + LANES
MM_COL_BLK = IN_W_PAD // 3
MM_ROW_BLK = 768

SUB = 16
CHUNK = 64
EXP_CAP = 80.0
NEG_BIG = -1e30
LOG2E = 1.4426950408889634


def _cparams(sem, vmem_mb=48):
    return pltpu.CompilerParams(dimension_semantics=sem, vmem_limit_bytes=vmem_mb * 1024 * 1024)


def _dot(a, b):
    return jnp.dot(a, b, preferred_element_type=F32)


def _dot_nt(a, b):
    return lax.dot_general(a, b, (((1,), (1,)), ((), ())), preferred_element_type=F32)


def _sigmoid(x):
    return 0.5 * jnp.tanh(0.5 * x) + 0.5


def _ada_kernel(c_ref, w_ref, b_ref, o_ref):
    c = c_ref[...]
    a = (c * jax.nn.sigmoid(c)).astype(BF16)
    o_ref[...] = _dot(a, w_ref[...]) + b_ref[...]


def _ada_vectors(cc, ada_w, ada_b):
    L, D, D3 = ada_w.shape
    tn = 1024
    return pl.pallas_call(
        _ada_kernel,
        out_shape=jax.ShapeDtypeStruct((L, cc.shape[0], D3), F32),
        grid=(L, D3 // tn),
        in_specs=[
            pl.BlockSpec((cc.shape[0], D), lambda l, j: (0, 0)),
            pl.BlockSpec((None, D, tn), lambda l, j: (l, 0, j)),
            pl.BlockSpec((None, 1, tn), lambda l, j: (l, 0, j)),
        ],
        out_specs=pl.BlockSpec((None, cc.shape[0], tn), lambda l, j: (l, 0, j)),
        compiler_params=_cparams(("parallel", "parallel")),
        name="ada_vectors",
    )(cc, ada_w, ada_b.reshape(L, 1, D3))


def _norm_proj_kernel(x_ref, tail_ref, g_ref, m_ref, w_ref, p_ref, h_ref, hs_ref, *, n_lat):
    first_col = pl.program_id(2) == 0
    last_row_blk = pl.program_id(1) == pl.num_programs(1) - 1

    def modulated(x, scale, shift):
        y = x * lax.rsqrt(jnp.mean(x * x, axis=-1, keepdims=True) + EPS) * g_ref[...]
        h = (y * (1.0 + scale) + shift).astype(BF16)
        hs_ref[...] = h
        h_ref[...] = h

    @pl.when(first_col & jnp.logical_not(last_row_blk))
    def _():
        modulated(x_ref[...], m_ref[0, 1:2, :], m_ref[0, 0:1, :])

    @pl.when(first_col & last_row_blk)
    def _():
        rows = tail_ref.shape[0]
        row = lax.broadcasted_iota(jnp.int32, (rows, 1), 0) + pl.program_id(1) * rows
        is_ctx = row >= n_lat
        modulated(tail_ref[...], jnp.where(is_ctx, m_ref[1, 1:2, :], m_ref[0, 1:2, :]),
                  jnp.where(is_ctx, m_ref[1, 0:1, :], m_ref[0, 0:1, :]))

    p_ref[...] = _dot(hs_ref[...], w_ref[...]).astype(p_ref.dtype)


def _norm_project(x_main, x_tail, g, modsel, w, n_lat, Nt):
    B, _, D = x_main.shape
    N = w.shape[1]
    last_main = Nt // MM_ROW_BLK - 2
    return pl.pallas_call(
        functools.partial(_norm_proj_kernel, n_lat=n_lat),
        out_shape=(jax.ShapeDtypeStruct((B, Nt, N), BF16), jax.ShapeDtypeStruct((B, Nt, D), BF16)),
        grid=(B, Nt // MM_ROW_BLK, N // MM_COL_BLK),
        in_specs=[
            pl.BlockSpec((None, MM_ROW_BLK, D), lambda b, i, j: (b, jnp.minimum(i, last_main), 0)),
            pl.BlockSpec((None, MM_ROW_BLK, D), lambda b, i, j: (b, 0, 0)),
            pl.BlockSpec((1, D), lambda b, i, j: (0, 0)),
            pl.BlockSpec((None, 2, 3, D), lambda b, i, j: (b, 0, 0, 0)),
            pl.BlockSpec((D, MM_COL_BLK), lambda b, i, j: (0, j)),
        ],
        out_specs=(
            pl.BlockSpec((None, MM_ROW_BLK, MM_COL_BLK), lambda b, i, j: (b, i, j)),
            pl.BlockSpec((None, MM_ROW_BLK, D), lambda b, i, j: (b, i, 0)),
        ),
        scratch_shapes=[pltpu.VMEM((MM_ROW_BLK, D), BF16)],
        compiler_params=_cparams(("parallel", "parallel", "arbitrary")),
        name="norm_project",
    )(x_main, x_tail, g.reshape(1, D), modsel, w)


def _pack_w_in(w_in_l):
    o_nat = W_MLA
    o_hg = o_nat + W_NAT
    o_gla = o_hg + W_HG
    o_z = o_gla + W_GLA
    D = w_in_l.shape[0]
    zeros = lambda n: jnp.zeros((D, n), w_in_l.dtype)
    cols = [
        w_in_l[:, o_hg:o_hg + W_HG],
        w_in_l[:, o_z:o_z + N_BRANCH * BRANCH_W],
        w_in_l[:, o_nat:o_nat + W_NAT],
        w_in_l[:, o_gla:o_gla + 1024],
        w_in_l[:, :W_MLA], zeros(512 - W_MLA),
        w_in_l[:, o_gla + 1024:o_gla + W_GLA], zeros(LANES - 2 * GLA_LOWRANK),
    ]
    return jnp.concatenate(cols, axis=1).astype(BF16)


MLA_TQ = 1024
MLA_TK = 768


def _mla_prep_kernel(p_ref, wq_ref, wk_ref, wv_ref, gcq_ref, gckv_ref, gq_ref, gk_ref, vone_ref,
                     c_ref, s_ref, qc_ref, kc_ref, v_ref, *, n_lat_blk):
    ctx_w = jnp.where(pl.program_id(1) == n_lat_blk, 1.0, 0.0)
    hw = MLA_HEADS * HEAD_PAD
    p = p_ref[...].astype(F32)
    cq = p[:, :Q_LORA]
    r = lax.rsqrt(jnp.mean(cq * cq, axis=-1, keepdims=True) + EPS)
    cqn = (cq * r * gcq_ref[...]).astype(BF16)
    qraw = _dot(cqn, wq_ref[...])
    kv = p[:, Q_LORA:]
    ckv = kv[:, :KV_LORA]
    r2 = lax.rsqrt(jnp.mean(ckv * ckv, axis=-1, keepdims=True) + EPS)
    lane = lax.broadcasted_iota(jnp.int32, kv.shape, 1)
    lhs = (kv * jnp.where(lane < KV_LORA, r2, 1.0) * gckv_ref[...]).astype(BF16)
    kraw = _dot(lhs, wk_ref[...])
    vraw = _dot(lhs, wv_ref[...]) + vone_ref[...]
    cs = c_ref[...]
    sn = s_ref[...]
    inv = 1.0 / MLA_QK
    for hd in range(MLA_HEADS):
        sl = slice(HEAD_PAD * hd, HEAD_PAD * (hd + 1))
        sw = slice(hw + HEAD_PAD * hd, hw + HEAD_PAD * (hd + 1))
        qh = qraw[:, sl]
        rq = lax.rsqrt(jnp.sum(qh * qh, axis=-1, keepdims=True) * inv + EPS)
        qn = qh * rq * gq_ref[0:1, :]
        q_rot = qn * cs + qraw[:, sw] * rq * gq_ref[1:2, :] * sn
        qc_ref[hd] = jnp.concatenate([q_rot, qn], axis=1).astype(qc_ref.dtype)
        kh = kraw[:, sl]
        rk = lax.rsqrt(jnp.sum(kh * kh, axis=-1, keepdims=True) * inv + EPS)
        kn = kh * rk * gk_ref[0:1, :]
        k_rot = kn * cs + kraw[:, sw] * rk * gk_ref[1:2, :] * sn
        kc_ref[hd] = jnp.concatenate([k_rot * (1.0 - ctx_w), kn * ctx_w], axis=1).astype(kc_ref.dtype)
        v_ref[hd] = vraw[:, sl].astype(v_ref.dtype)


def _mla_weights(w_uq, w_ukv, g_cq, g_ckv, g_q, g_k):
    H = MLA_HEADS
    half = MLA_ROPE // 2
    lo, mid, hi = MLA_NOPE, MLA_NOPE + half, MLA_QK

    def swapped(t):
        z = jnp.zeros_like(t)
        return z.at[..., lo:mid].set(-t[..., mid:hi]).at[..., mid:hi].set(t[..., lo:mid])

    def both(t):
        return jnp.concatenate([t, swapped(t)], axis=1).reshape(t.shape[0], 2 * H * HEAD_PAD).astype(BF16)

    wq = jnp.pad(w_uq.reshape(Q_LORA, H, MLA_QK), ((0, 0), (0, 0), (0, HEAD_PAD - MLA_QK)))
    wkv = w_ukv.reshape(KV_LORA, H, MLA_NOPE + MLA_V)
    wk = jnp.pad(wkv[:, :, :MLA_NOPE], ((0, 0), (0, 0), (0, HEAD_PAD - MLA_NOPE)))
    place = jnp.zeros((MLA_ROPE, H, HEAD_PAD), F32)
    place = place.at[jnp.arange(MLA_ROPE), :, MLA_NOPE + jnp.arange(MLA_ROPE)].set(1.0)
    wk = jnp.concatenate([wk, place, jnp.zeros((256 - KV_LORA - MLA_ROPE, H, HEAD_PAD), F32)], axis=0)
    wv = jnp.pad(wkv[:, :, MLA_NOPE:], ((0, 256 - KV_LORA), (0, 0), (0, HEAD_PAD - MLA_V)))
    wv = wv.reshape(256, H * HEAD_PAD).astype(BF16)
    vone = jnp.zeros((H, HEAD_PAD), F32).at[:, MLA_V].set(1.0).reshape(1, H * HEAD_PAD)
    gckv = jnp.concatenate([g_ckv, jnp.ones((256 - KV_LORA,), F32)]).reshape(1, 256)
    gq = jnp.pad(g_q * (MLA_QK ** -0.5 * LOG2E), (0, HEAD_PAD - MLA_QK))
    gk = jnp.pad(g_k, (0, HEAD_PAD - MLA_QK))
    unsign = jnp.where(jnp.arange(HEAD_PAD) < mid, -1.0, 1.0)
    gq2 = jnp.stack([gq, swapped(gq) * unsign], axis=0)
    gk2 = jnp.stack([gk, swapped(gk) * unsign], axis=0)
    return both(wq), both(wk), wv, g_cq.reshape(1, Q_LORA), gckv, gq2, gk2, vone


def _rope_tables(n_lat, n_ctx):
    quarter = MLA_ROPE // 4
    inv_freq = ROPE_BASE ** (-jnp.arange(quarter, dtype=F32) / quarter)
    t = jnp.arange(n_lat, dtype=jnp.int32)
    row = (t // GRID_W).astype(F32)
    col = (t % GRID_W).astype(F32)
    ang = jnp.concatenate([row[:, None] * inv_freq, col[:, None] * inv_freq], axis=-1)
    cos, sin = jnp.cos(ang), jnp.sin(ang)
    ones = jnp.ones((n_lat, MLA_NOPE), F32)
    tail = jnp.ones((n_lat, HEAD_PAD - MLA_QK), F32)
    zl = jnp.zeros((n_lat, MLA_NOPE), F32)
    zt = jnp.zeros((n_lat, HEAD_PAD - MLA_QK), F32)
    c = jnp.concatenate([ones, cos, cos, tail], axis=1)
    s = jnp.concatenate([zl, sin, sin, zt], axis=1)
    c = jnp.concatenate([c, jnp.ones((n_ctx, HEAD_PAD), F32)], axis=0)
    s = jnp.concatenate([s, jnp.zeros((n_ctx, HEAD_PAD), F32)], axis=0)
    return c, s


def _mla_prep(p3, mw, tabs):
    B, Nt, _ = p3.shape
    H = MLA_HEADS
    wq, wk, wv, gcq, gckv, gq, gk, vone = mw
    n_lat_blk = Nt // TOK_BLK - 1
    const = lambda shape: pl.BlockSpec(shape, lambda b, i: (0,) * len(shape))
    tab = pl.BlockSpec((TOK_BLK, HEAD_PAD), lambda b, i: (i, 0))
    out = lambda w: pl.BlockSpec((None, H, TOK_BLK, w), lambda b, i: (b, 0, i, 0))
    shp = lambda w: jax.ShapeDtypeStruct((B, H, Nt, w), BF16)
    return pl.pallas_call(
        functools.partial(_mla_prep_kernel, n_lat_blk=n_lat_blk),
        out_shape=(shp(2 * HEAD_PAD), shp(2 * HEAD_PAD), shp(HEAD_PAD)),
        grid=(B, Nt // TOK_BLK),
        in_specs=[
            pl.BlockSpec((None, TOK_BLK, 512), lambda b, i: (b, i, COL_MLA // 512)),
            const(wq.shape), const(wk.shape), const(wv.shape), const(gcq.shape), const(gckv.shape),
            const(gq.shape), const(gk.shape), const(vone.shape), tab, tab,
        ],
        out_specs=(out(2 * HEAD_PAD), out(2 * HEAD_PAD), out(HEAD_PAD)),
        compiler_params=_cparams(("parallel", "parallel")),
        name="mla_prep",
    )(p3, wq, wk, wv, gcq, gckv, gq, gk, vone, *tabs)


def _softmax_step(q, kc, vc, m, acc):
    s = _dot_nt(q, kc)
    m_new = jnp.maximum(m, jnp.max(s, axis=-1, keepdims=True))
    p = jnp.exp2((s - m_new).astype(BF16))
    acc = jnp.exp2(m - m_new) * acc + _dot(p, vc)
    return m_new, acc


def _mla_attn_kernel(qc_ref, kc_ref, v_ref, o_ref, *, n_chunks):
    tq = qc_ref.shape[1]

    def body(c, carry):
        off = pl.multiple_of(c * MLA_TK, MLA_TK)
        return tuple(
            _softmax_step(qc_ref[hh], kc_ref[hh, pl.ds(off, MLA_TK), :], v_ref[hh, pl.ds(off, MLA_TK), :], *carry[hh])
            for hh in range(2))

    m0 = jnp.full((tq, 1), NEG_BIG, F32)
    a0 = jnp.zeros((tq, HEAD_PAD), F32)
    carry = lax.fori_loop(0, n_chunks, body, ((m0, a0), (m0, a0)), unroll=True)
    outs = [acc[:, :MLA_V] / acc[:, MLA_V:MLA_V + 1] for _, acc in carry]
    o_ref[...] = jnp.concatenate(outs, axis=-1).astype(o_ref.dtype)


def _mla_ctx_kernel(qc_ref, kc_ref, v_ref, o_ref):
    outs = []
    for hh in range(2):
        s = _dot_nt(qc_ref[hh], kc_ref[hh])
        p = jnp.exp2(s - jnp.max(s, axis=-1, keepdims=True))
        acc = _dot(p.astype(BF16), v_ref[hh])
        outs.append(acc[:, :MLA_V] / acc[:, MLA_V:MLA_V + 1])
    o_ref[...] = jnp.concatenate(outs, axis=-1).astype(o_ref.dtype)


def _mla_attention(qc, kc, v, n_lat, need_ctx):
    B, H, Nt, _ = kc.shape
    n_ctx = Nt - n_lat
    o_lat = pl.pallas_call(
        functools.partial(_mla_attn_kernel, n_chunks=Nt // MLA_TK),
        out_shape=jax.ShapeDtypeStruct((B, n_lat, BRANCH_W), BF16),
        grid=(B, H // 2, n_lat // MLA_TQ),
        in_specs=[
            pl.BlockSpec((None, 2, MLA_TQ, 2 * HEAD_PAD), lambda b, g, i: (b, g, i, 0)),
            pl.BlockSpec((None, 2, Nt, 2 * HEAD_PAD), lambda b, g, i: (b, g, 0, 0)),
            pl.BlockSpec((None, 2, Nt, HEAD_PAD), lambda b, g, i: (b, g, 0, 0)),
        ],
        out_specs=pl.BlockSpec((None, MLA_TQ, 2 * MLA_V), lambda b, g, i: (b, i, g)),
        compiler_params=_cparams(("parallel", "parallel", "arbitrary")),
        name="mla_attention",
    )(qc, kc, v)
    if not need_ctx:
        return o_lat, None
    cblk = n_lat // n_ctx
    cspec = lambda w: pl.BlockSpec((None, 2, n_ctx, w), lambda b, g: (b, g, cblk, 0))
    o_ctx = pl.pallas_call(
        _mla_ctx_kernel,
        out_shape=jax.ShapeDtypeStruct((B, n_ctx, BRANCH_W), BF16),
        grid=(B, H // 2),
        in_specs=[cspec(2 * HEAD_PAD), cspec(2 * HEAD_PAD), cspec(HEAD_PAD)],
        out_specs=pl.BlockSpec((None, n_ctx, 2 * MLA_V), lambda b, g: (b, 0, g)),
        compiler_params=_cparams(("parallel", "parallel")),
        name="mla_attention_ctx",
    )(qc, kc, v)
    return o_lat, o_ctx


NAT_ROWS_BLK = 8
NAT_TQ = NAT_ROWS_BLK * GRID_W
NAT_WIN_ROWS = 16
NAT_WIN = NAT_WIN_ROWS * GRID_W
NAT_PREP_BLK = 1024


def _pair_norm(x, g, half0):
    x2 = x * x
    s0 = jnp.sum(jnp.where(half0, x2, 0.0), axis=-1, keepdims=True)
    s1 = jnp.sum(jnp.where(half0, 0.0, x2), axis=-1, keepdims=True)
    ms = jnp.where(half0, s0, s1) * (1.0 / NAT_DH)
    return x * lax.rsqrt(ms + EPS) * g


def _nat_kernel(q_ref, kl_ref, vl_ref, kc_ref, vc_ref, bias_ref, gq_ref, gk_ref, o_ref, kn_ref, vb_ref,
                *, n_lat, rows_total):
    rb = pl.program_id(2)
    half0 = lax.broadcasted_iota(jnp.int32, (1, LANES), 1) < NAT_DH

    @pl.when(rb == 0)
    def _():
        def body(c, carry):
            off = pl.multiple_of(c * NAT_PREP_BLK, NAT_PREP_BLK)
            kn_ref[pl.ds(off, NAT_PREP_BLK), :] = _pair_norm(
                kl_ref[pl.ds(off, NAT_PREP_BLK), :].astype(F32), gk_ref[...], half0).astype(BF16)
            vb_ref[pl.ds(off, NAT_PREP_BLK), :LANES] = vl_ref[pl.ds(off, NAT_PREP_BLK), :].astype(BF16)
            vb_ref[pl.ds(off, NAT_PREP_BLK), LANES:] = jnp.ones((NAT_PREP_BLK, LANES), BF16)
            return carry

        lax.fori_loop(0, n_lat // NAT_PREP_BLK, body, 0)
        kn_ref[n_lat:, :] = _pair_norm(kc_ref[...].astype(F32), gk_ref[...], half0).astype(BF16)
        vb_ref[n_lat:, :LANES] = vc_ref[...].astype(BF16)
        vb_ref[n_lat:, LANES:] = jnp.ones((vc_ref.shape[0], LANES), BF16)

    q = _pair_norm(q_ref[...].astype(F32), gq_ref[...], half0)
    ws = jnp.clip(rb * NAT_ROWS_BLK - NAT_KH // 2, 0, rows_total - NAT_WIN_ROWS)
    off = pl.multiple_of(ws * GRID_W, GRID_W)
    kw = kn_ref[pl.ds(off, NAT_WIN), :]
    vw = vb_ref[pl.ds(off, NAT_WIN), :]
    kc = kn_ref[n_lat:, :]
    vc = vb_ref[n_lat:, :]
    qms = [jnp.where(half0 if hh == 0 else jnp.logical_not(half0), q, 0.0).astype(BF16) for hh in range(2)]
    sws = [_dot_nt(qm, kw) + bias_ref[hh].astype(F32) for hh, qm in enumerate(qms)]
    scs = [_dot_nt(qm, kc) for qm in qms]
    ms = [jnp.maximum(jnp.max(sw, axis=-1, keepdims=True), jnp.max(sc, axis=-1, keepdims=True))
          for sw, sc in zip(sws, scs)]
    accs = [_dot(jnp.exp2((sw - m).astype(BF16)), vw) + _dot(jnp.exp2((sc - m).astype(BF16)), vc)
            for sw, sc, m in zip(sws, scs, ms)]
    outs = [acc[:, :LANES] / acc[:, LANES:] for acc in accs]
    o_ref[...] = jnp.where(half0, outs[0], outs[1]).astype(o_ref.dtype)


def _nat_ctx_kernel(q_ref, kc_ref, vc_ref, gq_ref, gk_ref, o_ref):
    half0 = lax.broadcasted_iota(jnp.int32, (1, LANES), 1) < NAT_DH
    q = _pair_norm(q_ref[...].astype(F32), gq_ref[...], half0)
    kc = _pair_norm(kc_ref[...].astype(F32), gk_ref[...], half0).astype(BF16)
    vc = vc_ref[...].astype(BF16)
    outs = []
    for hh in range(2):
        sel = half0 if hh == 0 else jnp.logical_not(half0)
        s = _dot_nt(jnp.where(sel, q, 0.0).astype(BF16), kc)
        p = jnp.exp2(s - jnp.max(s, axis=-1, keepdims=True))
        outs.append(_dot(p.astype(BF16), vc) / jnp.sum(p, axis=-1, keepdims=True))
    o_ref[...] = jnp.where(half0, outs[0], outs[1]).astype(o_ref.dtype)


def _nat_bias_tables(rpb, rows_total):
    rbs = np.array([0, min(NAT_ROWS_BLK, rows_total - NAT_ROWS_BLK), rows_total - NAT_ROWS_BLK])
    ws = np.clip(rbs - NAT_KH // 2, 0, rows_total - NAT_WIN_ROWS)
    qrow = rbs[:, None] + np.arange(NAT_ROWS_BLK)[None, :]
    r0 = np.clip(qrow - NAT_KH // 2, 0, rows_total - NAT_KH)
    krow = ws[:, None] + np.arange(NAT_WIN_ROWS)[None, :]
    row_ok = (krow[:, None, :] >= r0[:, :, None]) & (krow[:, None, :] < r0[:, :, None] + NAT_KH)
    dr = np.clip(krow[:, None, :] - qrow[:, :, None] + NAT_KH - 1, 0, 2 * NAT_KH - 2)
    col = np.arange(GRID_W)
    c0 = np.clip(col - NAT_KW // 2, 0, GRID_W - NAT_KW)
    col_ok = (col[None, :] >= c0[:, None]) & (col[None, :] < c0[:, None] + NAT_KW)
    dc = np.clip(col[None, :] - col[:, None], -(NAT_KW - 1), NAT_KW - 1) + NAT_KW - 1
    oh_c = np.eye(2 * NAT_KW - 1, dtype=np.float32)[dc]
    by_col = jnp.einsum('hrc,qkc->hrqk', rpb.astype(F32), oh_c, precision=lax.Precision.HIGHEST)
    by_col = jnp.where(jnp.asarray(col_ok), by_col * LOG2E, NEG_BIG).astype(BF16)
    masked = jnp.full(by_col[:, 0].shape, NEG_BIG, BF16)
    variants = []
    for v in range(3):
        rows = [jnp.concatenate([by_col[:, dr[v, i, j]] if row_ok[v, i, j] else masked
                                 for j in range(NAT_WIN_ROWS)], axis=-1) for i in range(NAT_ROWS_BLK)]
        variants.append(jnp.concatenate(rows, axis=-2))
    return jnp.stack(variants).reshape(3, NAT_HEADS // 2, 2, NAT_TQ, NAT_WIN)


def _nat_attention(p3, rpb, g_q, g_k, n_lat, need_ctx):
    B, Nt, _ = p3.shape
    n_ctx = Nt - n_lat
    rows_total = n_lat // GRID_W
    n_rb = rows_total // NAT_ROWS_BLK
    bias = _nat_bias_tables(rpb, rows_total)
    gq = jnp.tile(g_q * (NAT_DH ** -0.5 * LOG2E), 2).reshape(1, LANES)
    gk = jnp.tile(g_k, 2).reshape(1, LANES)
    cq, ck, cv = ((COL_NAT + j * BRANCH_W) // LANES for j in range(3))
    cblk = n_lat // n_ctx

    def variant(rb):
        return jnp.where(rb == 0, 0, jnp.where(rb == n_rb - 1, 2, 1))

    vec = pl.BlockSpec((1, LANES), lambda b, g, rb: (0, 0))
    o_lat = pl.pallas_call(
        functools.partial(_nat_kernel, n_lat=n_lat, rows_total=rows_total),
        out_shape=jax.ShapeDtypeStruct((B, n_lat, BRANCH_W), BF16),
        grid=(B, NAT_HEADS // 2, n_rb),
        in_specs=[
            pl.BlockSpec((None, NAT_TQ, LANES), lambda b, g, rb: (b, rb, cq + g)),
            pl.BlockSpec((None, n_lat, LANES), lambda b, g, rb: (b, 0, ck + g)),
            pl.BlockSpec((None, n_lat, LANES), lambda b, g, rb: (b, 0, cv + g)),
            pl.BlockSpec((None, n_ctx, LANES), lambda b, g, rb: (b, cblk, ck + g)),
            pl.BlockSpec((None, n_ctx, LANES), lambda b, g, rb: (b, cblk, cv + g)),
            pl.BlockSpec((None, None, 2, NAT_TQ, NAT_WIN), lambda b, g, rb: (variant(rb), g, 0, 0, 0)),
            vec, vec,
        ],
        out_specs=pl.BlockSpec((None, NAT_TQ, LANES), lambda b, g, rb: (b, rb, g)),
        scratch_shapes=[pltpu.VMEM((Nt, LANES), BF16), pltpu.VMEM((Nt, 2 * LANES), BF16)],
        compiler_params=_cparams(("parallel", "parallel", "arbitrary")),
        name="nat_attention",
    )(p3, p3, p3, p3, p3, bias, gq, gk)
    if not need_ctx:
        return o_lat, None
    vec2 = pl.BlockSpec((1, LANES), lambda b, g: (0, 0))
    o_ctx = pl.pallas_call(
        _nat_ctx_kernel,
        out_shape=jax.ShapeDtypeStruct((B, n_ctx, BRANCH_W), BF16),
        grid=(B, NAT_HEADS // 2),
        in_specs=[
            pl.BlockSpec((None, n_ctx, LANES), lambda b, g: (b, cblk, cq + g)),
            pl.BlockSpec((None, n_ctx, LANES), lambda b, g: (b, cblk, ck + g)),
            pl.BlockSpec((None, n_ctx, LANES), lambda b, g: (b, cblk, cv + g)),
            vec2, vec2,
        ],
        out_specs=pl.BlockSpec((None, n_ctx, LANES), lambda b, g: (b, 0, g)),
        compiler_params=_cparams(("parallel", "parallel")),
        name="nat_attention_ctx",
    )(p3, p3, p3, gq, gk)
    return o_lat, o_ctx


def _log_sigmoid(x):
    return jnp.minimum(x, 0.0) - jnp.log(1.0 + jnp.exp(-jnp.abs(x)))


def _run_block(operands, s_ref, o_ref, rev):
    n_ch, nsub = o_ref.shape[0] // CHUNK, CHUNK // SUB
    r64 = lax.broadcasted_iota(jnp.int32, (CHUNK, CHUNK), 0)
    c64 = lax.broadcasted_iota(jnp.int32, (CHUNK, CHUNK), 1)
    causal = (r64 <= c64) if rev else (r64 >= c64)
    tri = jnp.where(causal, 1.0, 0.0).astype(BF16)
    first = [SUB * i + (SUB - 1 if rev else 0) for i in range(nsub)]
    last = [SUB * i + (0 if rev else SUB - 1) for i in range(nsub)]
    loaded = []
    for c in range(n_ch):
        qs, k, v, a, heads = operands(slice(CHUNK * c, CHUNK * (c + 1)))
        hi = a.astype(BF16)
        mid = (a - hi.astype(F32)).astype(BF16)
        loaded.append((qs, k, v, a, _dot(tri, hi) + _dot(tri, mid)))
        yield
    staged = []
    for c in range(n_ch):
        qs, k, v, a, bc = loaded[c]
        W = qs.shape[1]
        bex = bc - a
        r_start = [bex[r:r + 1, :] for r in first]
        r_end = [bc[r:r + 1, :] for r in last]
        bend = r_end[0] if rev else r_end[nsub - 1]

        def spread(rows):
            return jnp.concatenate([jnp.broadcast_to(r, (SUB, W)) for r in rows], axis=0)

        rs_full = spread(r_start)
        qt = qs * jnp.exp(bc - rs_full)
        q_in = qt * spread([jnp.exp(r) for r in r_start])
        ke = k * jnp.exp(spread(r_end) - bc)
        k_out = ke * spread([jnp.exp(bend - r) for r in r_end])
        kd = k * jnp.exp(jnp.minimum(rs_full - bc, EXP_CAP))
        zeros = jnp.zeros((SUB, W), F32)
        kts = []
        for i in range(nsub):
            rows = []
            for j in range(nsub):
                blk = slice(SUB * j, SUB * (j + 1))
                if j == i:
                    rows.append(kd[blk])
                elif (j > i) if rev else (j < i):
                    rows.append(ke[blk] if abs(i - j) == 1 else ke[blk] * jnp.exp(r_start[i] - r_end[j]))
                else:
                    rows.append(zeros)
            kts.append(jnp.concatenate(rows, axis=0).astype(BF16))
        vb = v.astype(BF16)
        intra, incr, q_state = [], [], []
        for ks, kmask, vs in heads:
            ksl = slice(ks, ks + LANES)
            vsl = slice(vs, vs + LANES)

            def msk(t, kmask=kmask):
                return t if kmask is None else jnp.where(kmask, t, 0.0)

            qt_h = msk(qt[:, ksl]).astype(BF16)
            q_cat = jnp.concatenate([jnp.where(r64[:, :1] // SUB == i, qt_h, jnp.zeros_like(qt_h))
                                     for i in range(nsub)], axis=1)
            k_cat = jnp.concatenate([kts[i][:, ksl] for i in range(nsub)], axis=1)
            sc = jnp.where(causal, _dot_nt(q_cat, k_cat), 0.0).astype(BF16)
            intra.append(_dot(sc, vb[:, vsl]))
            incr.append(_dot(v[:, vsl].T.astype(BF16), msk(k_out[:, ksl]).astype(BF16)))
            q_state.append(msk(q_in[:, ksl]).astype(BF16))
        staged.append((intra, incr, q_state, jnp.exp(bend)))
        yield
    states = [s_ref[hd] for hd in range(len(heads))]
    for c in (range(n_ch - 1, -1, -1) if rev else range(n_ch)):
        intra, incr, q_state, dec = staged[c]
        outs = []
        for hd, (ks, _, _) in enumerate(heads):
            outs.append(intra[hd] + _dot_nt(q_state[hd], states[hd].astype(BF16)))
            states[hd] = states[hd] * dec[:, ks:ks + LANES] + incr[hd]
        o_ref[CHUNK * c:CHUNK * (c + 1), :] = jnp.concatenate(outs, axis=-1).astype(o_ref.dtype)
        yield
    for hd, st in enumerate(states):
        s_ref[hd] = st


def _hgrn2_operands(q_ref, f_ref, v_ref, lb, rows):
    q = q_ref[rows, :].astype(F32)
    f = f_ref[rows, :].astype(F32)
    loglb, log1mlb, oneml = lb[0:1, :], lb[1:2, :], lb[2:3, :]
    u = log1mlb + _log_sigmoid(f)
    a = jnp.maximum(u, loglb) + jnp.log(1.0 + jnp.exp(-jnp.abs(u - loglb)))
    k = oneml * _sigmoid(-f)
    qs = q * _sigmoid(q) * (HG_DK ** -0.5)
    heads = [(LANES * h, None, LANES * h) for h in range(HG_HEADS)]
    return qs, k, v_ref[rows, :].astype(F32), a, heads


def _gla_operands(q_ref, k_ref, v_ref, r_ref, w2, b2, rows):
    g = _dot(r_ref[rows, :].astype(BF16), w2) + b2
    a = _log_sigmoid(g) * (1.0 / GLA_NORMALIZER)
    qs = q_ref[rows, :].astype(F32) * (GLA_DK ** -0.5)
    half0 = lax.broadcasted_iota(jnp.int32, (1, LANES), 1) < GLA_DK
    heads = [(LANES * (h // 2), half0 if h % 2 == 0 else jnp.logical_not(half0), LANES * h)
             for h in range(GLA_HEADS)]
    return qs, k_ref[rows, :].astype(F32), v_ref[rows, :].astype(F32), a, heads


def _scans_kernel(hq_f, hf_f, hv_f, hq_b, hf_b, hv_b, lb_ref, gq_f, gk_f, gv_f, gr_f, gq_b, gk_b, gv_b, gr_b,
                  w2_ref, b2_ref, oh_f, oh_b, og_f, og_b, s_ref):
    @pl.when(pl.program_id(1) == 0)
    def _():
        s_ref[...] = jnp.zeros_like(s_ref)

    streams = [
        _run_block(functools.partial(_hgrn2_operands, hq_f, hf_f, hv_f, lb_ref[0]), s_ref.at[0], oh_f, False),
        _run_block(functools.partial(_hgrn2_operands, hq_b, hf_b, hv_b, lb_ref[1]), s_ref.at[1], oh_b, True),
        _run_block(functools.partial(_gla_operands, gq_f, gk_f, gv_f, gr_f, w2_ref[0], b2_ref[0]),
                   s_ref.at[2], og_f, False),
        _run_block(functools.partial(_gla_operands, gq_b, gk_b, gv_b, gr_b, w2_ref[1], b2_ref[1]),
                   s_ref.at[3], og_b, True),
    ]
    done = object()
    while streams:
        streams = [g for g in streams if next(g, done) is not done]


def _linear_scans(p3, lbvec, w2pad, b2, n_lat):
    B, Nt, _ = p3.shape
    nb = Nt // TOK_BLK
    n_lat_blk = n_lat // TOK_BLK
    fwd = lambda s: jnp.where(s == 0, n_lat_blk, s - 1)
    bwd = lambda s: jnp.where(s == 0, n_lat_blk, n_lat_blk - s)
    wqk = GLA_HEADS * GLA_DK
    spec = lambda blk, w, col: pl.BlockSpec((None, TOK_BLK, w), lambda b, s: (b, blk(s), col // w))
    hg = lambda blk, f_col: [spec(blk, BRANCH_W, COL_HG), spec(blk, BRANCH_W, f_col),
                             spec(blk, BRANCH_W, COL_HG + 3 * BRANCH_W)]
    gla = lambda blk: [spec(blk, wqk, COL_GLA), spec(blk, wqk, COL_GLA + wqk), spec(blk, BRANCH_W, COL_GLA + 2 * wqk),
                       spec(blk, LANES, COL_GLR)]
    const = lambda shape: pl.BlockSpec(shape, lambda b, s: (0,) * len(shape))
    out = lambda blk: pl.BlockSpec((None, TOK_BLK, BRANCH_W), lambda b, s: (b, blk(s), 0))
    shp = jax.ShapeDtypeStruct((B, Nt, BRANCH_W), BF16)
    return pl.pallas_call(
        _scans_kernel,
        out_shape=(shp, shp, shp, shp),
        grid=(B, nb),
        in_specs=(hg(fwd, COL_HG + BRANCH_W) + hg(bwd, COL_HG + 2 * BRANCH_W) + [const(lbvec.shape)]
                  + gla(fwd) + gla(bwd) + [const(w2pad.shape), const(b2.shape)]),
        out_specs=(out(fwd), out(bwd), out(fwd), out(bwd)),
        scratch_shapes=[pltpu.VMEM((4, HG_HEADS, LANES, LANES), F32)],
        compiler_params=_cparams(("parallel", "arbitrary")),
        name="linear_scans",
    )(*([p3] * 6), lbvec, *([p3] * 8), w2pad, b2)


def _group_norm(o, g):
    parts = []
    for hd in range(o.shape[1] // LANES):
        oh = o[:, LANES * hd:LANES * (hd + 1)]
        parts.append(oh * lax.rsqrt(jnp.mean(oh * oh, axis=-1, keepdims=True) + EPS))
    return jnp.concatenate(parts, axis=-1) * g


def _merge_kernel(x_ref, h_ref, za_ref, zb_ref, zc_ref, zd_ref, ya_ref, yb_ref, hf_ref, hb_ref, gf_ref,
                  gb_ref, wbr_ref, wmg_ref, bmg_ref, wout_ref, ghg_ref, ggl_ref, m_ref, o_ref):
    h = h_ref[...]
    ys = (
        ya_ref[...].astype(F32),
        yb_ref[...].astype(F32),
        _group_norm(hf_ref[...].astype(F32) + hb_ref[...].astype(F32), ghg_ref[...]),
        _group_norm(gf_ref[...].astype(F32) + gb_ref[...].astype(F32), ggl_ref[...]),
    )
    zs = (za_ref, zb_ref, zc_ref, zd_ref)
    acc = None
    for br in range(N_BRANCH):
        gate = _sigmoid(_dot(h, wmg_ref[br]) + bmg_ref[br])
        z = zs[br][...].astype(F32)
        yz = (ys[br] * (z * _sigmoid(z))).astype(BF16)
        part = gate * _dot(yz, wbr_ref[br])
        acc = part if acc is None else acc + part
    out = _dot(acc.astype(BF16), wout_ref[...])
    o_ref[...] = x_ref[...] + m_ref[2:3, :] * out


MERGE_LAT_BLK = 512


def _merge(x_part, y_a, y_b, h, p3, scans, w_br, w_merge, b_merge, w_out, g_hg, g_gla, modsel, part, n_lat):
    B, _, D = h.shape
    tm = TOK_BLK if part else MERGE_LAT_BLK
    n_blk = 1 if part else n_lat // tm
    blk0 = n_lat // tm if part else 0
    own = lambda w: pl.BlockSpec((None, tm, w), lambda b, i: (b, i, 0))
    stream = lambda w, col=0: pl.BlockSpec((None, tm, w), lambda b, i: (b, blk0 + i, col // w))
    once = pl.Buffered(1)
    const = lambda shape: pl.BlockSpec(shape, lambda b, i: (0,) * len(shape), pipeline_mode=once)
    ghg = jnp.tile(g_hg, HG_HEADS).reshape(1, BRANCH_W)
    ggl = jnp.tile(g_gla, GLA_HEADS).reshape(1, BRANCH_W)
    bm = b_merge.reshape(N_BRANCH, 1, D)
    return pl.pallas_call(
        _merge_kernel,
        out_shape=jax.ShapeDtypeStruct((B, n_blk * tm, D), F32),
        grid=(B, n_blk),
        in_specs=[own(D), stream(D)] + [stream(BRANCH_W, COL_Z + br * BRANCH_W) for br in range(N_BRANCH)]
        + [own(BRANCH_W)] * 2 + [stream(BRANCH_W)] * 4 + [
            const(w_br.shape), const(w_merge.shape), const(bm.shape), const(w_out.shape),
            const(ghg.shape), const(ggl.shape),
            pl.BlockSpec((None, None, 3, D), lambda b, i: (b, part, 0, 0)),
        ],
        out_specs=own(D),
        compiler_params=_cparams(("parallel", "parallel"), vmem_mb=56),
        name="branch_merge_ctx" if part else "branch_merge",
    )(x_part, h, p3, p3, p3, p3, y_a, y_b, *scans, w_br, w_merge, bm, w_out, ghg, ggl, modsel)


def kernel(x, c, ctx, c_ctx, ada_w, ada_b, norm_g, w_in, mla_w_uq, mla_w_ukv, mla_g_cq, mla_g_ckv, mla_g_q,
           mla_g_k, nat_rpb, nat_g_q, nat_g_k, hg_lb_logits, hg_g_o, gla_w2, gla_b2, gla_g_o, w_br, w_merge,
           b_merge, w_out):
    B, n_lat, D = x.shape
    n_ctx = ctx.shape[1]
    depth = ada_w.shape[0]
    Nt = n_lat + n_ctx
    n_lat_blk = n_lat // TOK_BLK
    assert n_ctx == TOK_BLK and n_lat % NAT_PREP_BLK == 0 and Nt % MM_ROW_BLK == 0
    assert Nt % MLA_TK == 0 and n_lat % MLA_TQ == 0

    cc = jnp.zeros((8, D), F32).at[:B].set(c).at[B].set(c_ctx)
    ada = _ada_vectors(cc, ada_w.astype(BF16), ada_b).reshape(depth, 8, 3, D)
    sm = jax.nn.softmax(hg_lb_logits.astype(F32), axis=0)
    lower = jnp.maximum(jnp.cumsum(sm, axis=0) - sm[0], 0.0)
    tabs = _rope_tables(n_lat, n_ctx)

    x_lat, x_ctx = x, ctx
    for l in range(depth):
        need_ctx = l < depth - 1
        modsel = jnp.stack([ada[l, :B], jnp.broadcast_to(ada[l, B], (B, 3, D))], axis=1)
        x_tail = jnp.concatenate([x_lat[:, n_lat - (MM_ROW_BLK - n_ctx):n_lat], x_ctx], axis=1)
        p3, h = _norm_project(x_lat, x_tail, norm_g[l], modsel, _pack_w_in(w_in[l]), n_lat, Nt)

        mw = _mla_weights(mla_w_uq[l], mla_w_ukv[l], mla_g_cq[l], mla_g_ckv[l], mla_g_q[l], mla_g_k[l])
        y_a = _mla_attention(*_mla_prep(p3, mw, tabs), n_lat, need_ctx)
        y_b = _nat_attention(p3, nat_rpb[l], nat_g_q[l], nat_g_k[l], n_lat, need_ctx)

        lb = lower[l]
        lbvec = jnp.stack([jnp.log(lb), jnp.log1p(-lb), 1.0 - lb], axis=1)
        w2pad = jnp.zeros((2, LANES, GLA_HEADS * GLA_DK), F32)
        for d in range(2):
            w2pad = w2pad.at[d, GLA_LOWRANK * d:GLA_LOWRANK * (d + 1)].set(gla_w2[l, d])
        scans = _linear_scans(p3, lbvec, w2pad.astype(BF16), gla_b2[l].reshape(2, 1, -1), n_lat)

        wts = (w_br[l].astype(BF16), w_merge[l].astype(BF16), b_merge[l], w_out[l].astype(BF16), hg_g_o[l], gla_g_o[l])
        parts = [_merge(x_part, y_a[part], y_b[part], h, p3, scans, *wts, modsel, part, n_lat)
                 for part, x_part in enumerate((x_lat, x_ctx)) if part == 0 or need_ctx]
        x_lat, x_ctx = parts[0], (parts[1] if need_ctx else None)
    return x_lat
```

```python
import functools

import jax
import jax.numpy as jnp
import numpy as np
from jax import lax
from jax.experimental import pallas as pl
from jax.experimental.pallas import tpu as pltpu

F32 = jnp.float32
BF16 = jnp.bfloat16

GRID_W = 64
N_BRANCH = 4
BRANCH_W = 512
MLA_HEADS = 8
MLA_NOPE = 64
MLA_ROPE = 32
MLA_V = 64
MLA_QK = MLA_NOPE + MLA_ROPE
Q_LORA = 256
KV_LORA = 128
ROPE_BASE = 10000.0
NAT_HEADS = 8
NAT_DH = 64
NAT_KH = 8
NAT_KW = 16
HG_HEADS = 4
HG_DK = 128
GLA_HEADS = 4
GLA_DK = 64
GLA_LOWRANK = 16
GLA_NORMALIZER = 16.0
EPS = 1e-6

W_MLA = Q_LORA + KV_LORA + MLA_ROPE
W_NAT = 3 * BRANCH_W
W_HG = 4 * BRANCH_W
W_GLA = 2 * GLA_HEADS * GLA_DK + BRANCH_W + 2 * GLA_LOWRANK

LANES = 128
HEAD_PAD = 128
TOK_BLK = 256

COL_HG = 0
COL_Z = COL_HG + W_HG
COL_NAT = COL_Z + N_BRANCH * BRANCH_W
COL_GLA = COL_NAT + W_NAT
COL_MLA = COL_GLA + 1024
COL_GLR = COL_MLA + 512
IN_W_PAD = COL_GLR + LANES
MM_COL_BLK = IN_W_PAD // 3
MM_ROW_BLK = 768

SUB = 16
CHUNK = 64
EXP_CAP = 80.0
NEG_BIG = -1e30
LOG2E = 1.4426950408889634


def _cparams(sem, vmem_mb=48):
    return pltpu.CompilerParams(dimension_semantics=sem, vmem_limit_bytes=vmem_mb * 1024 * 1024)


def _dot(a, b):
    return jnp.dot(a, b, preferred_element_type=F32)


def _dot_nt(a, b):
    return lax.dot_general(a, b, (((1,), (1,)), ((), ())), preferred_element_type=F32)


def _sigmoid(x):
    return 0.5 * jnp.tanh(0.5 * x) + 0.5


def _ada_kernel(c_ref, w_ref, b_ref, o_ref):
    c = c_ref[...]
    a = (c * jax.nn.sigmoid(c)).astype(BF16)
    o_ref[...] = _dot(a, w_ref[...]) + b_ref[...]


def _ada_vectors(cc, ada_w, ada_b):
    L, D, D3 = ada_w.shape
    tn = 1024
    return pl.pallas_call(
        _ada_kernel,
        out_shape=jax.ShapeDtypeStruct((L, cc.shape[0], D3), F32),
        grid=(L, D3 // tn),
        in_specs=[
            pl.BlockSpec((cc.shape[0], D), lambda l, j: (0, 0)),
            pl.BlockSpec((None, D, tn), lambda l, j: (l, 0, j)),
            pl.BlockSpec((None, 1, tn), lambda l, j: (l, 0, j)),
        ],
        out_specs=pl.BlockSpec((None, cc.shape[0], tn), lambda l, j: (l, 0, j)),
        compiler_params=_cparams(("parallel", "parallel")),
        name="ada_vectors",
    )(cc, ada_w, ada_b.reshape(L, 1, D3))


def _norm_proj_kernel(x_ref, tail_ref, g_ref, m_ref, w_ref, p_ref, h_ref, hs_ref, *, n_lat):
    first_col = pl.program_id(2) == 0
    last_row_blk = pl.program_id(1) == pl.num_programs(1) - 1

    def modulated(x, scale, shift):
        y = x * lax.rsqrt(jnp.mean(x * x, axis=-1, keepdims=True) + EPS) * g_ref[...]
        h = (y * (1.0 + scale) + shift).astype(BF16)
        hs_ref[...] = h
        h_ref[...] = h

    @pl.when(first_col & jnp.logical_not(last_row_blk))
    def _():
        modulated(x_ref[...], m_ref[0, 1:2, :], m_ref[0, 0:1, :])

    @pl.when(first_col & last_row_blk)
    def _():
        rows = tail_ref.shape[0]
        row = lax.broadcasted_iota(jnp.int32, (rows, 1), 0) + pl.program_id(1) * rows
        is_ctx = row >= n_lat
        modulated(tail_ref[...], jnp.where(is_ctx, m_ref[1, 1:2, :], m_ref[0, 1:2, :]),
                  jnp.where(is_ctx, m_ref[1, 0:1, :], m_ref[0, 0:1, :]))

    p_ref[...] = _dot(hs_ref[...], w_ref[...]).astype(p_ref.dtype)


def _norm_project(x_main, x_tail, g, modsel, w, n_lat, Nt):
    B, _, D = x_main.shape
    N = w.shape[1]
    last_main = Nt // MM_ROW_BLK - 2
    return pl.pallas_call(
        functools.partial(_norm_proj_kernel, n_lat=n_lat),
        out_shape=(jax.ShapeDtypeStruct((B, Nt, N), BF16), jax.ShapeDtypeStruct((B, Nt, D), BF16)),
        grid=(B, Nt // MM_ROW_BLK, N // MM_COL_BLK),
        in_specs=[
            pl.BlockSpec((None, MM_ROW_BLK, D), lambda b, i, j: (b, jnp.minimum(i, last_main), 0)),
            pl.BlockSpec((None, MM_ROW_BLK, D), lambda b, i, j: (b, 0, 0)),
            pl.BlockSpec((1, D), lambda b, i, j: (0, 0)),
            pl.BlockSpec((None, 2, 3, D), lambda b, i, j: (b, 0, 0, 0)),
            pl.BlockSpec((D, MM_COL_BLK), lambda b, i, j: (0, j)),
        ],
        out_specs=(
            pl.BlockSpec((None, MM_ROW_BLK, MM_COL_BLK), lambda b, i, j: (b, i, j)),
            pl.BlockSpec((None, MM_ROW_BLK, D), lambda b, i, j: (b, i, 0)),
        ),
        scratch_shapes=[pltpu.VMEM((MM_ROW_BLK, D), BF16)],
        compiler_params=_cparams(("parallel", "parallel", "arbitrary")),
        name="norm_project",
    )(x_main, x_tail, g.reshape(1, D), modsel, w)


def _pack_w_in(w_in_l):
    o_nat = W_MLA
    o_hg = o_nat + W_NAT
    o_gla = o_hg + W_HG
    o_z = o_gla + W_GLA
    D = w_in_l.shape[0]
    zeros = lambda n: jnp.zeros((D, n), w_in_l.dtype)
    cols = [
        w_in_l[:, o_hg:o_hg + W_HG],
        w_in_l[:, o_z:o_z + N_BRANCH * BRANCH_W],
        w_in_l[:, o_nat:o_nat + W_NAT],
        w_in_l[:, o_gla:o_gla + 1024],
        w_in_l[:, :W_MLA], zeros(512 - W_MLA),
        w_in_l[:, o_gla + 1024:o_gla + W_GLA], zeros(LANES - 2 * GLA_LOWRANK),
    ]
    return jnp.concatenate(cols, axis=1).astype(BF16)


MLA_TQ = 1024
MLA_TK = 768


def _mla_prep_kernel(p_ref, wq_ref, wk_ref, wv_ref, gcq_ref, gckv_ref, gq_ref, gk_ref, vone_ref,
                     c_ref, s_ref, qc_ref, kc_ref, v_ref, *, n_lat_blk):
    ctx_w = jnp.where(pl.program_id(1) == n_lat_blk, 1.0, 0.0)
    hw = MLA_HEADS * HEAD_PAD
    p = p_ref[...].astype(F32)
    cq = p[:, :Q_LORA]
    r = lax.rsqrt(jnp.mean(cq * cq, axis=-1, keepdims=True) + EPS)
    cqn = (cq * r * gcq_ref[...]).astype(BF16)
    qraw = _dot(cqn, wq_ref[...])
    kv = p[:, Q_LORA:]
    ckv = kv[:, :KV_LORA]
    r2 = lax.rsqrt(jnp.mean(ckv * ckv, axis=-1, keepdims=True) + EPS)
    lane = lax.broadcasted_iota(jnp.int32, kv.shape, 1)
    lhs = (kv * jnp.where(lane < KV_LORA, r2, 1.0) * gckv_ref[...]).astype(BF16)
    kraw = _dot(lhs, wk_ref[...])
    vraw = _dot(lhs, wv_ref[...]) + vone_ref[...]
    cs = c_ref[...]
    sn = s_ref[...]
    inv = 1.0 / MLA_QK
    for hd in range(MLA_HEADS):
        sl = slice(HEAD_PAD * hd, HEAD_PAD * (hd + 1))
        sw = slice(hw + HEAD_PAD * hd, hw + HEAD_PAD * (hd + 1))
        qh = qraw[:, sl]
        rq = lax.rsqrt(jnp.sum(qh * qh, axis=-1, keepdims=True) * inv + EPS)
        qn = qh * rq * gq_ref[0:1, :]
        q_rot = qn * cs + qraw[:, sw] * rq * gq_ref[1:2, :] * sn
        qc_ref[hd] = jnp.concatenate([q_rot, qn], axis=1).astype(qc_ref.dtype)
        kh = kraw[:, sl]
        rk = lax.rsqrt(jnp.sum(kh * kh, axis=-1, keepdims=True) * inv + EPS)
        kn = kh * rk * gk_ref[0:1, :]
        k_rot = kn * cs + kraw[:, sw] * rk * gk_ref[1:2, :] * sn
        kc_ref[hd] = jnp.concatenate([k_rot * (1.0 - ctx_w), kn * ctx_w], axis=1).astype(kc_ref.dtype)
        v_ref[hd] = vraw[:, sl].astype(v_ref.dtype)


def _mla_weights(w_uq, w_ukv, g_cq, g_ckv, g_q, g_k):
    H = MLA_HEADS
    half = MLA_ROPE // 2
    lo, mid, hi = MLA_NOPE, MLA_NOPE + half, MLA_QK

    def swapped(t):
        z = jnp.zeros_like(t)
        return z.at[..., lo:mid].set(-t[..., mid:hi]).at[..., mid:hi].set(t[..., lo:mid])

    def both(t):
        return jnp.concatenate([t, swapped(t)], axis=1).reshape(t.shape[0], 2 * H * HEAD_PAD).astype(BF16)

    wq = jnp.pad(w_uq.reshape(Q_LORA, H, MLA_QK), ((0, 0), (0, 0), (0, HEAD_PAD - MLA_QK)))
    wkv = w_ukv.reshape(KV_LORA, H, MLA_NOPE + MLA_V)
    wk = jnp.pad(wkv[:, :, :MLA_NOPE], ((0, 0), (0, 0), (0, HEAD_PAD - MLA_NOPE)))
    place = jnp.zeros((MLA_ROPE, H, HEAD_PAD), F32)
    place = place.at[jnp.arange(MLA_ROPE), :, MLA_NOPE + jnp.arange(MLA_ROPE)].set(1.0)
    wk = jnp.concatenate([wk, place, jnp.zeros((256 - KV_LORA - MLA_ROPE, H, HEAD_PAD), F32)], axis=0)
    wv = jnp.pad(wkv[:, :, MLA_NOPE:], ((0, 256 - KV_LORA), (0, 0), (0, HEAD_PAD - MLA_V)))
    wv = wv.reshape(256, H * HEAD_PAD).astype(BF16)
    vone = jnp.zeros((H, HEAD_PAD), F32).at[:, MLA_V].set(1.0).reshape(1, H * HEAD_PAD)
    gckv = jnp.concatenate([g_ckv, jnp.ones((256 - KV_LORA,), F32)]).reshape(1, 256)
    gq = jnp.pad(g_q * (MLA_QK ** -0.5 * LOG2E), (0, HEAD_PAD - MLA_QK))
    gk = jnp.pad(g_k, (0, HEAD_PAD - MLA_QK))
    unsign = jnp.where(jnp.arange(HEAD_PAD) < mid, -1.0, 1.0)
    gq2 = jnp.stack([gq, swapped(gq) * unsign], axis=0)
    gk2 = jnp.stack([gk, swapped(gk) * unsign], axis=0)
    return both(wq), both(wk), wv, g_cq.reshape(1, Q_LORA), gckv, gq2, gk2, vone


def _rope_tables(n_lat, n_ctx):
    quarter = MLA_ROPE // 4
    inv_freq = ROPE_BASE ** (-jnp.arange(quarter, dtype=F32) / quarter)
    t = jnp.arange(n_lat, dtype=jnp.int32)
    row = (t // GRID_W).astype(F32)
    col = (t % GRID_W).astype(F32)
    ang = jnp.concatenate([row[:, None] * inv_freq, col[:, None] * inv_freq], axis=-1)
    cos, sin = jnp.cos(ang), jnp.sin(ang)
    ones = jnp.ones((n_lat, MLA_NOPE), F32)
    tail = jnp.ones((n_lat, HEAD_PAD - MLA_QK), F32)
    zl = jnp.zeros((n_lat, MLA_NOPE), F32)
    zt = jnp.zeros((n_lat, HEAD_PAD - MLA_QK), F32)
    c = jnp.concatenate([ones, cos, cos, tail], axis=1)
    s = jnp.concatenate([zl, sin, sin, zt], axis=1)
    c = jnp.concatenate([c, jnp.ones((n_ctx, HEAD_PAD), F32)], axis=0)
    s = jnp.concatenate([s, jnp.zeros((n_ctx, HEAD_PAD), F32)], axis=0)
    return c, s


def _mla_prep(p3, mw, tabs):
    B, Nt, _ = p3.shape
    H = MLA_HEADS
    wq, wk, wv, gcq, gckv, gq, gk, vone = mw
    n_lat_blk = Nt // TOK_BLK - 1
    const = lambda shape: pl.BlockSpec(shape, lambda b, i: (0,) * len(shape))
    tab = pl.BlockSpec((TOK_BLK, HEAD_PAD), lambda b, i: (i, 0))
    out = lambda w: pl.BlockSpec((None, H, TOK_BLK, w), lambda b, i: (b, 0, i, 0))
    shp = lambda w: jax.ShapeDtypeStruct((B, H, Nt, w), BF16)
    return pl.pallas_call(
        functools.partial(_mla_prep_kernel, n_lat_blk=n_lat_blk),
        out_shape=(shp(2 * HEAD_PAD), shp(2 * HEAD_PAD), shp(HEAD_PAD)),
        grid=(B, Nt // TOK_BLK),
        in_specs=[
            pl.BlockSpec((None, TOK_BLK, 512), lambda b, i: (b, i, COL_MLA // 512)),
            const(wq.shape), const(wk.shape), const(wv.shape), const(gcq.shape), const(gckv.shape),
            const(gq.shape), const(gk.shape), const(vone.shape), tab, tab,
        ],
        out_specs=(out(2 * HEAD_PAD), out(2 * HEAD_PAD), out(HEAD_PAD)),
        compiler_params=_cparams(("parallel", "parallel")),
        name="mla_prep",
    )(p3, wq, wk, wv, gcq, gckv, gq, gk, vone, *tabs)


def _softmax_step(q, kc, vc, m, acc):
    s = _dot_nt(q, kc)
    m_new = jnp.maximum(m, jnp.max(s, axis=-1, keepdims=True))
    p = jnp.exp2((s - m_new).astype(BF16))
    acc = jnp.exp2(m - m_new) * acc + _dot(p, vc)
    return m_new, acc


def _mla_attn_kernel(qc_ref, kc_ref, v_ref, o_ref, *, n_chunks):
    tq = qc_ref.shape[1]

    def body(c, carry):
        off = pl.multiple_of(c * MLA_TK, MLA_TK)
        return tuple(
            _softmax_step(qc_ref[hh], kc_ref[hh, pl.ds(off, MLA_TK), :], v_ref[hh, pl.ds(off, MLA_TK), :], *carry[hh])
            for hh in range(2))

    m0 = jnp.full((tq, 1), NEG_BIG, F32)
    a0 = jnp.zeros((tq, HEAD_PAD), F32)
    carry = lax.fori_loop(0, n_chunks, body, ((m0, a0), (m0, a0)), unroll=True)
    outs = [acc[:, :MLA_V] / acc[:, MLA_V:MLA_V + 1] for _, acc in carry]
    o_ref[...] = jnp.concatenate(outs, axis=-1).astype(o_ref.dtype)


def _mla_ctx_kernel(qc_ref, kc_ref, v_ref, o_ref):
    outs = []
    for hh in range(2):
        s = _dot_nt(qc_ref[hh], kc_ref[hh])
        p = jnp.exp2(s - jnp.max(s, axis=-1, keepdims=True))
        acc = _dot(p.astype(BF16), v_ref[hh])
        outs.append(acc[:, :MLA_V] / acc[:, MLA_V:MLA_V + 1])
    o_ref[...] = jnp.concatenate(outs, axis=-1).astype(o_ref.dtype)


def _mla_attention(qc, kc, v, n_lat, need_ctx):
    B, H, Nt, _ = kc.shape
    n_ctx = Nt - n_lat
    o_lat = pl.pallas_call(
        functools.partial(_mla_attn_kernel, n_chunks=Nt // MLA_TK),
        out_shape=jax.ShapeDtypeStruct((B, n_lat, BRANCH_W), BF16),
        grid=(B, H // 2, n_lat // MLA_TQ),
        in_specs=[
            pl.BlockSpec((None, 2, MLA_TQ, 2 * HEAD_PAD), lambda b, g, i: (b, g, i, 0)),
            pl.BlockSpec((None, 2, Nt, 2 * HEAD_PAD), lambda b, g, i: (b, g, 0, 0)),
            pl.BlockSpec((None, 2, Nt, HEAD_PAD), lambda b, g, i: (b, g, 0, 0)),
        ],
        out_specs=pl.BlockSpec((None, MLA_TQ, 2 * MLA_V), lambda b, g, i: (b, i, g)),
        compiler_params=_cparams(("parallel", "parallel", "arbitrary")),
        name="mla_attention",
    )(qc, kc, v)
    if not need_ctx:
        return o_lat, None
    cblk = n_lat // n_ctx
    cspec = lambda w: pl.BlockSpec((None, 2, n_ctx, w), lambda b, g: (b, g, cblk, 0))
    o_ctx = pl.pallas_call(
        _mla_ctx_kernel,
        out_shape=jax.ShapeDtypeStruct((B, n_ctx, BRANCH_W), BF16),
        grid=(B, H // 2),
        in_specs=[cspec(2 * HEAD_PAD), cspec(2 * HEAD_PAD), cspec(HEAD_PAD)],
        out_specs=pl.BlockSpec((None, n_ctx, 2 * MLA_V), lambda b, g: (b, 0, g)),
        compiler_params=_cparams(("parallel", "parallel")),
        name="mla_attention_ctx",
    )(qc, kc, v)
    return o_lat, o_ctx


NAT_ROWS_BLK = 8
NAT_TQ = NAT_ROWS_BLK * GRID_W
NAT_WIN_ROWS = 16
NAT_WIN = NAT_WIN_ROWS * GRID_W
NAT_PREP_BLK = 1024


def _pair_norm(x, g, half0):
    x2 = x * x
    s0 = jnp.sum(jnp.where(half0, x2, 0.0), axis=-1, keepdims=True)
    s1 = jnp.sum(jnp.where(half0, 0.0, x2), axis=-1, keepdims=True)
    ms = jnp.where(half0, s0, s1) * (1.0 / NAT_DH)
    return x * lax.rsqrt(ms + EPS) * g


def _nat_kernel(q_ref, kl_ref, vl_ref, kc_ref, vc_ref, bias_ref, gq_ref, gk_ref, o_ref, kn_ref, vb_ref,
                *, n_lat, rows_total):
    rb = pl.program_id(2)
    half0 = lax.broadcasted_iota(jnp.int32, (1, LANES), 1) < NAT_DH

    @pl.when(rb == 0)
    def _():
        def body(c, carry):
            off = pl.multiple_of(c * NAT_PREP_BLK, NAT_PREP_BLK)
            kn_ref[pl.ds(off, NAT_PREP_BLK), :] = _pair_norm(
                kl_ref[pl.ds(off, NAT_PREP_BLK), :].astype(F32), gk_ref[...], half0).astype(BF16)
            vb_ref[pl.ds(off, NAT_PREP_BLK), :LANES] = vl_ref[pl.ds(off, NAT_PREP_BLK), :].astype(BF16)
            vb_ref[pl.ds(off, NAT_PREP_BLK), LANES:] = jnp.ones((NAT_PREP_BLK, LANES), BF16)
            return carry

        lax.fori_loop(0, n_lat // NAT_PREP_BLK, body, 0)
        kn_ref[n_lat:, :] = _pair_norm(kc_ref[...].astype(F32), gk_ref[...], half0).astype(BF16)
        vb_ref[n_lat:, :LANES] = vc_ref[...].astype(BF16)
        vb_ref[n_lat:, LANES:] = jnp.ones((vc_ref.shape[0], LANES), BF16)

    q = _pair_norm(q_ref[...].astype(F32), gq_ref[...], half0)
    ws = jnp.clip(rb * NAT_ROWS_BLK - NAT_KH // 2, 0, rows_total - NAT_WIN_ROWS)
    off = pl.multiple_of(ws * GRID_W, GRID_W)
    kw = kn_ref[pl.ds(off, NAT_WIN), :]
    vw = vb_ref[pl.ds(off, NAT_WIN), :]
    kc = kn_ref[n_lat:, :]
    vc = vb_ref[n_lat:, :]
    qms = [jnp.where(half0 if hh == 0 else jnp.logical_not(half0), q, 0.0).astype(BF16) for hh in range(2)]
    sws = [_dot_nt(qm, kw) + bias_ref[hh].astype(F32) for hh, qm in enumerate(qms)]
    scs = [_dot_nt(qm, kc) for qm in qms]
    ms = [jnp.maximum(jnp.max(sw, axis=-1, keepdims=True), jnp.max(sc, axis=-1, keepdims=True))
          for sw, sc in zip(sws, scs)]
    accs = [_dot(jnp.exp2((sw - m).astype(BF16)), vw) + _dot(jnp.exp2((sc - m).astype(BF16)), vc)
            for sw, sc, m in zip(sws, scs, ms)]
    outs = [acc[:, :LANES] / acc[:, LANES:] for acc in accs]
    o_ref[...] = jnp.where(half0, outs[0], outs[1]).astype(o_ref.dtype)


def _nat_ctx_kernel(q_ref, kc_ref, vc_ref, gq_ref, gk_ref, o_ref):
    half0 = lax.broadcasted_iota(jnp.int32, (1, LANES), 1) < NAT_DH
    q = _pair_norm(q_ref[...].astype(F32), gq_ref[...], half0)
    kc = _pair_norm(kc_ref[...].astype(F32), gk_ref[...], half0).astype(BF16)
    vc = vc_ref[...].astype(BF16)
    outs = []
    for hh in range(2):
        sel = half0 if hh == 0 else jnp.logical_not(half0)
        s = _dot_nt(jnp.where(sel, q, 0.0).astype(BF16), kc)
        p = jnp.exp2(s - jnp.max(s, axis=-1, keepdims=True))
        outs.append(_dot(p.astype(BF16), vc) / jnp.sum(p, axis=-1, keepdims=True))
    o_ref[...] = jnp.where(half0, outs[0], outs[1]).astype(o_ref.dtype)


def _nat_bias_tables(rpb, rows_total):
    rbs = np.array([0, min(NAT_ROWS_BLK, rows_total - NAT_ROWS_BLK), rows_total - NAT_ROWS_BLK])
    ws = np.clip(rbs - NAT_KH // 2, 0, rows_total - NAT_WIN_ROWS)
    qrow = rbs[:, None] + np.arange(NAT_ROWS_BLK)[None, :]
    r0 = np.clip(qrow - NAT_KH // 2, 0, rows_total - NAT_KH)
    krow = ws[:, None] + np.arange(NAT_WIN_ROWS)[None, :]
    row_ok = (krow[:, None, :] >= r0[:, :, None]) & (krow[:, None, :] < r0[:, :, None] + NAT_KH)
    dr = np.clip(krow[:, None, :] - qrow[:, :, None] + NAT_KH - 1, 0, 2 * NAT_KH - 2)
    col = np.arange(GRID_W)
    c0 = np.clip(col - NAT_KW // 2, 0, GRID_W - NAT_KW)
    col_ok = (col[None, :] >= c0[:, None]) & (col[None, :] < c0[:, None] + NAT_KW)
    dc = np.clip(col[None, :] - col[:, None], -(NAT_KW - 1), NAT_KW - 1) + NAT_KW - 1
    oh_c = np.eye(2 * NAT_KW - 1, dtype=np.float32)[dc]
    by_col = jnp.einsum('hrc,qkc->hrqk', rpb.astype(F32), oh_c, precision=lax.Precision.HIGHEST)
    by_col = jnp.where(jnp.asarray(col_ok), by_col * LOG2E, NEG_BIG).astype(BF16)
    masked = jnp.full(by_col[:, 0].shape, NEG_BIG, BF16)
    variants = []
    for v in range(3):
        rows = [jnp.concatenate([by_col[:, dr[v, i, j]] if row_ok[v, i, j] else masked
                                 for j in range(NAT_WIN_ROWS)], axis=-1) for i in range(NAT_ROWS_BLK)]
        variants.append(jnp.concatenate(rows, axis=-2))
    return jnp.stack(variants).reshape(3, NAT_HEADS // 2, 2, NAT_TQ, NAT_WIN)


def _nat_attention(p3, rpb, g_q, g_k, n_lat, need_ctx):
    B, Nt, _ = p3.shape
    n_ctx = Nt - n_lat
    rows_total = n_lat // GRID_W
    n_rb = rows_total // NAT_ROWS_BLK
    bias = _nat_bias_tables(rpb, rows_total)
    gq = jnp.tile(g_q * (NAT_DH ** -0.5 * LOG2E), 2).reshape(1, LANES)
    gk = jnp.tile(g_k, 2).reshape(1, LANES)
    cq, ck, cv = ((COL_NAT + j * BRANCH_W) // LANES for j in range(3))
    cblk = n_lat // n_ctx

    def variant(rb):
        return jnp.where(rb == 0, 0, jnp.where(rb == n_rb - 1, 2, 1))

    vec = pl.BlockSpec((1, LANES), lambda b, g, rb: (0, 0))
    o_lat = pl.pallas_call(
        functools.partial(_nat_kernel, n_lat=n_lat, rows_total=rows_total),
        out_shape=jax.ShapeDtypeStruct((B, n_lat, BRANCH_W), BF16),
        grid=(B, NAT_HEADS // 2, n_rb),
        in_specs=[
            pl.BlockSpec((None, NAT_TQ, LANES), lambda b, g, rb: (b, rb, cq + g)),
            pl.BlockSpec((None, n_lat, LANES), lambda b, g, rb: (b, 0, ck + g)),
            pl.BlockSpec((None, n_lat, LANES), lambda b, g, rb: (b, 0, cv + g)),
            pl.BlockSpec((None, n_ctx, LANES), lambda b, g, rb: (b, cblk, ck + g)),
            pl.BlockSpec((None, n_ctx, LANES), lambda b, g, rb: (b, cblk, cv + g)),
            pl.BlockSpec((None, None, 2, NAT_TQ, NAT_WIN), lambda b, g, rb: (variant(rb), g, 0, 0, 0)),
            vec, vec,
        ],
        out_specs=pl.BlockSpec((None, NAT_TQ, LANES), lambda b, g, rb: (b, rb, g)),
        scratch_shapes=[pltpu.VMEM((Nt, LANES), BF16), pltpu.VMEM((Nt, 2 * LANES), BF16)],
        compiler_params=_cparams(("parallel", "parallel", "arbitrary")),
        name="nat_attention",
    )(p3, p3, p3, p3, p3, bias, gq, gk)
    if not need_ctx:
        return o_lat, None
    vec2 = pl.BlockSpec((1, LANES), lambda b, g: (0, 0))
    o_ctx = pl.pallas_call(
        _nat_ctx_kernel,
        out_shape=jax.ShapeDtypeStruct((B, n_ctx, BRANCH_W), BF16),
        grid=(B, NAT_HEADS // 2),
        in_specs=[
            pl.BlockSpec((None, n_ctx, LANES), lambda b, g: (b, cblk, cq + g)),
            pl.BlockSpec((None, n_ctx, LANES), lambda b, g: (b, cblk, ck + g)),
            pl.BlockSpec((None, n_ctx, LANES), lambda b, g: (b, cblk, cv + g)),
            vec2, vec2,
        ],
        out_specs=pl.BlockSpec((None, n_ctx, LANES), lambda b, g: (b, 0, g)),
        compiler_params=_cparams(("parallel", "parallel")),
        name="nat_attention_ctx",
    )(p3, p3, p3, gq, gk)
    return o_lat, o_ctx


def _log_sigmoid(x):
    return jnp.minimum(x, 0.0) - jnp.log(1.0 + jnp.exp(-jnp.abs(x)))


def _run_block(operands, s_ref, o_ref, rev):
    n_ch, nsub = o_ref.shape[0] // CHUNK, CHUNK // SUB
    r64 = lax.broadcasted_iota(jnp.int32, (CHUNK, CHUNK), 0)
    c64 = lax.broadcasted_iota(jnp.int32, (CHUNK, CHUNK), 1)
    causal = (r64 <= c64) if rev else (r64 >= c64)
    tri = jnp.where(causal, 1.0, 0.0).astype(BF16)
    first = [SUB * i + (SUB - 1 if rev else 0) for i in range(nsub)]
    last = [SUB * i + (0 if rev else SUB - 1) for i in range(nsub)]
    loaded = []
    for c in range(n_ch):
        qs, k, v, a, heads = operands(slice(CHUNK * c, CHUNK * (c + 1)))
        hi = a.astype(BF16)
        mid = (a - hi.astype(F32)).astype(BF16)
        loaded.append((qs, k, v, a, _dot(tri, hi) + _dot(tri, mid)))
        yield
    staged = []
    for c in range(n_ch):
        qs, k, v, a, bc = loaded[c]
        W = qs.shape[1]
        bex = bc - a
        r_start = [bex[r:r + 1, :] for r in first]
        r_end = [bc[r:r + 1, :] for r in last]
        bend = r_end[0] if rev else r_end[nsub - 1]

        def spread(rows):
            return jnp.concatenate([jnp.broadcast_to(r, (SUB, W)) for r in rows], axis=0)

        rs_full = spread(r_start)
        qt = qs * jnp.exp(bc - rs_full)
        q_in = qt * spread([jnp.exp(r) for r in r_start])
        ke = k * jnp.exp(spread(r_end) - bc)
        k_out = ke * spread([jnp.exp(bend - r) for r in r_end])
        kd = k * jnp.exp(jnp.minimum(rs_full - bc, EXP_CAP))
        zeros = jnp.zeros((SUB, W), F32)
        kts = []
        for i in range(nsub):
            rows = []
            for j in range(nsub):
                blk = slice(SUB * j, SUB * (j + 1))
                if j == i:
                    rows.append(kd[blk])
                elif (j > i) if rev else (j < i):
                    rows.append(ke[blk] if abs(i - j) == 1 else ke[blk] * jnp.exp(r_start[i] - r_end[j]))
                else:
                    rows.append(zeros)
            kts.append(jnp.concatenate(rows, axis=0).astype(BF16))
        vb = v.astype(BF16)
        intra, incr, q_state = [], [], []
        for ks, kmask, vs in heads:
            ksl = slice(ks, ks + LANES)
            vsl = slice(vs, vs + LANES)

            def msk(t, kmask=kmask):
                return t if kmask is None else jnp.where(kmask, t, 0.0)

            qt_h = msk(qt[:, ksl]).astype(BF16)
            q_cat = jnp.concatenate([jnp.where(r64[:, :1] // SUB == i, qt_h, jnp.zeros_like(qt_h))
                                     for i in range(nsub)], axis=1)
            k_cat = jnp.concatenate([kts[i][:, ksl] for i in range(nsub)], axis=1)
            sc = jnp.where(causal, _dot_nt(q_cat, k_cat), 0.0).astype(BF16)
            vt = v[:, vsl].T.astype(BF16)
            intra.append(vt)
            incr.append(_dot(vt, msk(k_out[:, ksl]).astype(BF16)))
            q_state.append(jnp.concatenate([msk(q_in[:, ksl]).astype(BF16), sc], axis=1))
        staged.append((intra, incr, q_state, jnp.exp(bend)))
        yield
    states = [s_ref[hd] for hd in range(len(heads))]
    for c in (range(n_ch - 1, -1, -1) if rev else range(n_ch)):
        intra, incr, q_state, dec = staged[c]
        outs = []
        for hd, (ks, _, _) in enumerate(heads):
            outs.append(_dot_nt(q_state[hd], jnp.concatenate([states[hd].astype(BF16), intra[hd]], axis=1)))
            states[hd] = states[hd] * dec[:, ks:ks + LANES] + incr[hd]
        o_ref[CHUNK * c:CHUNK * (c + 1), :] = jnp.concatenate(outs, axis=-1).astype(o_ref.dtype)
        yield
    for hd, st in enumerate(states):
        s_ref[hd] = st


def _hgrn2_operands(q_ref, f_ref, v_ref, lb, rows):
    q = q_ref[rows, :].astype(F32)
    f = f_ref[rows, :].astype(F32)
    loglb, log1mlb, oneml = lb[0:1, :], lb[1:2, :], lb[2:3, :]
    u = log1mlb + _log_sigmoid(f)
    a = jnp.maximum(u, loglb) + jnp.log(1.0 + jnp.exp(-jnp.abs(u - loglb)))
    k = oneml * _sigmoid(-f)
    qs = q * _sigmoid(q) * (HG_DK ** -0.5)
    heads = [(LANES * h, None, LANES * h) for h in range(HG_HEADS)]
    return qs, k, v_ref[rows, :].astype(F32), a, heads


def _gla_operands(q_ref, k_ref, v_ref, r_ref, w2, b2, rows):
    g = _dot(r_ref[rows, :].astype(BF16), w2) + b2
    a = _log_sigmoid(g) * (1.0 / GLA_NORMALIZER)
    qs = q_ref[rows, :].astype(F32) * (GLA_DK ** -0.5)
    half0 = lax.broadcasted_iota(jnp.int32, (1, LANES), 1) < GLA_DK
    heads = [(LANES * (h // 2), half0 if h % 2 == 0 else jnp.logical_not(half0), LANES * h)
             for h in range(GLA_HEADS)]
    return qs, k_ref[rows, :].astype(F32), v_ref[rows, :].astype(F32), a, heads


def _scans_kernel(hq_f, hf_f, hv_f, hq_b, hf_b, hv_b, lb_ref, gq_f, gk_f, gv_f, gr_f, gq_b, gk_b, gv_b, gr_b,
                  w2_ref, b2_ref, oh_f, oh_b, og_f, og_b, s_ref):
    @pl.when(pl.program_id(1) == 0)
    def _():
        s_ref[...] = jnp.zeros_like(s_ref)

    streams = [
        _run_block(functools.partial(_hgrn2_operands, hq_f, hf_f, hv_f, lb_ref[0]), s_ref.at[0], oh_f, False),
        _run_block(functools.partial(_hgrn2_operands, hq_b, hf_b, hv_b, lb_ref[1]), s_ref.at[1], oh_b, True),
        _run_block(functools.partial(_gla_operands, gq_f, gk_f, gv_f, gr_f, w2_ref[0], b2_ref[0]),
                   s_ref.at[2], og_f, False),
        _run_block(functools.partial(_gla_operands, gq_b, gk_b, gv_b, gr_b, w2_ref[1], b2_ref[1]),
                   s_ref.at[3], og_b, True),
    ]
    done = object()
    while streams:
        streams = [g for g in streams if next(g, done) is not done]


def _linear_scans(p3, lbvec, w2pad, b2, n_lat):
    B, Nt, _ = p3.shape
    nb = Nt // TOK_BLK
    n_lat_blk = n_lat // TOK_BLK
    fwd = lambda s: jnp.where(s == 0, n_lat_blk, s - 1)
    bwd = lambda s: jnp.where(s == 0, n_lat_blk, n_lat_blk - s)
    wqk = GLA_HEADS * GLA_DK
    spec = lambda blk, w, col: pl.BlockSpec((None, TOK_BLK, w), lambda b, s: (b, blk(s), col // w))
    hg = lambda blk, f_col: [spec(blk, BRANCH_W, COL_HG), spec(blk, BRANCH_W, f_col),
                             spec(blk, BRANCH_W, COL_HG + 3 * BRANCH_W)]
    gla = lambda blk: [spec(blk, wqk, COL_GLA), spec(blk, wqk, COL_GLA + wqk), spec(blk, BRANCH_W, COL_GLA + 2 * wqk),
                       spec(blk, LANES, COL_GLR)]
    const = lambda shape: pl.BlockSpec(shape, lambda b, s: (0,) * len(shape))
    out = lambda blk: pl.BlockSpec((None, TOK_BLK, BRANCH_W), lambda b, s: (b, blk(s), 0))
    shp = jax.ShapeDtypeStruct((B, Nt, BRANCH_W), BF16)
    return pl.pallas_call(
        _scans_kernel,
        out_shape=(shp, shp, shp, shp),
        grid=(B, nb),
        in_specs=(hg(fwd, COL_HG + BRANCH_W) + hg(bwd, COL_HG + 2 * BRANCH_W) + [const(lbvec.shape)]
                  + gla(fwd) + gla(bwd) + [const(w2pad.shape), const(b2.shape)]),
        out_specs=(out(fwd), out(bwd), out(fwd), out(bwd)),
        scratch_shapes=[pltpu.VMEM((4, HG_HEADS, LANES, LANES), F32)],
        compiler_params=_cparams(("parallel", "arbitrary")),
        name="linear_scans",
    )(*([p3] * 6), lbvec, *([p3] * 8), w2pad, b2)


def _group_norm(o, g):
    parts = []
    for hd in range(o.shape[1] // LANES):
        oh = o[:, LANES * hd:LANES * (hd + 1)]
        parts.append(oh * lax.rsqrt(jnp.mean(oh * oh, axis=-1, keepdims=True) + EPS))
    return jnp.concatenate(parts, axis=-1) * g


def _merge_kernel(x_ref, h_ref, za_ref, zb_ref, zc_ref, zd_ref, ya_ref, yb_ref, hf_ref, hb_ref, gf_ref,
                  gb_ref, wbr_ref, wmg_ref, bmg_ref, wout_ref, ghg_ref, ggl_ref, m_ref, o_ref):
    h = h_ref[...]
    ys = (
        ya_ref[...].astype(F32),
        yb_ref[...].astype(F32),
        _group_norm(hf_ref[...].astype(F32) + hb_ref[...].astype(F32), ghg_ref[...]),
        _group_norm(gf_ref[...].astype(F32) + gb_ref[...].astype(F32), ggl_ref[...]),
    )
    zs = (za_ref, zb_ref, zc_ref, zd_ref)
    acc = None
    for br in range(N_BRANCH):
        gate = _sigmoid(_dot(h, wmg_ref[br]) + bmg_ref[br])
        z = zs[br][...].astype(F32)
        yz = (ys[br] * (z * _sigmoid(z))).astype(BF16)
        part = gate * _dot(yz, wbr_ref[br])
        acc = part if acc is None else acc + part
    out = _dot(acc.astype(BF16), wout_ref[...])
    o_ref[...] = x_ref[...] + m_ref[2:3, :] * out


MERGE_LAT_BLK = 512


def _merge(x_part, y_a, y_b, h, p3, scans, w_br, w_merge, b_merge, w_out, g_hg, g_gla, modsel, part, n_lat):
    B, _, D = h.shape
    tm = TOK_BLK if part else MERGE_LAT_BLK
    n_blk = 1 if part else n_lat // tm
    blk0 = n_lat // tm if part else 0
    own = lambda w: pl.BlockSpec((None, tm, w), lambda b, i: (b, i, 0))
    stream = lambda w, col=0: pl.BlockSpec((None, tm, w), lambda b, i: (b, blk0 + i, col // w))
    once = pl.Buffered(1)
    const = lambda shape: pl.BlockSpec(shape, lambda b, i: (0,) * len(shape), pipeline_mode=once)
    ghg = jnp.tile(g_hg, HG_HEADS).reshape(1, BRANCH_W)
    ggl = jnp.tile(g_gla, GLA_HEADS).reshape(1, BRANCH_W)
    bm = b_merge.reshape(N_BRANCH, 1, D)
    return pl.pallas_call(
        _merge_kernel,
        out_shape=jax.ShapeDtypeStruct((B, n_blk * tm, D), F32),
        grid=(B, n_blk),
        in_specs=[own(D), stream(D)] + [stream(BRANCH_W, COL_Z + br * BRANCH_W) for br in range(N_BRANCH)]
        + [own(BRANCH_W)] * 2 + [stream(BRANCH_W)] * 4 + [
            const(w_br.shape), const(w_merge.shape), const(bm.shape), const(w_out.shape),
            const(ghg.shape), const(ggl.shape),
            pl.BlockSpec((None, None, 3, D), lambda b, i: (b, part, 0, 0)),
        ],
        out_specs=own(D),
        compiler_params=_cparams(("parallel", "parallel"), vmem_mb=56),
        name="branch_merge_ctx" if part else "branch_merge",
    )(x_part, h, p3, p3, p3, p3, y_a, y_b, *scans, w_br, w_merge, bm, w_out, ghg, ggl, modsel)


def kernel(x, c, ctx, c_ctx, ada_w, ada_b, norm_g, w_in, mla_w_uq, mla_w_ukv, mla_g_cq, mla_g_ckv, mla_g_q,
           mla_g_k, nat_rpb, nat_g_q, nat_g_k, hg_lb_logits, hg_g_o, gla_w2, gla_b2, gla_g_o, w_br, w_merge,
           b_merge, w_out):
    B, n_lat, D = x.shape
    n_ctx = ctx.shape[1]
    depth = ada_w.shape[0]
    Nt = n_lat + n_ctx
    n_lat_blk = n_lat // TOK_BLK
    assert n_ctx == TOK_BLK and n_lat % NAT_PREP_BLK == 0 and Nt % MM_ROW_BLK == 0
    assert Nt % MLA_TK == 0 and n_lat % MLA_TQ == 0

    cc = jnp.zeros((8, D), F32).at[:B].set(c).at[B].set(c_ctx)
    ada = _ada_vectors(cc, ada_w.astype(BF16), ada_b).reshape(depth, 8, 3, D)
    sm = jax.nn.softmax(hg_lb_logits.astype(F32), axis=0)
    lower = jnp.maximum(jnp.cumsum(sm, axis=0) - sm[0], 0.0)
    tabs = _rope_tables(n_lat, n_ctx)

    x_lat, x_ctx = x, ctx
    for l in range(depth):
        need_ctx = l < depth - 1
        modsel = jnp.stack([ada[l, :B], jnp.broadcast_to(ada[l, B], (B, 3, D))], axis=1)
        x_tail = jnp.concatenate([x_lat[:, n_lat - (MM_ROW_BLK - n_ctx):n_lat], x_ctx], axis=1)
        p3, h = _norm_project(x_lat, x_tail, norm_g[l], modsel, _pack_w_in(w_in[l]), n_lat, Nt)

        mw = _mla_weights(mla_w_uq[l], mla_w_ukv[l], mla_g_cq[l], mla_g_ckv[l], mla_g_q[l], mla_g_k[l])
        y_a = _mla_attention(*_mla_prep(p3, mw, tabs), n_lat, need_ctx)
        y_b = _nat_attention(p3, nat_rpb[l], nat_g_q[l], nat_g_k[l], n_lat, need_ctx)

        lb = lower[l]
        lbvec = jnp.stack([jnp.log(lb), jnp.log1p(-lb), 1.0 - lb], axis=1)
        w2pad = jnp.zeros((2, LANES, GLA_HEADS * GLA_DK), F32)
        for d in range(2):
            w2pad = w2pad.at[d, GLA_LOWRANK * d:GLA_LOWRANK * (d + 1)].set(gla_w2[l, d])
        scans = _linear_scans(p3, lbvec, w2pad.astype(BF16), gla_b2[l].reshape(2, 1, -1), n_lat)

        wts = (w_br[l].astype(BF16), w_merge[l].astype(BF16), b_merge[l], w_out[l].astype(BF16), hg_g_o[l], gla_g_o[l])
        parts = [_merge(x_part, y_a[part], y_b[part], h, p3, scans, *wts, modsel, part, n_lat)
                 for part, x_part in enumerate((x_lat, x_ctx)) if part == 0 or need_ctx]
        x_lat, x_ctx = parts[0], (parts[1] if need_ctx else None)
    return x_lat
```
